```python
import jax, jax.numpy as jnp
from jax import lax
import numpy as np

D_MODEL = 2048
BATCH = 1
SEQ = 8192
DEPTH = 2

CHUNK = 64
Q_BLOCK = 128
N_MEM = 256
EPS = 1e-6

D_CONV = D_MODEL
CONV_K = 31

MLA_HEADS = 16
Q_LORA = 512
KV_LORA = 512
QK_NOPE = 128
QK_ROPE = 64
V_DIM = 128
ROPE_THETA = 10000.0

X_HEADS = 4
X_HEAD_DIM = 128

D_FF = 4 * D_MODEL

N_BRANCH = 2
IN_SIZES = (2 * D_CONV, Q_LORA, KV_LORA, QK_ROPE, N_BRANCH * D_MODEL)
IN_COLS = sum(IN_SIZES)
IN_SPLITS = tuple(int(v) for v in np.cumsum(IN_SIZES)[:-1])

kernel_name = "hybrid_conformer_mla_gated_encoder"


def rms_norm(x, g):
    x32 = x.astype(jnp.float32)
    y = x32 * lax.rsqrt(jnp.mean(x32 * x32, axis=-1, keepdims=True) + EPS)
    return (y * g.astype(jnp.float32)).astype(x.dtype)


def layer_norm(x, g, b):
    x32 = x.astype(jnp.float32)
    mu = jnp.mean(x32, axis=-1, keepdims=True)
    xc = x32 - mu
    y = xc * lax.rsqrt(jnp.mean(xc * xc, axis=-1, keepdims=True) + EPS)
    return (y * g.astype(jnp.float32) + b.astype(jnp.float32)).astype(x.dtype)


def rope_tables(positions):
    inv_freq = 1.0 / (ROPE_THETA ** (jnp.arange(0, QK_ROPE, 2, dtype=jnp.float32) / QK_ROPE))
    ang = positions.astype(jnp.float32)[..., None] * inv_freq
    return jnp.cos(ang), jnp.sin(ang)


def apply_rope(x, cos, sin):
    x32 = x.astype(jnp.float32)
    x1, x2 = jnp.split(x32, 2, axis=-1)
    out = jnp.concatenate([x1 * cos - x2 * sin, x2 * cos + x1 * sin], axis=-1)
    return out.astype(x.dtype)


def conformer_conv_branch(u, conv_w, conv_b, ln_g, ln_b, w_pw2):
    a, gate = jnp.split(u, 2, axis=-1)
    h = a * jax.nn.sigmoid(gate)
    h = lax.conv_general_dilated(
        h, conv_w[:, None, :], window_strides=(1,), padding=[(CONV_K - 1, 0)],
        dimension_numbers=("NWC", "WIO", "NWC"), feature_group_count=D_CONV) + conv_b
    h = jax.nn.silu(layer_norm(h, ln_g, ln_b))
    return h @ w_pw2


def chunk_causal_attention(q, k, v):
    B, S, H, Dk = q.shape
    nb = S // Q_BLOCK
    scale = Dk ** -0.5
    q_blocks = q.reshape(B, nb, Q_BLOCK, H, Dk).transpose(1, 0, 2, 3, 4)
    k_chunk = jnp.arange(S) // CHUNK

    def one_block(args):
        q_blk, blk = args
        s = jnp.einsum("bqhd,bkhd->bhqk", q_blk, k, preferred_element_type=jnp.float32) * scale
        q_chunk = (blk * Q_BLOCK + jnp.arange(Q_BLOCK)) // CHUNK
        mask = k_chunk[None, :] <= q_chunk[:, None]
        s = jnp.where(mask[None, None], s, -jnp.inf)
        p = jax.nn.softmax(s, axis=-1).astype(v.dtype)
        return jnp.einsum("bhqk,bkhd->bqhd", p, v)

    o = lax.map(one_block, (q_blocks, jnp.arange(nb)))
    return o.transpose(1, 0, 2, 3, 4).reshape(B, S, H, v.shape[-1])


def mla_branch(c_q, c_kv, k_pe, cos, sin, q_norm_g, w_uq, kv_norm_g, w_ukv, w_o):
    B, S, _ = c_q.shape
    q = (rms_norm(c_q, q_norm_g) @ w_uq).reshape(B, S, MLA_HEADS, QK_NOPE + QK_ROPE)
    q_nope, q_pe = q[..., :QK_NOPE], q[..., QK_NOPE:]
    q_pe = apply_rope(q_pe, cos[:, :, None, :], sin[:, :, None, :])
    kv = (rms_norm(c_kv, kv_norm_g) @ w_ukv).reshape(B, S, MLA_HEADS, QK_NOPE + V_DIM)
    k_nope, v = kv[..., :QK_NOPE], kv[..., QK_NOPE:]
    k_pe = apply_rope(k_pe, cos, sin)
    q_full = jnp.concatenate([q_nope, q_pe], axis=-1)
    k_full = jnp.concatenate(
        [k_nope, jnp.broadcast_to(k_pe[:, :, None, :], (B, S, MLA_HEADS, QK_ROPE))], axis=-1)
    o = chunk_causal_attention(q_full, k_full, v)
    return o.reshape(B, S, MLA_HEADS * V_DIM) @ w_o


def memory_cross_attention(h, mem_n, w_xq, w_xkv, w_xo):
    B, S, _ = h.shape
    q = (h @ w_xq).reshape(B, S, X_HEADS, X_HEAD_DIM)
    kv = (mem_n @ w_xkv).reshape(B, mem_n.shape[1], 2, X_HEADS, X_HEAD_DIM)
    k, v = kv[:, :, 0], kv[:, :, 1]
    s = jnp.einsum("bqhd,bmhd->bhqm", q, k, preferred_element_type=jnp.float32) * (X_HEAD_DIM ** -0.5)
    p = jax.nn.softmax(s, axis=-1).astype(v.dtype)
    o = jnp.einsum("bhqm,bmhd->bqhd", p, v).reshape(B, S, X_HEADS * X_HEAD_DIM)
    return o @ w_xo


def squared_relu_mlp(h, w1, w2):
    return jnp.square(jax.nn.relu(h @ w1)) @ w2


def setup_inputs(seed: int = 0) -> dict:
    key = jax.random.key(seed)
    ks = iter(jax.random.split(key, 40))

    def w(shape, fan_in):
        return jax.random.normal(next(ks), shape, jnp.float32) * (fan_in ** -0.5)

    def gain(shape):
        return 1.0 + 0.02 * jax.random.normal(next(ks), shape, jnp.float32)

    def bias(shape):
        return 0.02 * jax.random.normal(next(ks), shape, jnp.float32)

    L = DEPTH
    return {
        "x": jax.random.normal(next(ks), (BATCH, SEQ, D_MODEL), jnp.float32),
        "mem": jax.random.normal(next(ks), (BATCH, N_MEM, D_MODEL), jnp.float32),
        "positions": jnp.broadcast_to(jnp.arange(SEQ, dtype=jnp.int32), (BATCH, SEQ)),
        "norm_mix_g": gain((L, D_MODEL)),
        "w_in": w((L, D_MODEL, IN_COLS), D_MODEL),
        "b_gate": bias((L, N_BRANCH * D_MODEL)),
        "conv_w": w((L, CONV_K, D_CONV), CONV_K),
        "conv_b": bias((L, D_CONV)),
        "conv_ln_g": gain((L, D_CONV)),
        "conv_ln_b": bias((L, D_CONV)),
        "w_conv_out": w((L, D_CONV, D_MODEL), D_CONV),
        "q_norm_g": gain((L, Q_LORA)),
        "w_uq": w((L, Q_LORA, MLA_HEADS * (QK_NOPE + QK_ROPE)), Q_LORA),
        "kv_norm_g": gain((L, KV_LORA)),
        "w_ukv": w((L, KV_LORA, MLA_HEADS * (QK_NOPE + V_DIM)), KV_LORA),
        "w_mla_out": w((L, MLA_HEADS * V_DIM, D_MODEL), MLA_HEADS * V_DIM),
        "w_out": w((L, D_MODEL, D_MODEL), D_MODEL),
        "norm_mem_g": gain((L, D_MODEL)),
        "mem_norm_g": gain((L, D_MODEL)),
        "w_xq": w((L, D_MODEL, X_HEADS * X_HEAD_DIM), D_MODEL),
        "w_xkv": w((L, D_MODEL, 2 * X_HEADS * X_HEAD_DIM), D_MODEL),
        "w_xo": w((L, X_HEADS * X_HEAD_DIM, D_MODEL), X_HEADS * X_HEAD_DIM),
        "norm_ffn_g": gain((L, D_MODEL)),
        "w_ff1": w((L, D_MODEL, D_FF), D_MODEL),
        "w_ff2": w((L, D_FF, D_MODEL), D_FF),
        "final_norm_g": gain((D_MODEL,)),
    }


def reference(x, mem, positions, norm_mix_g, w_in, b_gate, conv_w, conv_b, conv_ln_g, conv_ln_b,
              w_conv_out, q_norm_g, w_uq, kv_norm_g, w_ukv, w_mla_out, w_out,
              norm_mem_g, mem_norm_g, w_xq, w_xkv, w_xo, norm_ffn_g, w_ff1, w_ff2, final_norm_g):
    B, S, D = x.shape
    cos, sin = rope_tables(positions)
    for l in range(DEPTH):
        h = rms_norm(x, norm_mix_g[l])
        z = h @ w_in[l]
        u_conv, c_q, c_kv, k_pe, gate_logits = jnp.split(z, IN_SPLITS, axis=-1)
        y_conv = conformer_conv_branch(u_conv, conv_w[l], conv_b[l], conv_ln_g[l], conv_ln_b[l],
                                       w_conv_out[l])
        y_mla = mla_branch(c_q, c_kv, k_pe, cos, sin, q_norm_g[l], w_uq[l], kv_norm_g[l],
                           w_ukv[l], w_mla_out[l])
        gates = jax.nn.sigmoid(gate_logits + b_gate[l]).reshape(B, S, N_BRANCH, D)
        merged = gates[:, :, 0, :] * y_conv + gates[:, :, 1, :] * y_mla
        x = x + merged @ w_out[l]
        x = x + memory_cross_attention(rms_norm(x, norm_mem_g[l]), rms_norm(mem, mem_norm_g[l]),
                                       w_xq[l], w_xkv[l], w_xo[l])
        x = x + squared_relu_mlp(rms_norm(x, norm_ffn_g[l]), w_ff1[l], w_ff2[l])
    return rms_norm(x, final_norm_g)
```

```python
import functools

import numpy as np
import jax
import jax.numpy as jnp
from jax import lax
from jax.experimental import pallas as pl
from jax.experimental.pallas import tpu as pltpu

D_MODEL = 2048
SEQ = 8192
DEPTH = 2
CHUNK = 64
N_MEM = 256
EPS = 1e-6
D_CONV = D_MODEL
CONV_K = 31
MLA_HEADS = 16
Q_LORA = 512
KV_LORA = 512
QK_NOPE = 128
QK_ROPE = 64
V_DIM = 128
ROPE_THETA = 10000.0
X_HEADS = 4
X_HEAD_DIM = 128
D_FF = 4 * D_MODEL

LANES = 128
QK_PAD = 256
CONV_HALO = 32
NEG_BIG = -1e30
VMEM_LIMIT = 56 * 1024 * 1024

F32 = jnp.float32
BF16 = jnp.bfloat16


def _cparams(*sem):
    return pltpu.CompilerParams(dimension_semantics=sem, vmem_limit_bytes=VMEM_LIMIT)


def _rms(x, g):
    return x * lax.rsqrt(jnp.mean(x * x, axis=-1, keepdims=True) + EPS) * g


def _rope_combine(z, table):
    t = z * table
    u = t + pltpu.roll(t, QK_ROPE, 1)
    lane = lax.broadcasted_iota(jnp.int32, u.shape, 1)
    return jnp.where(lane < QK_ROPE, u, 0.0)


def _rope_table_kernel(pos_ref, freq_ref, o_ref):
    ang = pos_ref[...].astype(F32) * freq_ref[...]
    lane = lax.broadcasted_iota(jnp.int32, ang.shape, 1)
    s = jnp.sin(ang)
    o_ref[...] = jnp.where(lane < 64, jnp.cos(ang), jnp.where(lane < 96, -s, s))


def rope_table(pos_col, freq_row):
    S = pos_col.shape[0]
    tm = 1024
    return pl.pallas_call(
        _rope_table_kernel,
        grid=(S // tm,),
        in_specs=[pl.BlockSpec((tm, 1), lambda i: (i, 0)),
                  pl.BlockSpec((1, LANES), lambda i: (0, 0))],
        out_specs=pl.BlockSpec((tm, LANES), lambda i: (i, 0)),
        out_shape=jax.ShapeDtypeStruct((S, LANES), F32),
        name="rope_table",
        compiler_params=_cparams("arbitrary"),
    )(pos_col, freq_row)


def _norm_kernel(x_ref, g_ref, o_ref):
    o_ref[...] = _rms(x_ref[...], g_ref[...]).astype(o_ref.dtype)


def rms_norm_rows(x, g, out_dtype, tm=512):
    M, D = x.shape
    return pl.pallas_call(
        _norm_kernel,
        grid=(M // tm,),
        in_specs=[pl.BlockSpec((tm, D), lambda i: (i, 0)),
                  pl.BlockSpec((1, D), lambda i: (0, 0))],
        out_specs=pl.BlockSpec((tm, D), lambda i: (i, 0)),
        out_shape=jax.ShapeDtypeStruct((M, D), out_dtype),
        name="rms_norm",
        compiler_params=_cparams("arbitrary"),
    )(x, g)


def _mm_kernel(*refs, n_b, n_extra, epilogue):
    a = refs[0][...]
    accs = [jnp.dot(a, b[...], preferred_element_type=F32) for b in refs[1:1 + n_b]]
    extras = [e[...] for e in refs[1 + n_b:1 + n_b + n_extra]]
    outs = epilogue(accs, extras)
    for o_ref, o in zip(refs[1 + n_b + n_extra:], outs):
        o_ref[...] = o.astype(o_ref.dtype)


def matmul(name, a, bs, extras, extra_kinds, out_dtypes, epilogue, tm, tn):
    M, K = a.shape
    N = bs[0].shape[1]
    in_specs = [pl.BlockSpec((tm, K), lambda i, j: (i, 0))]
    in_specs += [pl.BlockSpec((K, tn), lambda i, j: (0, j)) for _ in bs]
    for kind in extra_kinds:
        if kind == "row":
            in_specs.append(pl.BlockSpec((1, tn), lambda i, j: (0, j)))
        else:
            off = kind[1] // tn
            in_specs.append(pl.BlockSpec((tm, tn), lambda i, j, off=off: (i, j + off)))
    out_specs = [pl.BlockSpec((tm, tn), lambda i, j: (i, j)) for _ in out_dtypes]
    out_shape = [jax.ShapeDtypeStruct((M, N), dt) for dt in out_dtypes]
    return pl.pallas_call(
        functools.partial(_mm_kernel, n_b=len(bs), n_extra=len(extras), epilogue=epilogue),
        grid=(M // tm, N // tn),
        in_specs=in_specs,
        out_specs=out_specs,
        out_shape=out_shape,
        name=name,
        compiler_params=_cparams("arbitrary", "arbitrary"),
    )(a, *bs, *extras)


def _epi_glu(accs, extras):
    return [accs[0] * jax.nn.sigmoid(accs[1])]


def _epi_gate(accs, extras):
    return [jax.nn.sigmoid(accs[0] + extras[0])]


def _epi_mul(accs, extras):
    return [accs[0] * extras[0]]


def _epi_merge(accs, extras):
    return [extras[0] + extras[1] * accs[0]]


def _epi_resid_norm(accs, extras):
    x = extras[0] + accs[0]
    return [x, _rms(x, extras[1])]


def _latent_kernel(h_ref, w_ref, qg_ref, kvg_ref, rope_ref, cq_ref, ckv_ref, kpe_ref):
    z = jnp.dot(h_ref[...], w_ref[...], preferred_element_type=F32)
    cq_ref[...] = _rms(z[:, :Q_LORA], qg_ref[...]).astype(cq_ref.dtype)
    ckv_ref[...] = _rms(z[:, Q_LORA:Q_LORA + KV_LORA], kvg_ref[...]).astype(ckv_ref.dtype)
    kpe_ref[...] = _rope_combine(z[:, Q_LORA + KV_LORA:], rope_ref[...]).astype(kpe_ref.dtype)


def latent_proj(h, w, qg, kvg, rope, tm=512):
    M, K = h.shape
    N = w.shape[1]
    row = lambda n: pl.BlockSpec((1, n), lambda i: (0, 0))
    return pl.pallas_call(
        _latent_kernel,
        grid=(M // tm,),
        in_specs=[pl.BlockSpec((tm, K), lambda i: (i, 0)),
                  pl.BlockSpec((K, N), lambda i: (0, 0)),
                  row(Q_LORA), row(KV_LORA),
                  pl.BlockSpec((tm, LANES), lambda i: (i, 0))],
        out_specs=[pl.BlockSpec((tm, Q_LORA), lambda i: (i, 0)),
                   pl.BlockSpec((tm, KV_LORA), lambda i: (i, 0)),
                   pl.BlockSpec((tm, LANES), lambda i: (i, 0))],
        out_shape=[jax.ShapeDtypeStruct((M, Q_LORA), BF16),
                   jax.ShapeDtypeStruct((M, KV_LORA), BF16),
                   jax.ShapeDtypeStruct((M, LANES), BF16)],
        name="latent_proj",
        compiler_params=_cparams("arbitrary"),
    )(h, w, qg, kvg, rope)


def _qproj_kernel(c_ref, w_ref, rope_ref, q_ref):
    c = c_ref[...]
    table = rope_ref[...]
    scale = (QK_NOPE + QK_ROPE) ** -0.5
    for h in range(MLA_HEADS):
        blk = jnp.dot(c, w_ref[:, h * QK_PAD:(h + 1) * QK_PAD], preferred_element_type=F32)
        q_ref[h, :, 0:QK_NOPE] = (blk[:, :QK_NOPE] * scale).astype(q_ref.dtype)
        pe = _rope_combine(blk[:, QK_NOPE:], table) * scale
        q_ref[h, :, QK_NOPE:QK_PAD] = pe.astype(q_ref.dtype)


def q_proj(cq, w, rope, tm=256):
    M, K = cq.shape
    return pl.pallas_call(
        _qproj_kernel,
        grid=(M // tm,),
        in_specs=[pl.BlockSpec((tm, K), lambda i: (i, 0)),
                  pl.BlockSpec((K, MLA_HEADS * QK_PAD), lambda i: (0, 0)),
                  pl.BlockSpec((tm, LANES), lambda i: (i, 0))],
        out_specs=pl.BlockSpec((MLA_HEADS, tm, QK_PAD), lambda i: (0, i, 0)),
        out_shape=jax.ShapeDtypeStruct((MLA_HEADS, M, QK_PAD), BF16),
        name="q_proj",
        compiler_params=_cparams("arbitrary"),
    )(cq, w, rope)


def _kvproj_kernel(c_ref, w_ref, kpe_ref, k_ref, v_ref):
    c = c_ref[...]
    kpe = kpe_ref[...]
    for h in range(MLA_HEADS):
        blk = jnp.dot(c, w_ref[:, h * 256:(h + 1) * 256], preferred_element_type=F32)
        k_ref[h, :, 0:QK_NOPE] = blk[:, :QK_NOPE].astype(k_ref.dtype)
        k_ref[h, :, QK_NOPE:QK_PAD] = kpe
        v_ref[:, h * V_DIM:(h + 1) * V_DIM] = blk[:, QK_NOPE:].astype(v_ref.dtype)


def kv_proj(ckv, w, kpe, tm=256):
    M, K = ckv.shape
    return pl.pallas_call(
        _kvproj_kernel,
        grid=(M // tm,),
        in_specs=[pl.BlockSpec((tm, K), lambda i: (i, 0)),
                  pl.BlockSpec((K, MLA_HEADS * 256), lambda i: (0, 0)),
                  pl.BlockSpec((tm, LANES), lambda i: (i, 0))],
        out_specs=[pl.BlockSpec((MLA_HEADS, tm, QK_PAD), lambda i: (0, i, 0)),
                   pl.BlockSpec((tm, MLA_HEADS * V_DIM), lambda i: (i, 0))],
        out_shape=[jax.ShapeDtypeStruct((MLA_HEADS, M, QK_PAD), BF16),
                   jax.ShapeDtypeStruct((M, MLA_HEADS * V_DIM), BF16)],
        name="kv_proj",
        compiler_params=_cparams("arbitrary"),
    )(ckv, w, kpe)


def _attn_kernel(q_ref, k_ref, v_ref, o_ref, *, tq):
    qi = pl.program_id(1)
    q = q_ref[0]

    def step(j, carry, masked):
        m, l, acc = carry
        start = pl.multiple_of(j * tq, tq)
        kb = k_ref[0, pl.ds(start, tq), :]
        vb = v_ref[pl.ds(start, tq), :]
        s = lax.dot_general(q, kb, (((1,), (1,)), ((), ())), preferred_element_type=F32)
        if masked:
            rows = lax.broadcasted_iota(jnp.int32, s.shape, 0) // CHUNK
            cols = lax.broadcasted_iota(jnp.int32, s.shape, 1) // CHUNK
            s = jnp.where(cols <= rows, s, NEG_BIG)
        m_new = jnp.maximum(m, jnp.max(s, axis=-1, keepdims=True))
        alpha = jnp.exp(m - m_new)
        p = jnp.exp(s - m_new)
        l = alpha * l + jnp.sum(p, axis=-1, keepdims=True)
        acc = alpha * acc + jnp.dot(p.astype(vb.dtype), vb, preferred_element_type=F32)
        return m_new, l, acc

    init = (jnp.full((tq, 1), NEG_BIG, F32), jnp.zeros((tq, 1), F32), jnp.zeros((tq, V_DIM), F32))
    carry = lax.fori_loop(0, qi, lambda j, c: step(j, c, False), init)
    m, l, acc = step(qi, carry, True)
    o_ref[...] = (acc / l).astype(o_ref.dtype)


def mla_attention(q, k, v, tq=512):
    H, S, _ = q.shape
    return pl.pallas_call(
        functools.partial(_attn_kernel, tq=tq),
        grid=(H, S // tq),
        in_specs=[pl.BlockSpec((1, tq, QK_PAD), lambda h, i: (h, i, 0)),
                  pl.BlockSpec((1, S, QK_PAD), lambda h, i: (h, 0, 0)),
                  pl.BlockSpec((S, V_DIM), lambda h, i: (0, h))],
        out_specs=pl.BlockSpec((tq, V_DIM), lambda h, i: (i, h)),
        out_shape=jax.ShapeDtypeStruct((S, H * V_DIM), BF16),
        name="mla_attention",
        compiler_params=_cparams("arbitrary", "arbitrary"),
    )(q, k, v)


CONV_RB = 64
CONV_CB = 256


def _conv_kernel(cur_ref, halo_ref, w_ref, b_ref, g_ref, beta_ref, o_ref, buf_ref, acc_ref, *, tm):
    i = pl.program_id(0)
    buf_ref[0:CONV_HALO, :] = jnp.where(i > 0, halo_ref[...], 0.0)
    buf_ref[CONV_HALO:CONV_HALO + tm, :] = cur_ref[...]
    off = CONV_HALO - (CONV_K - 1)
    for r0 in range(0, tm, CONV_RB):
        for c0 in range(0, D_CONV, CONV_CB):
            acc = jnp.broadcast_to(b_ref[:, c0:c0 + CONV_CB], (CONV_RB, CONV_CB))
            for k in range(CONV_K):
                acc = acc + buf_ref[r0 + off + k:r0 + off + k + CONV_RB, c0:c0 + CONV_CB] * w_ref[k:k + 1, c0:c0 + CONV_CB]
            acc_ref[r0:r0 + CONV_RB, c0:c0 + CONV_CB] = acc
    y = acc_ref[...]
    mu = jnp.mean(y, axis=-1, keepdims=True)
    yc = y - mu
    yn = yc * lax.rsqrt(jnp.mean(yc * yc, axis=-1, keepdims=True) + EPS) * g_ref[...] + beta_ref[...]
    o_ref[...] = (yn * jax.nn.sigmoid(yn)).astype(o_ref.dtype)


def conv_ln_silu(hglu, conv_w, conv_b, ln_g, ln_b, tm=128):
    S, D = hglu.shape
    row = pl.BlockSpec((1, D), lambda i: (0, 0))
    return pl.pallas_call(
        functools.partial(_conv_kernel, tm=tm),
        grid=(S // tm,),
        in_specs=[pl.BlockSpec((tm, D), lambda i: (i, 0)),
                  pl.BlockSpec((CONV_HALO, D), lambda i: (jnp.maximum(i * (tm // CONV_HALO) - 1, 0), 0)),
                  pl.BlockSpec((CONV_K, D), lambda i: (0, 0)),
                  row, row, row],
        out_specs=pl.BlockSpec((tm, D), lambda i: (i, 0)),
        out_shape=jax.ShapeDtypeStruct((S, D), BF16),
        scratch_shapes=[pltpu.VMEM((CONV_HALO + tm, D), F32), pltpu.VMEM((tm, D), F32)],
        name="conv_ln_silu",
        compiler_params=_cparams("arbitrary"),
    )(hglu, hglu, conv_w, conv_b, ln_g, ln_b)


def _memkv_kernel(mem_ref, g_ref, w_ref, k_ref, v_ref):
    mn = _rms(mem_ref[...], g_ref[...]).astype(BF16)
    kv = jnp.dot(mn, w_ref[...], preferred_element_type=F32)
    n = X_HEADS * X_HEAD_DIM
    k_ref[...] = kv[:, :n].astype(k_ref.dtype)
    v_ref[...] = kv[:, n:].astype(v_ref.dtype)


def mem_kv(mem, g, w):
    n = X_HEADS * X_HEAD_DIM
    full = lambda shape: pl.BlockSpec(shape, lambda i: (0, 0))
    return pl.pallas_call(
        _memkv_kernel,
        grid=(1,),
        in_specs=[full(mem.shape), full(g.shape), full(w.shape)],
        out_specs=[full((N_MEM, n)), full((N_MEM, n))],
        out_shape=[jax.ShapeDtypeStruct((N_MEM, n), BF16)] * 2,
        name="mem_kv",
        compiler_params=_cparams("arbitrary"),
    )(mem, g, w)


def _xattn_kernel(h_ref, wq_ref, k_ref, v_ref, wo_ref, x_ref, g_ref, xo_ref, ho_ref):
    q = jnp.dot(h_ref[...], wq_ref[...], preferred_element_type=F32) * (X_HEAD_DIM ** -0.5)
    q = q.astype(BF16)
    outs = []
    for h in range(X_HEADS):
        sl = slice(h * X_HEAD_DIM, (h + 1) * X_HEAD_DIM)
        s = lax.dot_general(q[:, sl], k_ref[:, sl], (((1,), (1,)), ((), ())), preferred_element_type=F32)
        p = jnp.exp(s - jnp.max(s, axis=-1, keepdims=True))
        l = jnp.sum(p, axis=-1, keepdims=True)
        o = jnp.dot(p.astype(BF16), v_ref[:, sl], preferred_element_type=F32) / l
        outs.append(o.astype(BF16))
    o = jnp.concatenate(outs, axis=-1)
    x = x_ref[...] + jnp.dot(o, wo_ref[...], preferred_element_type=F32)
    xo_ref[...] = x
    ho_ref[...] = _rms(x, g_ref[...]).astype(ho_ref.dtype)


def cross_attention(h, wq, mk, mv, wo, x, g, tm=512):
    M, D = h.shape
    n = X_HEADS * X_HEAD_DIM
    const = lambda shape: pl.BlockSpec(shape, lambda i: (0, 0))
    rows = lambda w: pl.BlockSpec((tm, w), lambda i: (i, 0))
    return pl.pallas_call(
        _xattn_kernel,
        grid=(M // tm,),
        in_specs=[rows(D), const((D, n)), const((N_MEM, n)), const((N_MEM, n)), const((n, D)),
                  rows(D), const((1, D))],
        out_specs=[rows(D), rows(D)],
        out_shape=[jax.ShapeDtypeStruct((M, D), F32), jax.ShapeDtypeStruct((M, D), BF16)],
        name="cross_attention",
        compiler_params=_cparams("arbitrary"),
    )(h, wq, mk, mv, wo, x, g)


def _mlp_kernel(h_ref, w1_ref, w2_ref, x_ref, g_ref, xo_ref, ho_ref):
    f = pl.program_id(1)
    a = jnp.dot(h_ref[...], w1_ref[...], preferred_element_type=F32)
    a = jnp.square(jnp.maximum(a, 0.0)).astype(BF16)
    d = jnp.dot(a, w2_ref[...], preferred_element_type=F32)

    @pl.when(f == 0)
    def _():
        xo_ref[...] = x_ref[...] + d

    @pl.when(f > 0)
    def _():
        xo_ref[...] += d

    @pl.when(f == pl.num_programs(1) - 1)
    def _():
        ho_ref[...] = _rms(xo_ref[...], g_ref[...]).astype(ho_ref.dtype)


def mlp(h, w1, w2, x, g, out_dtype, tm=512, tf=1024):
    M, D = h.shape
    F = w1.shape[1]
    rows = pl.BlockSpec((tm, D), lambda i, f: (i, 0))
    return pl.pallas_call(
        _mlp_kernel,
        grid=(M // tm, F // tf),
        in_specs=[rows,
                  pl.BlockSpec((D, tf), lambda i, f: (0, f)),
                  pl.BlockSpec((tf, D), lambda i, f: (f, 0)),
                  rows,
                  pl.BlockSpec((1, D), lambda i, f: (0, 0))],
        out_specs=[rows, rows],
        out_shape=[jax.ShapeDtypeStruct((M, D), F32), jax.ShapeDtypeStruct((M, D), out_dtype)],
        name="mlp",
        compiler_params=_cparams("arbitrary", "arbitrary"),
    )(h, w1, w2, x, g)


def _q_column_order():
    half = QK_ROPE // 2
    idx = []
    for h in range(MLA_HEADS):
        base = h * (QK_NOPE + QK_ROPE)
        pe = base + QK_NOPE
        idx += list(range(base, base + QK_NOPE))
        idx += list(range(pe, pe + QK_ROPE))
        idx += list(range(pe + half, pe + QK_ROPE)) + list(range(pe, pe + half))
    return np.asarray(idx, dtype=np.int32)


def kernel(x, mem, positions, norm_mix_g, w_in, b_gate, conv_w, conv_b, conv_ln_g, conv_ln_b, w_conv_out, q_norm_g, w_uq, kv_norm_g, w_ukv, w_mla_out, w_out, norm_mem_g, mem_norm_g, w_xq, w_xkv, w_xo, norm_ffn_g, w_ff1, w_ff2, final_norm_g):
    B, S, D = x.shape
    assert (B, S, D) == (1, SEQ, D_MODEL)
    x2d = x.reshape(S, D)
    mem2d = mem.reshape(N_MEM, D)
    row = lambda v: v.reshape(1, -1)

    inv_freq = 1.0 / (ROPE_THETA ** (jnp.arange(0, QK_ROPE, 2, dtype=F32) / QK_ROPE))
    rope = rope_table(positions.reshape(S, 1), jnp.tile(inv_freq, 4).reshape(1, LANES))

    c0, c1, c2, c3 = 2 * D_CONV, 2 * D_CONV + Q_LORA, 2 * D_CONV + Q_LORA + KV_LORA, 2 * D_CONV + Q_LORA + KV_LORA + QK_ROPE
    half = QK_ROPE // 2
    q_cols = _q_column_order()

    h = rms_norm_rows(x2d, row(norm_mix_g[0]), BF16)
    xcur = x2d
    out = None
    for l in range(DEPTH):
        wl = w_in[l]
        w_a = wl[:, :D_CONV].astype(BF16)
        w_g = wl[:, D_CONV:c0].astype(BF16)
        w_kpe = wl[:, c2:c3]
        w_lat = jnp.concatenate([wl[:, c0:c2], w_kpe, w_kpe[:, half:], w_kpe[:, :half]], axis=1).astype(BF16)
        w_gate = wl[:, c3:].astype(BF16)

        (hglu,) = matmul("glu_proj", h, [w_a, w_g], [], [], [F32], _epi_glu, 1024, 1024)
        (gates,) = matmul("gate_proj", h, [w_gate], [row(b_gate[l])], ["row"], [F32], _epi_gate, 1024, 1024)
        cq, ckv, kpe = latent_proj(h, w_lat, row(q_norm_g[l]), row(kv_norm_g[l]), rope)

        hc = conv_ln_silu(hglu, conv_w[l], row(conv_b[l]), row(conv_ln_g[l]), row(conv_ln_b[l]))
        (gy,) = matmul("conv_out", hc, [w_conv_out[l].astype(BF16)], [gates], [("tile", 0)], [F32], _epi_mul, 1024, 1024)

        q = q_proj(cq, w_uq[l][:, q_cols].astype(BF16), rope)
        k, v = kv_proj(ckv, w_ukv[l].astype(BF16), kpe)
        o = mla_attention(q, k, v)
        (merged,) = matmul("mla_out_merge", o, [w_mla_out[l].astype(BF16)], [gy, gates], [("tile", 0), ("tile", D)], [BF16],
                           _epi_merge, 1024, 1024)

        xcur, hm = matmul("mix_out", merged, [w_out[l].astype(BF16)], [xcur, row(norm_mem_g[l])], [("tile", 0), "row"],
                          [F32, BF16], _epi_resid_norm, 512, D)

        mk, mv = mem_kv(mem2d, row(mem_norm_g[l]), w_xkv[l].astype(BF16))
        xcur, hf = cross_attention(hm, w_xq[l].astype(BF16), mk, mv, w_xo[l].astype(BF16), xcur,
                                   row(norm_ffn_g[l]))

        last = l == DEPTH - 1
        g_next = final_norm_g if last else norm_mix_g[l + 1]
        xcur, h = mlp(hf, w_ff1[l].astype(BF16), w_ff2[l].astype(BF16), xcur, row(g_next),
                      F32 if last else BF16)
        out = h
    return out.reshape(B, S, D)
```

```python
import functools
import math

import numpy as np
import jax
import jax.numpy as jnp
from jax import lax
from jax.experimental import pallas as pl
from jax.experimental.pallas import tpu as pltpu

D_MODEL = 2048
SEQ = 8192
DEPTH = 2
CHUNK = 64
N_MEM = 256
EPS = 1e-6
D_CONV = D_MODEL
CONV_K = 31
MLA_HEADS = 16
Q_LORA = 512
KV_LORA = 512
QK_NOPE = 128
QK_ROPE = 64
V_DIM = 128
ROPE_THETA = 10000.0
X_HEADS = 4
X_HEAD_DIM = 128
D_FF = 4 * D_MODEL

LANES = 128
QK_PAD = 256
CONV_HALO = 32
NEG_BIG = -1e30
VMEM_LIMIT = 56 * 1024 * 1024

F32 = jnp.float32
BF16 = jnp.bfloat16


def _cparams(*sem):
    return pltpu.CompilerParams(dimension_semantics=sem, vmem_limit_bytes=VMEM_LIMIT)


def _rms(x, g):
    return x * lax.rsqrt(jnp.mean(x * x, axis=-1, keepdims=True) + EPS) * g


def _rope_combine(z, table):
    t = z * table
    u = t + pltpu.roll(t, QK_ROPE, 1)
    lane = lax.broadcasted_iota(jnp.int32, u.shape, 1)
    return jnp.where(lane < QK_ROPE, u, 0.0)


def _rope_table_kernel(pos_ref, freq_ref, o_ref):
    ang = pos_ref[...].astype(F32) * freq_ref[...]
    lane = lax.broadcasted_iota(jnp.int32, ang.shape, 1)
    s = jnp.sin(ang)
    o_ref[...] = jnp.where(lane < 64, jnp.cos(ang), jnp.where(lane < 96, -s, s))


def rope_table(pos_col, freq_row):
    S = pos_col.shape[0]
    tm = 1024
    return pl.pallas_call(
        _rope_table_kernel,
        grid=(S // tm,),
        in_specs=[pl.BlockSpec((tm, 1), lambda i: (i, 0)),
                  pl.BlockSpec((1, LANES), lambda i: (0, 0))],
        out_specs=pl.BlockSpec((tm, LANES), lambda i: (i, 0)),
        out_shape=jax.ShapeDtypeStruct((S, LANES), F32),
        name="rope_table",
        compiler_params=_cparams("arbitrary"),
    )(pos_col, freq_row)


def _norm_kernel(x_ref, g_ref, o_ref):
    o_ref[...] = _rms(x_ref[...], g_ref[...]).astype(o_ref.dtype)


def rms_norm_rows(x, g, out_dtype, tm=512):
    M, D = x.shape
    return pl.pallas_call(
        _norm_kernel,
        grid=(M // tm,),
        in_specs=[pl.BlockSpec((tm, D), lambda i: (i, 0)),
                  pl.BlockSpec((1, D), lambda i: (0, 0))],
        out_specs=pl.BlockSpec((tm, D), lambda i: (i, 0)),
        out_shape=jax.ShapeDtypeStruct((M, D), out_dtype),
        name="rms_norm",
        compiler_params=_cparams("arbitrary"),
    )(x, g)


def _mm_kernel(*refs, n_b, n_extra, epilogue):
    a = refs[0][...]
    accs = [jnp.dot(a, b[...], preferred_element_type=F32) for b in refs[1:1 + n_b]]
    extras = [e[...] for e in refs[1 + n_b:1 + n_b + n_extra]]
    outs = epilogue(accs, extras)
    for o_ref, o in zip(refs[1 + n_b + n_extra:], outs):
        o_ref[...] = o.astype(o_ref.dtype)


def matmul(name, a, bs, extras, extra_kinds, out_dtypes, epilogue, tm, tn):
    M, K = a.shape
    N = bs[0].shape[1]
    in_specs = [pl.BlockSpec((tm, K), lambda i, j: (i, 0))]
    in_specs += [pl.BlockSpec((K, tn), lambda i, j: (0, j)) for _ in bs]
    for kind in extra_kinds:
        if kind == "row":
            in_specs.append(pl.BlockSpec((1, tn), lambda i, j: (0, j)))
        else:
            off = kind[1] // tn
            in_specs.append(pl.BlockSpec((tm, tn), lambda i, j, off=off: (i, j + off)))
    out_specs = [pl.BlockSpec((tm, tn), lambda i, j: (i, j)) for _ in out_dtypes]
    out_shape = [jax.ShapeDtypeStruct((M, N), dt) for dt in out_dtypes]
    return pl.pallas_call(
        functools.partial(_mm_kernel, n_b=len(bs), n_extra=len(extras), epilogue=epilogue),
        grid=(M // tm, N // tn),
        in_specs=in_specs,
        out_specs=out_specs,
        out_shape=out_shape,
        name=name,
        compiler_params=_cparams("arbitrary", "arbitrary"),
    )(a, *bs, *extras)


def _epi_glu(accs, extras):
    return [accs[0] * jax.nn.sigmoid(accs[1])]


def _epi_gate(accs, extras):
    return [jax.nn.sigmoid(accs[0] + extras[0])]


def _epi_mul(accs, extras):
    return [accs[0] * extras[0]]


def _epi_merge(accs, extras):
    return [extras[0] + extras[1] * accs[0]]


def _epi_resid_norm(accs, extras):
    x = extras[0] + accs[0]
    return [x, _rms(x, extras[1])]


def _latent_kernel(h_ref, w_ref, qg_ref, kvg_ref, rope_ref, cq_ref, ckv_ref, kpe_ref):
    z = jnp.dot(h_ref[...], w_ref[...], preferred_element_type=F32)
    cq_ref[...] = _rms(z[:, :Q_LORA], qg_ref[...]).astype(cq_ref.dtype)
    ckv_ref[...] = _rms(z[:, Q_LORA:Q_LORA + KV_LORA], kvg_ref[...]).astype(ckv_ref.dtype)
    kpe_ref[...] = _rope_combine(z[:, Q_LORA + KV_LORA:], rope_ref[...]).astype(kpe_ref.dtype)


def latent_proj(h, w, qg, kvg, rope, tm=512):
    M, K = h.shape
    N = w.shape[1]
    row = lambda n: pl.BlockSpec((1, n), lambda i: (0, 0))
    return pl.pallas_call(
        _latent_kernel,
        grid=(M // tm,),
        in_specs=[pl.BlockSpec((tm, K), lambda i: (i, 0)),
                  pl.BlockSpec((K, N), lambda i: (0, 0)),
                  row(Q_LORA), row(KV_LORA),
                  pl.BlockSpec((tm, LANES), lambda i: (i, 0))],
        out_specs=[pl.BlockSpec((tm, Q_LORA), lambda i: (i, 0)),
                   pl.BlockSpec((tm, KV_LORA), lambda i: (i, 0)),
                   pl.BlockSpec((tm, LANES), lambda i: (i, 0))],
        out_shape=[jax.ShapeDtypeStruct((M, Q_LORA), BF16),
                   jax.ShapeDtypeStruct((M, KV_LORA), BF16),
                   jax.ShapeDtypeStruct((M, LANES), BF16)],
        name="latent_proj",
        compiler_params=_cparams("arbitrary"),
    )(h, w, qg, kvg, rope)


def _qproj_kernel(c_ref, w_ref, rope_ref, q_ref):
    c = c_ref[...]
    table = rope_ref[...]
    scale = (QK_NOPE + QK_ROPE) ** -0.5 * math.log2(math.e)
    for h in range(MLA_HEADS):
        blk = jnp.dot(c, w_ref[:, h * QK_PAD:(h + 1) * QK_PAD], preferred_element_type=F32)
        q_ref[h, :, 0:QK_NOPE] = (blk[:, :QK_NOPE] * scale).astype(q_ref.dtype)
        pe = _rope_combine(blk[:, QK_NOPE:], table) * scale
        q_ref[h, :, QK_NOPE:QK_PAD] = pe.astype(q_ref.dtype)


def q_proj(cq, w, rope, tm=256):
    M, K = cq.shape
    return pl.pallas_call(
        _qproj_kernel,
        grid=(M // tm,),
        in_specs=[pl.BlockSpec((tm, K), lambda i: (i, 0)),
                  pl.BlockSpec((K, MLA_HEADS * QK_PAD), lambda i: (0, 0)),
                  pl.BlockSpec((tm, LANES), lambda i: (i, 0))],
        out_specs=pl.BlockSpec((MLA_HEADS, tm, QK_PAD), lambda i: (0, i, 0)),
        out_shape=jax.ShapeDtypeStruct((MLA_HEADS, M, QK_PAD), BF16),
        name="q_proj",
        compiler_params=_cparams("arbitrary"),
    )(cq, w, rope)


def _kvproj_kernel(c_ref, w_ref, kpe_ref, k_ref, v_ref):
    c = c_ref[...]
    kpe = kpe_ref[...]
    for h in range(MLA_HEADS):
        blk = jnp.dot(c, w_ref[:, h * 256:(h + 1) * 256], preferred_element_type=F32)
        k_ref[h, :, 0:QK_NOPE] = blk[:, :QK_NOPE].astype(k_ref.dtype)
        k_ref[h, :, QK_NOPE:QK_PAD] = kpe
        v_ref[:, h * V_DIM:(h + 1) * V_DIM] = blk[:, QK_NOPE:].astype(v_ref.dtype)


def kv_proj(ckv, w, kpe, tm=256):
    M, K = ckv.shape
    return pl.pallas_call(
        _kvproj_kernel,
        grid=(M // tm,),
        in_specs=[pl.BlockSpec((tm, K), lambda i: (i, 0)),
                  pl.BlockSpec((K, MLA_HEADS * 256), lambda i: (0, 0)),
                  pl.BlockSpec((tm, LANES), lambda i: (i, 0))],
        out_specs=[pl.BlockSpec((MLA_HEADS, tm, QK_PAD), lambda i: (0, i, 0)),
                   pl.BlockSpec((tm, MLA_HEADS * V_DIM), lambda i: (i, 0))],
        out_shape=[jax.ShapeDtypeStruct((MLA_HEADS, M, QK_PAD), BF16),
                   jax.ShapeDtypeStruct((M, MLA_HEADS * V_DIM), BF16)],
        name="kv_proj",
        compiler_params=_cparams("arbitrary"),
    )(ckv, w, kpe)


ATTN_RB = CHUNK


def _attn_kernel(q_ref, k_ref, v_ref, o_ref, s0_ref, s1_ref, p0_ref, p1_ref, a0_ref, a1_ref,
                 m_ref, l_ref, acc_ref, *, tq):
    qi = pl.program_id(1)
    ncol = tq // LANES
    s_refs, p_refs, a_refs = (s0_ref, s1_ref), (p0_ref, p1_ref), (a0_ref, a1_ref)

    def scores(j, par):
        start = pl.multiple_of(j * tq, tq)
        kb = k_ref[0, pl.ds(start, tq), :]
        s_refs[par][...] = lax.dot_general(q_ref[0], kb, (((1,), (1,)), ((), ())),
                                           preferred_element_type=F32)

    def softmax(par, masked):
        s_ref, p_ref, a_ref = s_refs[par], p_refs[par], a_refs[par]
        for r in range(tq // ATTN_RB):
            rows = slice(r * ATTN_RB, (r + 1) * ATTN_RB)
            cols = []
            for c in range(ncol):
                first_chunk = (c * LANES) // CHUNK
                if masked and first_chunk > r:
                    cols.append(None)
                    continue
                sc = s_ref[rows, c * LANES:(c + 1) * LANES]
                if masked and first_chunk == r:
                    lane = lax.broadcasted_iota(jnp.int32, sc.shape, 1)
                    sc = jnp.where(lane < CHUNK, sc, NEG_BIG)
                cols.append(sc)
            live = [sc for sc in cols if sc is not None]
            mx = functools.reduce(jnp.maximum, live)
            m_prev = m_ref[rows, :]
            m_new = jnp.maximum(m_prev, jnp.max(mx, axis=-1, keepdims=True))
            alpha = jnp.exp2(m_prev - m_new)
            ps = [None if sc is None else jnp.exp2(sc - m_new) for sc in cols]
            lsum = functools.reduce(jnp.add, [p for p in ps if p is not None])
            l_ref[rows, :] = alpha * l_ref[rows, :] + jnp.sum(lsum, axis=-1, keepdims=True)
            m_ref[rows, :] = m_new
            a_ref[rows, :] = alpha
            for c, p in enumerate(ps):
                pc = jnp.zeros((ATTN_RB, LANES), BF16) if p is None else p.astype(BF16)
                p_ref[rows, c * LANES:(c + 1) * LANES] = pc

    def values(j, par):
        start = pl.multiple_of(j * tq, tq)
        vb = v_ref[pl.ds(start, tq), :]
        acc_ref[...] = a_refs[par][...] * acc_ref[...] + jnp.dot(p_refs[par][...], vb,
                                                                 preferred_element_type=F32)

    def steady_tick(t, par):
        scores(t, par)
        softmax(1 - par, False)
        values(t - 2, par)

    m_ref[...] = jnp.full(m_ref.shape, NEG_BIG, F32)
    l_ref[...] = jnp.zeros(l_ref.shape, F32)
    acc_ref[...] = jnp.zeros(acc_ref.shape, F32)
    scores(0, 0)

    @pl.when(qi >= 1)
    def _():
        scores(1, 1)
        softmax(0, False)

    n_steady = jnp.maximum(qi - 1, 0)

    def pair(i, carry):
        t = 2 + 2 * i
        steady_tick(t, 0)
        steady_tick(t + 1, 1)
        return carry

    lax.fori_loop(0, n_steady // 2, pair, 0)

    @pl.when(n_steady % 2 == 1)
    def _():
        steady_tick(qi, 0)

    @pl.when(qi == 0)
    def _():
        softmax(0, True)
        values(0, 0)

    @pl.when(jnp.logical_and(qi > 0, qi % 2 == 0))
    def _():
        softmax(0, True)
        values(qi - 1, 1)
        values(qi, 0)

    @pl.when(qi % 2 == 1)
    def _():
        softmax(1, True)
        values(qi - 1, 0)
        values(qi, 1)

    o_ref[...] = (acc_ref[...] / l_ref[...]).astype(o_ref.dtype)


def mla_attention(q, k, v, tq=512):
    H, S, _ = q.shape
    return pl.pallas_call(
        functools.partial(_attn_kernel, tq=tq),
        grid=(H, S // tq),
        in_specs=[pl.BlockSpec((1, tq, QK_PAD), lambda h, i: (h, i, 0)),
                  pl.BlockSpec((1, S, QK_PAD), lambda h, i: (h, 0, 0)),
                  pl.BlockSpec((S, V_DIM), lambda h, i: (0, h))],
        out_specs=pl.BlockSpec((tq, V_DIM), lambda h, i: (i, h)),
        out_shape=jax.ShapeDtypeStruct((S, H * V_DIM), BF16),
        scratch_shapes=[pltpu.VMEM((tq, tq), F32), pltpu.VMEM((tq, tq), F32),
                        pltpu.VMEM((tq, tq), BF16), pltpu.VMEM((tq, tq), BF16),
                        pltpu.VMEM((tq, LANES), F32), pltpu.VMEM((tq, LANES), F32),
                        pltpu.VMEM((tq, LANES), F32),
                        pltpu.VMEM((tq, LANES), F32),
                        pltpu.VMEM((tq, V_DIM), F32)],
        name="mla_attention",
        compiler_params=_cparams("arbitrary", "arbitrary"),
    )(q, k, v)


CONV_RB = 64
CONV_CB = 256


def _conv_kernel(cur_ref, halo_ref, w_ref, b_ref, g_ref, beta_ref, o_ref, buf_ref, acc_ref, *, tm):
    i = pl.program_id(0)
    buf_ref[0:CONV_HALO, :] = jnp.where(i > 0, halo_ref[...], 0.0)
    buf_ref[CONV_HALO:CONV_HALO + tm, :] = cur_ref[...]
    off = CONV_HALO - (CONV_K - 1)
    for r0 in range(0, tm, CONV_RB):
        for c0 in range(0, D_CONV, CONV_CB):
            acc = jnp.broadcast_to(b_ref[:, c0:c0 + CONV_CB], (CONV_RB, CONV_CB))
            for k in range(CONV_K):
                acc = acc + buf_ref[r0 + off + k:r0 + off + k + CONV_RB, c0:c0 + CONV_CB] * w_ref[k:k + 1, c0:c0 + CONV_CB]
            acc_ref[r0:r0 + CONV_RB, c0:c0 + CONV_CB] = acc
    y = acc_ref[...]
    mu = jnp.mean(y, axis=-1, keepdims=True)
    yc = y - mu
    yn = yc * lax.rsqrt(jnp.mean(yc * yc, axis=-1, keepdims=True) + EPS) * g_ref[...] + beta_ref[...]
    o_ref[...] = (yn * jax.nn.sigmoid(yn)).astype(o_ref.dtype)


def conv_ln_silu(hglu, conv_w, conv_b, ln_g, ln_b, tm=128):
    S, D = hglu.shape
    row = pl.BlockSpec((1, D), lambda i: (0, 0))
    return pl.pallas_call(
        functools.partial(_conv_kernel, tm=tm),
        grid=(S // tm,),
        in_specs=[pl.BlockSpec((tm, D), lambda i: (i, 0)),
                  pl.BlockSpec((CONV_HALO, D), lambda i: (jnp.maximum(i * (tm // CONV_HALO) - 1, 0), 0)),
                  pl.BlockSpec((CONV_K, D), lambda i: (0, 0)),
                  row, row, row],
        out_specs=pl.BlockSpec((tm, D), lambda i: (i, 0)),
        out_shape=jax.ShapeDtypeStruct((S, D), BF16),
        scratch_shapes=[pltpu.VMEM((CONV_HALO + tm, D), F32), pltpu.VMEM((tm, D), F32)],
        name="conv_ln_silu",
        compiler_params=_cparams("arbitrary"),
    )(hglu, hglu, conv_w, conv_b, ln_g, ln_b)


def _memkv_kernel(mem_ref, g_ref, w_ref, k_ref, v_ref):
    mn = _rms(mem_ref[...], g_ref[...]).astype(BF16)
    kv = jnp.dot(mn, w_ref[...], preferred_element_type=F32)
    n = X_HEADS * X_HEAD_DIM
    k_ref[...] = kv[:, :n].astype(k_ref.dtype)
    v_ref[...] = kv[:, n:].astype(v_ref.dtype)


def mem_kv(mem, g, w):
    n = X_HEADS * X_HEAD_DIM
    full = lambda shape: pl.BlockSpec(shape, lambda i: (0, 0))
    return pl.pallas_call(
        _memkv_kernel,
        grid=(1,),
        in_specs=[full(mem.shape), full(g.shape), full(w.shape)],
        out_specs=[full((N_MEM, n)), full((N_MEM, n))],
        out_shape=[jax.ShapeDtypeStruct((N_MEM, n), BF16)] * 2,
        name="mem_kv",
        compiler_params=_cparams("arbitrary"),
    )(mem, g, w)


def _xattn_kernel(h_ref, wq_ref, k_ref, v_ref, wo_ref, x_ref, g_ref, xo_ref, ho_ref):
    q = jnp.dot(h_ref[...], wq_ref[...], preferred_element_type=F32) * (X_HEAD_DIM ** -0.5)
    q = q.astype(BF16)
    outs = []
    for h in range(X_HEADS):
        sl = slice(h * X_HEAD_DIM, (h + 1) * X_HEAD_DIM)
        s = lax.dot_general(q[:, sl], k_ref[:, sl], (((1,), (1,)), ((), ())), preferred_element_type=F32)
        p = jnp.exp(s - jnp.max(s, axis=-1, keepdims=True))
        l = jnp.sum(p, axis=-1, keepdims=True)
        o = jnp.dot(p.astype(BF16), v_ref[:, sl], preferred_element_type=F32) / l
        outs.append(o.astype(BF16))
    o = jnp.concatenate(outs, axis=-1)
    x = x_ref[...] + jnp.dot(o, wo_ref[...], preferred_element_type=F32)
    xo_ref[...] = x
    ho_ref[...] = _rms(x, g_ref[...]).astype(ho_ref.dtype)


def cross_attention(h, wq, mk, mv, wo, x, g, tm=512):
    M, D = h.shape
    n = X_HEADS * X_HEAD_DIM
    const = lambda shape: pl.BlockSpec(shape, lambda i: (0, 0))
    rows = lambda w: pl.BlockSpec((tm, w), lambda i: (i, 0))
    return pl.pallas_call(
        _xattn_kernel,
        grid=(M // tm,),
        in_specs=[rows(D), const((D, n)), const((N_MEM, n)), const((N_MEM, n)), const((n, D)),
                  rows(D), const((1, D))],
        out_specs=[rows(D), rows(D)],
        out_shape=[jax.ShapeDtypeStruct((M, D), F32), jax.ShapeDtypeStruct((M, D), BF16)],
        name="cross_attention",
        compiler_params=_cparams("arbitrary"),
    )(h, wq, mk, mv, wo, x, g)


def _mlp_kernel(h_ref, w1_ref, w2_ref, x_ref, g_ref, xo_ref, ho_ref):
    f = pl.program_id(1)
    a = jnp.dot(h_ref[...], w1_ref[...], preferred_element_type=F32)
    a = jnp.square(jnp.maximum(a, 0.0)).astype(BF16)
    d = jnp.dot(a, w2_ref[...], preferred_element_type=F32)

    @pl.when(f == 0)
    def _():
        xo_ref[...] = x_ref[...] + d

    @pl.when(f > 0)
    def _():
        xo_ref[...] += d

    @pl.when(f == pl.num_programs(1) - 1)
    def _():
        ho_ref[...] = _rms(xo_ref[...], g_ref[...]).astype(ho_ref.dtype)


def mlp(h, w1, w2, x, g, out_dtype, tm=512, tf=1024):
    M, D = h.shape
    F = w1.shape[1]
    rows = pl.BlockSpec((tm, D), lambda i, f: (i, 0))
    return pl.pallas_call(
        _mlp_kernel,
        grid=(M // tm, F // tf),
        in_specs=[rows,
                  pl.BlockSpec((D, tf), lambda i, f: (0, f)),
                  pl.BlockSpec((tf, D), lambda i, f: (f, 0)),
                  rows,
                  pl.BlockSpec((1, D), lambda i, f: (0, 0))],
        out_specs=[rows, rows],
        out_shape=[jax.ShapeDtypeStruct((M, D), F32), jax.ShapeDtypeStruct((M, D), out_dtype)],
        name="mlp",
        compiler_params=_cparams("arbitrary", "arbitrary"),
    )(h, w1, w2, x, g)


def _q_column_order():
    half = QK_ROPE // 2
    idx = []
    for h in range(MLA_HEADS):
        base = h * (QK_NOPE + QK_ROPE)
        pe = base + QK_NOPE
        idx += list(range(base, base + QK_NOPE))
        idx += list(range(pe, pe + QK_ROPE))
        idx += list(range(pe + half, pe + QK_ROPE)) + list(range(pe, pe + half))
    return np.asarray(idx, dtype=np.int32)


def kernel(x, mem, positions, norm_mix_g, w_in, b_gate, conv_w, conv_b, conv_ln_g, conv_ln_b, w_conv_out, q_norm_g, w_uq, kv_norm_g, w_ukv, w_mla_out, w_out, norm_mem_g, mem_norm_g, w_xq, w_xkv, w_xo, norm_ffn_g, w_ff1, w_ff2, final_norm_g):
    B, S, D = x.shape
    assert (B, S, D) == (1, SEQ, D_MODEL)
    x2d = x.reshape(S, D)
    mem2d = mem.reshape(N_MEM, D)
    row = lambda v: v.reshape(1, -1)

    inv_freq = 1.0 / (ROPE_THETA ** (jnp.arange(0, QK_ROPE, 2, dtype=F32) / QK_ROPE))
    rope = rope_table(positions.reshape(S, 1), jnp.tile(inv_freq, 4).reshape(1, LANES))

    c0, c1, c2, c3 = 2 * D_CONV, 2 * D_CONV + Q_LORA, 2 * D_CONV + Q_LORA + KV_LORA, 2 * D_CONV + Q_LORA + KV_LORA + QK_ROPE
    half = QK_ROPE // 2
    q_cols = _q_column_order()

    h = rms_norm_rows(x2d, row(norm_mix_g[0]), BF16)
    xcur = x2d
    out = None
    for l in range(DEPTH):
        wl = w_in[l]
        w_a = wl[:, :D_CONV].astype(BF16)
        w_g = wl[:, D_CONV:c0].astype(BF16)
        w_kpe = wl[:, c2:c3]
        w_lat = jnp.concatenate([wl[:, c0:c2], w_kpe, w_kpe[:, half:], w_kpe[:, :half]], axis=1).astype(BF16)
        w_gate = wl[:, c3:].astype(BF16)

        (hglu,) = matmul("glu_proj", h, [w_a, w_g], [], [], [F32], _epi_glu, 1024, 1024)
        (gates,) = matmul("gate_proj", h, [w_gate], [row(b_gate[l])], ["row"], [F32], _epi_gate, 1024, 1024)
        cq, ckv, kpe = latent_proj(h, w_lat, row(q_norm_g[l]), row(kv_norm_g[l]), rope)

        hc = conv_ln_silu(hglu, conv_w[l], row(conv_b[l]), row(conv_ln_g[l]), row(conv_ln_b[l]))
        (gy,) = matmul("conv_out", hc, [w_conv_out[l].astype(BF16)], [gates], [("tile", 0)], [F32], _epi_mul, 1024, 1024)

        q = q_proj(cq, w_uq[l][:, q_cols].astype(BF16), rope)
        k, v = kv_proj(ckv, w_ukv[l].astype(BF16), kpe)
        o = mla_attention(q, k, v)
        (merged,) = matmul("mla_out_merge", o, [w_mla_out[l].astype(BF16)], [gy, gates], [("tile", 0), ("tile", D)], [BF16],
                           _epi_merge, 1024, 1024)

        xcur, hm = matmul("mix_out", merged, [w_out[l].astype(BF16)], [xcur, row(norm_mem_g[l])], [("tile", 0), "row"],
                          [F32, BF16], _epi_resid_norm, 512, D)

        mk, mv = mem_kv(mem2d, row(mem_norm_g[l]), w_xkv[l].astype(BF16))
        xcur, hf = cross_attention(hm, w_xq[l].astype(BF16), mk, mv, w_xo[l].astype(BF16), xcur,
                                   row(norm_ffn_g[l]))

        last = l == DEPTH - 1
        g_next = final_norm_g if last else norm_mix_g[l + 1]
        xcur, h = mlp(hf, w_ff1[l].astype(BF16), w_ff2[l].astype(BF16), xcur, row(g_next),
                      F32 if last else BF16)
        out = h
    return out.reshape(B, S, D)
```

```python
import functools
import math

import jax
import jax.numpy as jnp
from jax import lax
from jax.experimental import pallas as pl
from jax.experimental.pallas import tpu as pltpu

D_MODEL = 2048
SEQ = 8192
DEPTH = 2
CHUNK = 64
N_MEM = 256
EPS = 1e-6
D_CONV = D_MODEL
CONV_K = 31
MLA_HEADS = 16
Q_LORA = 512
KV_LORA = 512
QK_NOPE = 128
QK_ROPE = 64
V_DIM = 128
ROPE_THETA = 10000.0
X_HEADS = 4
X_HEAD_DIM = 128
D_FF = 4 * D_MODEL

LANES = 128
SUBLANES = 8
QK_PAD = 256
CONV_HALO = 32
NEG_BIG = -1e30
VMEM_LIMIT = 56 * 1024 * 1024

COL_GLU_A = 0
COL_GLU_G = D_CONV
COL_LATENT = 2 * D_CONV
COL_KPE = COL_LATENT + Q_LORA + KV_LORA
COL_GATE = COL_KPE + QK_ROPE
W_IN_LO_COLS = COL_KPE + LANES

F32 = jnp.float32
BF16 = jnp.bfloat16


def _cparams(*sem):
    return pltpu.CompilerParams(dimension_semantics=sem, vmem_limit_bytes=VMEM_LIMIT)


def _rms(x, g):
    return x * lax.rsqrt(jnp.mean(x * x, axis=-1, keepdims=True) + EPS) * g


def _layer_row(n, layer, col0=0):
    return pl.BlockSpec((None, 1, n), lambda *_: (layer, 0, col0 // n))


def _rope_table_kernel(pos_ref, freq_ref, cos_ref, sin_ref):
    ang = pos_ref[...].astype(F32) * freq_ref[...]
    lane = lax.broadcasted_iota(jnp.int32, ang.shape, 1)
    s = jnp.sin(ang)
    cos_ref[...] = jnp.where(lane < QK_ROPE, jnp.cos(ang), 0.0)
    sin_ref[...] = jnp.where(lane < QK_ROPE // 2, -s, jnp.where(lane < QK_ROPE, s, 0.0))


def rope_tables(pos_col, freq_row):
    S = pos_col.shape[0]
    tm = 1024
    tab = pl.BlockSpec((tm, LANES), lambda i: (i, 0))
    return pl.pallas_call(
        _rope_table_kernel,
        grid=(S // tm,),
        in_specs=[pl.BlockSpec((tm, 1), lambda i: (i, 0)),
                  pl.BlockSpec((1, LANES), lambda i: (0, 0))],
        out_specs=[tab, tab],
        out_shape=[jax.ShapeDtypeStruct((S, LANES), F32)] * 2,
        name="rope_table",
        compiler_params=_cparams("arbitrary"),
    )(pos_col, freq_row)


def _norm_kernel(x_ref, g_ref, o_ref):
    o_ref[...] = _rms(x_ref[...], g_ref[...]).astype(o_ref.dtype)


def rms_norm_rows(x, g3, layer, out_dtype, tm=512):
    M, D = x.shape
    return pl.pallas_call(
        _norm_kernel,
        grid=(M // tm,),
        in_specs=[pl.BlockSpec((tm, D), lambda i: (i, 0)), _layer_row(D, layer)],
        out_specs=pl.BlockSpec((tm, D), lambda i: (i, 0)),
        out_shape=jax.ShapeDtypeStruct((M, D), out_dtype),
        name="rms_norm",
        compiler_params=_cparams("arbitrary"),
    )(x, g3)


def _mm_kernel(*refs, a_index, n_a, n_extra, epilogue):
    n_b = len(a_index)
    acts = [r[...] for r in refs[:n_a]]
    accs = [jnp.dot(acts[ai], b[...], preferred_element_type=F32)
            for ai, b in zip(a_index, refs[n_a:n_a + n_b])]
    extras = [e[...] for e in refs[n_a + n_b:n_a + n_b + n_extra]]
    outs = epilogue(accs, extras)
    for o_ref, o in zip(refs[n_a + n_b + n_extra:], outs):
        o_ref[...] = o.astype(o_ref.dtype)


def matmul(name, acts, weights, extras, out_dtypes, epilogue, n_out, tm, tn):
    M = acts[0].shape[0]
    in_specs = [pl.BlockSpec((tm, a.shape[1]), lambda i, j: (i, 0)) for a in acts]
    operands = list(acts)
    for _, w, layer, col0 in weights:
        in_specs.append(pl.BlockSpec((None, w.shape[1], tn),
                                     lambda i, j, layer=layer, off=col0 // tn: (layer, 0, j + off)))
        operands.append(w)
    for e in extras:
        if e[0] == "row":
            _, arr, layer, col0 = e
            in_specs.append(pl.BlockSpec((None, 1, tn),
                                         lambda i, j, layer=layer, off=col0 // tn: (layer, 0, j + off)))
        else:
            _, arr, col0 = e
            in_specs.append(pl.BlockSpec((tm, tn), lambda i, j, off=col0 // tn: (i, j + off)))
        operands.append(arr)
    return pl.pallas_call(
        functools.partial(_mm_kernel, a_index=tuple(w[0] for w in weights), n_a=len(acts),
                          n_extra=len(extras), epilogue=epilogue),
        grid=(M // tm, n_out // tn),
        in_specs=in_specs,
        out_specs=[pl.BlockSpec((tm, tn), lambda i, j: (i, j)) for _ in out_dtypes],
        out_shape=[jax.ShapeDtypeStruct((M, n_out), dt) for dt in out_dtypes],
        name=name,
        compiler_params=_cparams("arbitrary", "arbitrary"),
    )(*operands)


def _epi_glu(accs, extras):
    return [accs[0] * jax.nn.sigmoid(accs[1])]


def _epi_gated(accs, extras):
    return [accs[0] * jax.nn.sigmoid(accs[1] + extras[0])]


def _epi_gated_merge(accs, extras):
    return [extras[1] + accs[0] * jax.nn.sigmoid(accs[1] + extras[0])]


def _epi_resid_norm(accs, extras):
    x = extras[0] + accs[0]
    return [x, _rms(x, extras[1])]


def _latent_kernel(h_ref, w_ref, wpe_ref, qg_ref, kvg_ref, cos_ref, sin_ref, cq_ref, ckv_ref, kpe_ref):
    h = h_ref[...]
    z = jnp.dot(h, w_ref[...], preferred_element_type=F32)
    cq_ref[...] = _rms(z[:, :Q_LORA], qg_ref[...]).astype(cq_ref.dtype)
    ckv_ref[...] = _rms(z[:, Q_LORA:], kvg_ref[...]).astype(ckv_ref.dtype)
    pe = jnp.dot(h, wpe_ref[...], preferred_element_type=F32)
    half = QK_ROPE // 2
    lane = lax.broadcasted_iota(jnp.int32, pe.shape, 1)
    swapped = jnp.where(lane < half, pltpu.roll(pe, LANES - half, 1), pltpu.roll(pe, half, 1))
    roped = pe * cos_ref[...] + swapped * sin_ref[...]
    kpe_ref[...] = jnp.where(lane < QK_ROPE, roped, 0.0).astype(kpe_ref.dtype)


def latent_proj(h, w_in_lo, layer, qg3, kvg3, cos, sin, tm=512):
    M, K = h.shape
    n_lat = Q_LORA + KV_LORA
    rows = lambda n: pl.BlockSpec((tm, n), lambda i: (i, 0))
    return pl.pallas_call(
        _latent_kernel,
        grid=(M // tm,),
        in_specs=[rows(K),
                  pl.BlockSpec((None, K, n_lat), lambda i: (layer, 0, COL_LATENT // n_lat)),
                  pl.BlockSpec((None, K, LANES), lambda i: (layer, 0, COL_KPE // LANES)),
                  _layer_row(Q_LORA, layer), _layer_row(KV_LORA, layer),
                  rows(LANES), rows(LANES)],
        out_specs=[rows(Q_LORA), rows(KV_LORA), rows(LANES)],
        out_shape=[jax.ShapeDtypeStruct((M, Q_LORA), BF16),
                   jax.ShapeDtypeStruct((M, KV_LORA), BF16),
                   jax.ShapeDtypeStruct((M, LANES), BF16)],
        name="latent_proj",
        compiler_params=_cparams("arbitrary"),
    )(h, w_in_lo, w_in_lo, qg3, kvg3, cos, sin)


def _qproj_kernel(c_ref, w_ref, cos_ref, sin_ref, q_ref):
    c = c_ref[...]
    cos, sin = cos_ref[...], sin_ref[...]
    scale = (QK_NOPE + QK_ROPE) ** -0.5 * math.log2(math.e)
    for h in range(MLA_HEADS):
        blk = jnp.dot(c, w_ref[:, h * QK_PAD:(h + 1) * QK_PAD], preferred_element_type=F32)
        q_ref[h, :, 0:QK_NOPE] = (blk[:, :QK_NOPE] * scale).astype(q_ref.dtype)
        pe = blk[:, QK_NOPE:]
        roped = pe * cos + pltpu.roll(pe, QK_ROPE, 1) * sin
        q_ref[h, :, QK_NOPE:QK_PAD] = (roped * scale).astype(q_ref.dtype)


def q_proj(cq, w, layer, cos, sin, tm=512):
    M, K = cq.shape
    rows = lambda n: pl.BlockSpec((tm, n), lambda i: (i, 0))
    return pl.pallas_call(
        _qproj_kernel,
        grid=(M // tm,),
        in_specs=[rows(K),
                  pl.BlockSpec((None, K, MLA_HEADS * QK_PAD), lambda i: (layer, 0, 0)),
                  rows(LANES), rows(LANES)],
        out_specs=pl.BlockSpec((MLA_HEADS, tm, QK_PAD), lambda i: (0, i, 0)),
        out_shape=jax.ShapeDtypeStruct((MLA_HEADS, M, QK_PAD), BF16),
        name="q_proj",
        compiler_params=_cparams("arbitrary"),
    )(cq, w, cos, sin)


def _kvproj_kernel(c_ref, w_ref, kpe_ref, k_ref, v_ref):
    c = c_ref[...]
    kpe = kpe_ref[...]
    width = QK_NOPE + V_DIM
    for h in range(MLA_HEADS):
        blk = jnp.dot(c, w_ref[:, h * width:(h + 1) * width], preferred_element_type=F32)
        k_ref[h, :, 0:QK_NOPE] = blk[:, :QK_NOPE].astype(k_ref.dtype)
        k_ref[h, :, QK_NOPE:QK_PAD] = kpe
        v_ref[:, h * V_DIM:(h + 1) * V_DIM] = blk[:, QK_NOPE:].astype(v_ref.dtype)


def kv_proj(ckv, w, layer, kpe, tm=512):
    M, K = ckv.shape
    rows = lambda n: pl.BlockSpec((tm, n), lambda i: (i, 0))
    return pl.pallas_call(
        _kvproj_kernel,
        grid=(M // tm,),
        in_specs=[rows(K),
                  pl.BlockSpec((None, K, MLA_HEADS * (QK_NOPE + V_DIM)), lambda i: (layer, 0, 0)),
                  rows(LANES)],
        out_specs=[pl.BlockSpec((MLA_HEADS, tm, QK_PAD), lambda i: (0, i, 0)),
                   rows(MLA_HEADS * V_DIM)],
        out_shape=[jax.ShapeDtypeStruct((MLA_HEADS, M, QK_PAD), BF16),
                   jax.ShapeDtypeStruct((M, MLA_HEADS * V_DIM), BF16)],
        name="kv_proj",
        compiler_params=_cparams("arbitrary"),
    )(ckv, w, kpe)


ATTN_RB = CHUNK
ATTN_UNROLL = 4


def _attn_kernel(q_ref, k_ref, v_ref, o_ref, s0_ref, s1_ref, p0_ref, p1_ref, a0_ref, a1_ref,
                 m_ref, l_ref, acc_ref, *, tq):
    qi = pl.program_id(1)
    ncol = tq // LANES
    s_refs, p_refs, a_refs = (s0_ref, s1_ref), (p0_ref, p1_ref), (a0_ref, a1_ref)

    def scores(j, par):
        start = pl.multiple_of(j * tq, tq)
        kb = k_ref[0, pl.ds(start, tq), :]
        s_refs[par][...] = lax.dot_general(q_ref[0], kb, (((1,), (1,)), ((), ())),
                                           preferred_element_type=F32)

    def softmax(par, masked):
        s_ref, p_ref, a_ref = s_refs[par], p_refs[par], a_refs[par]
        for r in range(tq // ATTN_RB):
            rows = slice(r * ATTN_RB, (r + 1) * ATTN_RB)
            cols = []
            for c in range(ncol):
                first_chunk = (c * LANES) // CHUNK
                if masked and first_chunk > r:
                    cols.append(None)
                    continue
                sc = s_ref[rows, c * LANES:(c + 1) * LANES]
                if masked and first_chunk == r:
                    lane = lax.broadcasted_iota(jnp.int32, sc.shape, 1)
                    sc = jnp.where(lane < CHUNK, sc, NEG_BIG)
                cols.append(sc)
            live = [sc for sc in cols if sc is not None]
            mx = functools.reduce(jnp.maximum, live)
            m_prev = m_ref[rows, :]
            m_new = jnp.maximum(m_prev, jnp.max(mx, axis=-1, keepdims=True))
            alpha = jnp.exp2(m_prev - m_new)
            ps = [None if sc is None else jnp.exp2(sc - m_new) for sc in cols]
            lsum = functools.reduce(jnp.add, [p for p in ps if p is not None])
            l_ref[rows, :] = alpha * l_ref[rows, :] + jnp.sum(lsum, axis=-1, keepdims=True)
            m_ref[rows, :] = m_new
            a_ref[rows, :] = alpha
            for c, p in enumerate(ps):
                pc = jnp.zeros((ATTN_RB, LANES), BF16) if p is None else p.astype(BF16)
                p_ref[rows, c * LANES:(c + 1) * LANES] = pc

    def values(j, par):
        start = pl.multiple_of(j * tq, tq)
        vb = v_ref[pl.ds(start, tq), :]
        acc_ref[...] = a_refs[par][...] * acc_ref[...] + jnp.dot(p_refs[par][...], vb,
                                                                 preferred_element_type=F32)

    def steady_tick(t, par):
        scores(t, par)
        softmax(1 - par, False)
        values(t - 2, par)

    m_ref[...] = jnp.full(m_ref.shape, NEG_BIG, F32)
    l_ref[...] = jnp.zeros(l_ref.shape, F32)
    acc_ref[...] = jnp.zeros(acc_ref.shape, F32)

    @pl.when(qi >= 1)
    def _():
        scores(0, 0)
        scores(1, 1)
        softmax(0, False)

    n_steady = jnp.maximum(qi - 1, 0)

    def unrolled(i, carry):
        t = 2 + ATTN_UNROLL * i
        for u in range(ATTN_UNROLL):
            steady_tick(t + u, u % 2)
        return carry

    lax.fori_loop(0, n_steady // ATTN_UNROLL, unrolled, 0)
    t_pair = 2 + (n_steady // ATTN_UNROLL) * ATTN_UNROLL
    n_left = n_steady % ATTN_UNROLL

    def pair(i, carry):
        steady_tick(t_pair + 2 * i, 0)
        steady_tick(t_pair + 2 * i + 1, 1)
        return carry

    lax.fori_loop(0, n_left // 2, pair, 0)

    @pl.when(n_left % 2 == 1)
    def _():
        steady_tick(qi, 0)

    @pl.when(qi == 0)
    def _():
        scores(0, 0)
        softmax(0, True)
        values(0, 0)

    @pl.when(jnp.logical_and(qi > 0, qi % 2 == 0))
    def _():
        softmax(0, True)
        values(qi - 1, 1)
        values(qi, 0)

    @pl.when(qi % 2 == 1)
    def _():
        softmax(1, True)
        values(qi - 1, 0)
        values(qi, 1)

    o_ref[...] = (acc_ref[...] / l_ref[...]).astype(o_ref.dtype)


def mla_attention(q, k, v, tq=512):
    H, S, _ = q.shape
    par2 = lambda shape, dt: [pltpu.VMEM(shape, dt), pltpu.VMEM(shape, dt)]
    return pl.pallas_call(
        functools.partial(_attn_kernel, tq=tq),
        grid=(H, S // tq),
        in_specs=[pl.BlockSpec((1, tq, QK_PAD), lambda h, i: (h, i, 0)),
                  pl.BlockSpec((1, S, QK_PAD), lambda h, i: (h, 0, 0)),
                  pl.BlockSpec((S, V_DIM), lambda h, i: (0, h))],
        out_specs=pl.BlockSpec((tq, V_DIM), lambda h, i: (i, h)),
        out_shape=jax.ShapeDtypeStruct((S, H * V_DIM), BF16),
        scratch_shapes=(par2((tq, tq), F32)
                        + par2((tq, tq), BF16)
                        + par2((tq, LANES), F32)
                        + [pltpu.VMEM((tq, LANES), F32),
                           pltpu.VMEM((tq, LANES), F32),
                           pltpu.VMEM((tq, V_DIM), F32)]),
        name="mla_attention",
        compiler_params=_cparams("arbitrary", "arbitrary"),
    )(q, k, v)


CONV_RB = 64
CONV_CB = LANES


def _conv_kernel(cur_ref, halo_ref, w_ref, b_ref, g_ref, beta_ref, o_ref, buf_ref, sh_ref, acc_ref, wb_ref, *, tm):
    i = pl.program_id(0)

    @pl.when(i == 0)
    def _():
        for k in range(CONV_K):
            wb_ref[k] = jnp.broadcast_to(w_ref[k:k + 1, :], (SUBLANES, D_CONV))

    buf_ref[0:CONV_HALO, :] = jnp.where(i > 0, halo_ref[...], 0.0)
    buf_ref[CONV_HALO:CONV_HALO + tm, :] = cur_ref[...]
    n_buf = tm + CONV_HALO
    n_sh = n_buf - SUBLANES
    for c0 in range(0, D_CONV, LANES):
        x = buf_ref[:, c0:c0 + LANES]
        for b in range(1, SUBLANES):
            sh_ref[b - 1, :, c0:c0 + LANES] = pltpu.roll(x, n_buf - b, 0)[0:n_sh, :]
    off = CONV_HALO - (CONV_K - 1)
    for c0 in range(0, D_CONV, CONV_CB):
        cols = slice(c0, c0 + CONV_CB)
        w_all = wb_ref[:, :, cols]
        bias = jnp.broadcast_to(b_ref[:, cols], (CONV_RB, CONV_CB))

        def row_block(rb, carry, cols=cols, w_all=w_all, bias=bias):
            r0 = pl.multiple_of(rb * CONV_RB, CONV_RB)
            acc = bias
            for b in range(SUBLANES):
                taps = [k for k in range(CONV_K) if (off + k) % SUBLANES == b]
                a_lo, a_hi = (off + taps[0]) // SUBLANES, (off + taps[-1]) // SUBLANES
                rows = pl.ds(r0 + a_lo * SUBLANES, (a_hi - a_lo) * SUBLANES + CONV_RB)
                slab = buf_ref[rows, cols] if b == 0 else sh_ref[b - 1, rows, cols]
                for k in taps:
                    s0 = ((off + k) // SUBLANES - a_lo) * SUBLANES
                    wk = jnp.concatenate([w_all[k]] * (CONV_RB // SUBLANES), axis=0)
                    acc = acc + slab[s0:s0 + CONV_RB, :] * wk
            acc_ref[pl.ds(r0, CONV_RB), cols] = acc
            return carry

        lax.fori_loop(0, tm // CONV_RB, row_block, 0)
    y = acc_ref[...]
    mu = jnp.mean(y, axis=-1, keepdims=True)
    yc = y - mu
    yn = yc * lax.rsqrt(jnp.mean(yc * yc, axis=-1, keepdims=True) + EPS) * g_ref[...] + beta_ref[...]
    o_ref[...] = (yn * jax.nn.sigmoid(yn)).astype(o_ref.dtype)


def conv_ln_silu(hglu, conv_w, conv_b3, ln_g3, ln_b3, layer, tm=256):
    S, D = hglu.shape
    row = _layer_row(D, layer)
    return pl.pallas_call(
        functools.partial(_conv_kernel, tm=tm),
        grid=(S // tm,),
        in_specs=[pl.BlockSpec((tm, D), lambda i: (i, 0)),
                  pl.BlockSpec((CONV_HALO, D), lambda i: (jnp.maximum(i * (tm // CONV_HALO) - 1, 0), 0)),
                  pl.BlockSpec((None, CONV_K, D), lambda i: (layer, 0, 0)),
                  row, row, row],
        out_specs=pl.BlockSpec((tm, D), lambda i: (i, 0)),
        out_shape=jax.ShapeDtypeStruct((S, D), BF16),
        scratch_shapes=[pltpu.VMEM((CONV_HALO + tm, D), F32),
                        pltpu.VMEM((SUBLANES - 1, CONV_HALO + tm - SUBLANES, D), F32),
                        pltpu.VMEM((tm, D), F32),
                        pltpu.VMEM((CONV_K, SUBLANES, D), F32)],
        name="conv_ln_silu",
        compiler_params=_cparams("arbitrary"),
    )(hglu, hglu, conv_w, conv_b3, ln_g3, ln_b3)


def _memkv_kernel(mem_ref, g_ref, w_ref, k_ref, v_ref):
    mn = _rms(mem_ref[...], g_ref[...]).astype(BF16)
    kv = jnp.dot(mn, w_ref[...], preferred_element_type=F32)
    n = X_HEADS * X_HEAD_DIM
    k_ref[...] = kv[:, :n].astype(k_ref.dtype)
    v_ref[...] = kv[:, n:].astype(v_ref.dtype)


def mem_kv(mem, g3, w, layer):
    n = X_HEADS * X_HEAD_DIM
    D = mem.shape[1]
    full = lambda shape: pl.BlockSpec(shape, lambda i: (0, 0))
    return pl.pallas_call(
        _memkv_kernel,
        grid=(1,),
        in_specs=[full(mem.shape), _layer_row(D, layer),
                  pl.BlockSpec((None, D, 2 * n), lambda i: (layer, 0, 0))],
        out_specs=[full((N_MEM, n)), full((N_MEM, n))],
        out_shape=[jax.ShapeDtypeStruct((N_MEM, n), BF16)] * 2,
        name="mem_kv",
        compiler_params=_cparams("arbitrary"),
    )(mem, g3, w)


def _xattn_kernel(h_ref, wq_ref, k_ref, v_ref, wo_ref, x_ref, g_ref, xo_ref, ho_ref):
    q = jnp.dot(h_ref[...], wq_ref[...], preferred_element_type=F32) * (X_HEAD_DIM ** -0.5)
    q = q.astype(BF16)
    outs = []
    for h in range(X_HEADS):
        sl = slice(h * X_HEAD_DIM, (h + 1) * X_HEAD_DIM)
        s = lax.dot_general(q[:, sl], k_ref[:, sl], (((1,), (1,)), ((), ())), preferred_element_type=F32)
        p = jnp.exp(s - jnp.max(s, axis=-1, keepdims=True))
        l = jnp.sum(p, axis=-1, keepdims=True)
        o = jnp.dot(p.astype(BF16), v_ref[:, sl], preferred_element_type=F32) / l
        outs.append(o.astype(BF16))
    o = jnp.concatenate(outs, axis=-1)
    x = x_ref[...] + jnp.dot(o, wo_ref[...], preferred_element_type=F32)
    xo_ref[...] = x
    ho_ref[...] = _rms(x, g_ref[...]).astype(ho_ref.dtype)


def cross_attention(h, wq, mk, mv, wo, x, g3, layer, tm=512):
    M, D = h.shape
    n = X_HEADS * X_HEAD_DIM
    const = lambda shape: pl.BlockSpec(shape, lambda i: (0, 0))
    rows = lambda w: pl.BlockSpec((tm, w), lambda i: (i, 0))
    return pl.pallas_call(
        _xattn_kernel,
        grid=(M // tm,),
        in_specs=[rows(D),
                  pl.BlockSpec((None, D, n), lambda i: (layer, 0, 0)),
                  const((N_MEM, n)), const((N_MEM, n)),
                  pl.BlockSpec((None, n, D), lambda i: (layer, 0, 0)),
                  rows(D), _layer_row(D, layer)],
        out_specs=[rows(D), rows(D)],
        out_shape=[jax.ShapeDtypeStruct((M, D), F32), jax.ShapeDtypeStruct((M, D), BF16)],
        name="cross_attention",
        compiler_params=_cparams("arbitrary"),
    )(h, wq, mk, mv, wo, x, g3)


def _mlp_kernel(h_ref, w1_ref, w2_ref, x_ref, g_ref, xo_ref, ho_ref):
    f = pl.program_id(1)

    @pl.when(f == 0)
    def _():
        xo_ref[...] = x_ref[...]

    a = jnp.dot(h_ref[...], w1_ref[...], preferred_element_type=F32)
    a = jnp.square(jnp.maximum(a, 0.0)).astype(BF16)
    xo_ref[...] += jnp.dot(a, w2_ref[...], preferred_element_type=F32)

    @pl.when(f == pl.num_programs(1) - 1)
    def _():
        ho_ref[...] = _rms(xo_ref[...], g_ref[...]).astype(ho_ref.dtype)


def mlp(h, w1, w2, layer, x, g3, g_layer, out_dtype, tm=512, tf=1024):
    M, D = h.shape
    F = w1.shape[2]
    rows = pl.BlockSpec((tm, D), lambda i, f: (i, 0))
    return pl.pallas_call(
        _mlp_kernel,
        grid=(M // tm, F // tf),
        in_specs=[rows,
                  pl.BlockSpec((None, D, tf), lambda i, f: (layer, 0, f)),
                  pl.BlockSpec((None, tf, D), lambda i, f: (layer, f, 0)),
                  rows,
                  _layer_row(D, g_layer)],
        out_specs=[rows, rows],
        out_shape=[jax.ShapeDtypeStruct((M, D), F32), jax.ShapeDtypeStruct((M, D), out_dtype)],
        name="mlp",
        compiler_params=_cparams("arbitrary", "arbitrary"),
    )(h, w1, w2, x, g3)


def _permute_q_weight(w_uq):
    L, K, _ = w_uq.shape
    half = QK_ROPE // 2
    w = w_uq.reshape(L, K, MLA_HEADS, QK_NOPE + QK_ROPE)
    pe = w[..., QK_NOPE:]
    w = jnp.concatenate([w, pe[..., half:], pe[..., :half]], axis=-1)
    return w.reshape(L, K, MLA_HEADS * QK_PAD)


def kernel(x, mem, positions, norm_mix_g, w_in, b_gate, conv_w, conv_b, conv_ln_g, conv_ln_b, w_conv_out, q_norm_g, w_uq, kv_norm_g, w_ukv, w_mla_out, w_out, norm_mem_g, mem_norm_g, w_xq, w_xkv, w_xo, norm_ffn_g, w_ff1, w_ff2, final_norm_g):
    B, S, D = x.shape
    assert (B, S, D) == (1, SEQ, D_MODEL)
    x2d = x.reshape(S, D)
    mem2d = mem.reshape(N_MEM, D)
    vec3 = lambda v: v.reshape(v.shape[0], 1, v.shape[1])

    inv_freq = 1.0 / (ROPE_THETA ** (jnp.arange(0, QK_ROPE, 2, dtype=F32) / QK_ROPE))
    cos, sin = rope_tables(positions.reshape(S, 1), jnp.tile(inv_freq, 4).reshape(1, LANES))

    w_in_lo = w_in[:, :, :W_IN_LO_COLS].astype(BF16)
    w_gate = w_in[:, :, COL_GATE:].astype(BF16)
    w_q = _permute_q_weight(w_uq).astype(BF16)
    w_kv, w_co, w_mo, w_o = (w.astype(BF16) for w in (w_ukv, w_conv_out, w_mla_out, w_out))
    w_q_x, w_kv_x, w_o_x = (w.astype(BF16) for w in (w_xq, w_xkv, w_xo))
    w_1, w_2 = w_ff1.astype(BF16), w_ff2.astype(BF16)

    g_mix, g_mem, g_memn, g_ffn = vec3(norm_mix_g), vec3(norm_mem_g), vec3(mem_norm_g), vec3(norm_ffn_g)
    g_q, g_kv, b_g = vec3(q_norm_g), vec3(kv_norm_g), vec3(b_gate)
    c_b, ln_g, ln_b = vec3(conv_b), vec3(conv_ln_g), vec3(conv_ln_b)
    g_final = final_norm_g.reshape(1, 1, D)

    h = rms_norm_rows(x2d, g_mix, 0, BF16)
    xcur = x2d
    for l in range(DEPTH):
        (hglu,) = matmul("glu_proj", [h], [(0, w_in_lo, l, COL_GLU_A), (0, w_in_lo, l, COL_GLU_G)], [],
                         [F32], _epi_glu, D_CONV, 1024, 512)
        cq, ckv, kpe = latent_proj(h, w_in_lo, l, g_q, g_kv, cos, sin)

        hc = conv_ln_silu(hglu, conv_w, c_b, ln_g, ln_b, l)
        (gy,) = matmul("conv_out_gated", [hc, h], [(0, w_co, l, 0), (1, w_gate, l, 0)],
                       [("row", b_g, l, 0)], [BF16], _epi_gated, D, 1024, 512)

        q = q_proj(cq, w_q, l, cos, sin)
        k, v = kv_proj(ckv, w_kv, l, kpe)
        o = mla_attention(q, k, v)
        (merged,) = matmul("mla_out_gated_merge", [o, h], [(0, w_mo, l, 0), (1, w_gate, l, D)],
                           [("row", b_g, l, D), ("tile", gy, 0)], [BF16], _epi_gated_merge, D, 1024, 512)

        xcur, hm = matmul("mix_out", [merged], [(0, w_o, l, 0)], [("tile", xcur, 0), ("row", g_mem, l, 0)],
                          [F32, BF16], _epi_resid_norm, D, 512, D)

        mk, mv = mem_kv(mem2d, g_memn, w_kv_x, l)
        xcur, hf = cross_attention(hm, w_q_x, mk, mv, w_o_x, xcur, g_ffn, l)

        last = l == DEPTH - 1
        xcur, h = mlp(hf, w_1, w_2, l, xcur, g_final if last else g_mix, 0 if last else l + 1,
                      F32 if last else BF16)
    return h.reshape(B, S, D)
```

```python
import functools
import math

import jax
import jax.numpy as jnp
from jax import lax
from jax.experimental import pallas as pl
from jax.experimental.pallas import tpu as pltpu

D_MODEL = 2048
SEQ = 8192
DEPTH = 2
CHUNK = 64
N_MEM = 256
EPS = 1e-6
D_CONV = D_MODEL
CONV_K = 31
MLA_HEADS = 16
Q_LORA = 512
KV_LORA = 512
QK_NOPE = 128
QK_ROPE = 64
V_DIM = 128
ROPE_THETA = 10000.0
X_HEADS = 4
X_HEAD_DIM = 128
D_FF = 4 * D_MODEL

LANES = 128
SUBLANES = 8
QK_PAD = 256
CONV_HALO = 32
NEG_BIG = -1e30
VMEM_LIMIT = 56 * 1024 * 1024

COL_GLU_A = 0
COL_GLU_G = D_CONV
COL_LATENT = 2 * D_CONV
COL_KPE = COL_LATENT + Q_LORA + KV_LORA
COL_GATE = COL_KPE + QK_ROPE

F32 = jnp.float32
BF16 = jnp.bfloat16


def _cparams(*sem):
    return pltpu.CompilerParams(dimension_semantics=sem, vmem_limit_bytes=VMEM_LIMIT)


def _rms(x, g):
    return x * lax.rsqrt(jnp.mean(x * x, axis=-1, keepdims=True) + EPS) * g


def _layer_row(n, layer, col0=0):
    return pl.BlockSpec((None, 1, n), lambda *_: (layer, 0, col0 // n))


def _rope_table_kernel(pos_ref, freq_ref, cos_ref, sin_ref):
    ang = pos_ref[...].astype(F32) * freq_ref[...]
    lane = lax.broadcasted_iota(jnp.int32, ang.shape, 1)
    s = jnp.sin(ang)
    cos_ref[...] = jnp.where(lane < QK_ROPE, jnp.cos(ang), 0.0)
    sin_ref[...] = jnp.where(lane < QK_ROPE // 2, -s, jnp.where(lane < QK_ROPE, s, 0.0))


def rope_tables(pos_col, freq_row):
    S = pos_col.shape[0]
    tm = 1024
    tab = pl.BlockSpec((tm, LANES), lambda i: (i, 0))
    return pl.pallas_call(
        _rope_table_kernel,
        grid=(S // tm,),
        in_specs=[pl.BlockSpec((tm, 1), lambda i: (i, 0)),
                  pl.BlockSpec((1, LANES), lambda i: (0, 0))],
        out_specs=[tab, tab],
        out_shape=[jax.ShapeDtypeStruct((S, LANES), F32)] * 2,
        name="rope_table",
        compiler_params=_cparams("arbitrary"),
    )(pos_col, freq_row)


def _norm_kernel(x_ref, g_ref, o_ref):
    o_ref[...] = _rms(x_ref[...], g_ref[...]).astype(o_ref.dtype)


def rms_norm_rows(x, g3, layer, out_dtype, tm=512):
    M, D = x.shape
    return pl.pallas_call(
        _norm_kernel,
        grid=(M // tm,),
        in_specs=[pl.BlockSpec((tm, D), lambda i: (i, 0)), _layer_row(D, layer)],
        out_specs=pl.BlockSpec((tm, D), lambda i: (i, 0)),
        out_shape=jax.ShapeDtypeStruct((M, D), out_dtype),
        name="rms_norm",
        compiler_params=_cparams("arbitrary"),
    )(x, g3)


def _mm_kernel(*refs, a_index, n_a, n_extra, epilogue):
    n_b = len(a_index)
    acts = [r[...] for r in refs[:n_a]]
    accs = [jnp.dot(acts[ai], b[...], preferred_element_type=F32)
            for ai, b in zip(a_index, refs[n_a:n_a + n_b])]
    extras = [e[...] for e in refs[n_a + n_b:n_a + n_b + n_extra]]
    outs = epilogue(accs, extras)
    for o_ref, o in zip(refs[n_a + n_b + n_extra:], outs):
        o_ref[...] = o.astype(o_ref.dtype)


def matmul(name, acts, weights, extras, out_dtypes, epilogue, n_out, tm, tn):
    M = acts[0].shape[0]
    in_specs = [pl.BlockSpec((tm, a.shape[1]), lambda i, j: (i, 0)) for a in acts]
    operands = list(acts)
    for _, w, layer, col0 in weights:
        in_specs.append(pl.BlockSpec((None, w.shape[1], tn),
                                     lambda i, j, layer=layer, off=col0 // tn: (layer, 0, j + off)))
        operands.append(w)
    for e in extras:
        if e[0] == "row":
            _, arr, layer, col0 = e
            in_specs.append(pl.BlockSpec((None, 1, tn),
                                         lambda i, j, layer=layer, off=col0 // tn: (layer, 0, j + off)))
        else:
            _, arr, col0 = e
            in_specs.append(pl.BlockSpec((tm, tn), lambda i, j, off=col0 // tn: (i, j + off)))
        operands.append(arr)
    return pl.pallas_call(
        functools.partial(_mm_kernel, a_index=tuple(w[0] for w in weights), n_a=len(acts),
                          n_extra=len(extras), epilogue=epilogue),
        grid=(M // tm, n_out // tn),
        in_specs=in_specs,
        out_specs=[pl.BlockSpec((tm, tn), lambda i, j: (i, j)) for _ in out_dtypes],
        out_shape=[jax.ShapeDtypeStruct((M, n_out), dt) for dt in out_dtypes],
        name=name,
        compiler_params=_cparams("arbitrary", "arbitrary"),
    )(*operands)


def _epi_glu(accs, extras):
    return [accs[0] * jax.nn.sigmoid(accs[1])]


def _epi_gated(accs, extras):
    return [accs[0] * jax.nn.sigmoid(accs[1] + extras[0])]


def _epi_gated_merge(accs, extras):
    return [extras[1] + accs[0] * jax.nn.sigmoid(accs[1] + extras[0])]


def _epi_resid_norm(accs, extras):
    x = extras[0] + accs[0]
    return [x, _rms(x, extras[1])]


def _latent_kernel(h_ref, w_ref, wpe_ref, qg_ref, kvg_ref, cos_ref, sin_ref, cq_ref, ckv_ref, kpe_ref):
    h = h_ref[...]
    z = jnp.dot(h, w_ref[...], preferred_element_type=F32)
    cq_ref[...] = _rms(z[:, :Q_LORA], qg_ref[...]).astype(cq_ref.dtype)
    ckv_ref[...] = _rms(z[:, Q_LORA:], kvg_ref[...]).astype(ckv_ref.dtype)
    pe = jnp.dot(h, wpe_ref[...], preferred_element_type=F32)
    half = QK_ROPE // 2
    lane = lax.broadcasted_iota(jnp.int32, pe.shape, 1)
    swapped = jnp.where(lane < half, pltpu.roll(pe, LANES - half, 1), pltpu.roll(pe, half, 1))
    roped = pe * cos_ref[...] + swapped * sin_ref[...]
    kpe_ref[...] = jnp.where(lane < QK_ROPE, roped, 0.0).astype(kpe_ref.dtype)


def latent_proj(h, w_in_lo, layer, qg3, kvg3, cos, sin, tm=512):
    M, K = h.shape
    n_lat = Q_LORA + KV_LORA
    rows = lambda n: pl.BlockSpec((tm, n), lambda i: (i, 0))
    return pl.pallas_call(
        _latent_kernel,
        grid=(M // tm,),
        in_specs=[rows(K),
                  pl.BlockSpec((None, K, n_lat), lambda i: (layer, 0, COL_LATENT // n_lat)),
                  pl.BlockSpec((None, K, LANES), lambda i: (layer, 0, COL_KPE // LANES)),
                  _layer_row(Q_LORA, layer), _layer_row(KV_LORA, layer),
                  rows(LANES), rows(LANES)],
        out_specs=[rows(Q_LORA), rows(KV_LORA), rows(LANES)],
        out_shape=[jax.ShapeDtypeStruct((M, Q_LORA), BF16),
                   jax.ShapeDtypeStruct((M, KV_LORA), BF16),
                   jax.ShapeDtypeStruct((M, LANES), BF16)],
        name="latent_proj",
        compiler_params=_cparams("arbitrary"),
    )(h, w_in_lo, w_in_lo, qg3, kvg3, cos, sin)


def _qproj_kernel(c_ref, w_ref, cos_ref, sin_ref, q_ref):
    c = c_ref[...]
    cos, sin = cos_ref[...], sin_ref[...]
    scale = (QK_NOPE + QK_ROPE) ** -0.5 * math.log2(math.e)
    for h in range(MLA_HEADS):
        blk = jnp.dot(c, w_ref[:, h * QK_PAD:(h + 1) * QK_PAD], preferred_element_type=F32)
        q_ref[h, :, 0:QK_NOPE] = (blk[:, :QK_NOPE] * scale).astype(q_ref.dtype)
        pe = blk[:, QK_NOPE:]
        roped = pe * cos + pltpu.roll(pe, QK_ROPE, 1) * sin
        q_ref[h, :, QK_NOPE:QK_PAD] = (roped * scale).astype(q_ref.dtype)


def q_proj(cq, w, layer, cos, sin, tm=512):
    M, K = cq.shape
    rows = lambda n: pl.BlockSpec((tm, n), lambda i: (i, 0))
    return pl.pallas_call(
        _qproj_kernel,
        grid=(M // tm,),
        in_specs=[rows(K),
                  pl.BlockSpec((None, K, MLA_HEADS * QK_PAD), lambda i: (layer, 0, 0)),
                  rows(LANES), rows(LANES)],
        out_specs=pl.BlockSpec((MLA_HEADS, tm, QK_PAD), lambda i: (0, i, 0)),
        out_shape=jax.ShapeDtypeStruct((MLA_HEADS, M, QK_PAD), BF16),
        name="q_proj",
        compiler_params=_cparams("arbitrary"),
    )(cq, w, cos, sin)


def _kvproj_kernel(c_ref, w_ref, kpe_ref, k_ref, v_ref):
    c = c_ref[...]
    kpe = kpe_ref[...]
    width = QK_NOPE + V_DIM
    for h in range(MLA_HEADS):
        blk = jnp.dot(c, w_ref[:, h * width:(h + 1) * width], preferred_element_type=F32)
        k_ref[h, :, 0:QK_NOPE] = blk[:, :QK_NOPE].astype(k_ref.dtype)
        k_ref[h, :, QK_NOPE:QK_PAD] = kpe
        v_ref[:, h * V_DIM:(h + 1) * V_DIM] = blk[:, QK_NOPE:].astype(v_ref.dtype)


def kv_proj(ckv, w, layer, kpe, tm=512):
    M, K = ckv.shape
    rows = lambda n: pl.BlockSpec((tm, n), lambda i: (i, 0))
    return pl.pallas_call(
        _kvproj_kernel,
        grid=(M // tm,),
        in_specs=[rows(K),
                  pl.BlockSpec((None, K, MLA_HEADS * (QK_NOPE + V_DIM)), lambda i: (layer, 0, 0)),
                  rows(LANES)],
        out_specs=[pl.BlockSpec((MLA_HEADS, tm, QK_PAD), lambda i: (0, i, 0)),
                   rows(MLA_HEADS * V_DIM)],
        out_shape=[jax.ShapeDtypeStruct((MLA_HEADS, M, QK_PAD), BF16),
                   jax.ShapeDtypeStruct((M, MLA_HEADS * V_DIM), BF16)],
        name="kv_proj",
        compiler_params=_cparams("arbitrary"),
    )(ckv, w, kpe)


ATTN_RB = CHUNK
ATTN_UNROLL = 4


def _attn_kernel(q_ref, k_ref, v_ref, o_ref, s0_ref, s1_ref, p0_ref, p1_ref, a0_ref, a1_ref,
                 m_ref, vone_ref, acc_ref, *, tq):
    vone_ref[:, 0:V_DIM] = v_ref[...]
    vone_ref[:, V_DIM:] = jnp.ones((vone_ref.shape[0], V_DIM), vone_ref.dtype)
    lax.fori_loop(0, q_ref.shape[1] // tq,
                  functools.partial(_attn_query_block, q_ref, k_ref, vone_ref, o_ref, (s0_ref, s1_ref),
                                    (p0_ref, p1_ref), (a0_ref, a1_ref), m_ref, acc_ref, tq), 0)


def _attn_query_block(q_ref, k_ref, v_ref, o_ref, s_refs, p_refs, a_refs, m_ref, acc_ref, tq, qi, carry):
    ncol = tq // LANES
    q_rows = pl.ds(pl.multiple_of(qi * tq, tq), tq)

    def scores(j, par):
        start = pl.multiple_of(j * tq, tq)
        kb = k_ref[0, pl.ds(start, tq), :]
        s_refs[par][...] = lax.dot_general(q_ref[0, q_rows, :], kb, (((1,), (1,)), ((), ())),
                                           preferred_element_type=F32)

    def softmax(par, masked):
        s_ref, p_ref, a_ref = s_refs[par], p_refs[par], a_refs[par]
        for r in range(tq // ATTN_RB):
            rows = slice(r * ATTN_RB, (r + 1) * ATTN_RB)
            cols = []
            for c in range(ncol):
                first_chunk = (c * LANES) // CHUNK
                if masked and first_chunk > r:
                    cols.append(None)
                    continue
                sc = s_ref[rows, c * LANES:(c + 1) * LANES]
                if masked and first_chunk == r:
                    lane = lax.broadcasted_iota(jnp.int32, sc.shape, 1)
                    sc = jnp.where(lane < CHUNK, sc, NEG_BIG)
                cols.append(sc)
            live = [sc for sc in cols if sc is not None]
            mx = functools.reduce(jnp.maximum, live)
            m_prev = m_ref[rows, :]
            m_new = jnp.maximum(m_prev, jnp.max(mx, axis=-1, keepdims=True))
            alpha = jnp.exp2(m_prev - m_new)
            ps = [None if sc is None else jnp.exp2(sc - m_new) for sc in cols]
            m_ref[rows, :] = m_new
            a_ref[rows, :] = alpha
            for c, p in enumerate(ps):
                pc = jnp.zeros((ATTN_RB, LANES), BF16) if p is None else p.astype(BF16)
                p_ref[rows, c * LANES:(c + 1) * LANES] = pc

    def values(j, par):
        start = pl.multiple_of(j * tq, tq)
        vb = v_ref[pl.ds(start, tq), :]
        alpha = a_refs[par][...]
        acc_ref[...] = (jnp.concatenate([alpha, alpha], axis=1) * acc_ref[...]
                        + jnp.dot(p_refs[par][...], vb, preferred_element_type=F32))

    def steady_tick(t, par):
        scores(t, par)
        softmax(1 - par, False)
        values(t - 2, par)

    m_ref[...] = jnp.full(m_ref.shape, NEG_BIG, F32)
    acc_ref[...] = jnp.zeros(acc_ref.shape, F32)

    @pl.when(qi >= 1)
    def _():
        scores(0, 0)
        scores(1, 1)
        softmax(0, False)

    n_steady = jnp.maximum(qi - 1, 0)

    def unrolled(i, carry):
        t = 2 + ATTN_UNROLL * i
        for u in range(ATTN_UNROLL):
            steady_tick(t + u, u % 2)
        return carry

    lax.fori_loop(0, n_steady // ATTN_UNROLL, unrolled, 0)
    t_pair = 2 + (n_steady // ATTN_UNROLL) * ATTN_UNROLL
    n_left = n_steady % ATTN_UNROLL

    def pair(i, carry):
        steady_tick(t_pair + 2 * i, 0)
        steady_tick(t_pair + 2 * i + 1, 1)
        return carry

    lax.fori_loop(0, n_left // 2, pair, 0)

    @pl.when(n_left % 2 == 1)
    def _():
        steady_tick(qi, 0)

    @pl.when(qi == 0)
    def _():
        scores(0, 0)
        softmax(0, True)
        values(0, 0)

    @pl.when(jnp.logical_and(qi > 0, qi % 2 == 0))
    def _():
        softmax(0, True)
        values(qi - 1, 1)
        values(qi, 0)

    @pl.when(qi % 2 == 1)
    def _():
        softmax(1, True)
        values(qi - 1, 0)
        values(qi, 1)

    o_ref[q_rows, :] = (acc_ref[:, 0:V_DIM] / acc_ref[:, V_DIM:]).astype(o_ref.dtype)
    return carry


def mla_attention(q, k, v, tq=512):
    H, S, _ = q.shape
    par2 = lambda shape, dt: [pltpu.VMEM(shape, dt), pltpu.VMEM(shape, dt)]
    return pl.pallas_call(
        functools.partial(_attn_kernel, tq=tq),
        grid=(H,),
        in_specs=[pl.BlockSpec((1, S, QK_PAD), lambda h: (h, 0, 0)),
                  pl.BlockSpec((1, S, QK_PAD), lambda h: (h, 0, 0)),
                  pl.BlockSpec((S, V_DIM), lambda h: (0, h))],
        out_specs=pl.BlockSpec((S, V_DIM), lambda h: (0, h)),
        out_shape=jax.ShapeDtypeStruct((S, H * V_DIM), BF16),
        scratch_shapes=(par2((tq, tq), F32)
                        + par2((tq, tq), BF16)
                        + par2((tq, LANES), F32)
                        + [pltpu.VMEM((tq, LANES), F32),
                           pltpu.VMEM((S, 2 * V_DIM), BF16),
                           pltpu.VMEM((tq, 2 * V_DIM), F32)]),
        name="mla_attention",
        compiler_params=_cparams("arbitrary"),
    )(q, k, v)


CONV_RB = 64
CONV_CB = LANES


def _conv_kernel(cur_ref, halo_ref, w_ref, b_ref, g_ref, beta_ref, o_ref, buf_ref, sh_ref, acc_ref, wb_ref, *, tm):
    i = pl.program_id(0)

    @pl.when(i == 0)
    def _():
        for k in range(CONV_K):
            wb_ref[k] = jnp.broadcast_to(w_ref[k:k + 1, :], (SUBLANES, D_CONV))

    buf_ref[0:CONV_HALO, :] = jnp.where(i > 0, halo_ref[...], 0.0)
    buf_ref[CONV_HALO:CONV_HALO + tm, :] = cur_ref[...]
    n_buf = tm + CONV_HALO
    n_sh = n_buf - SUBLANES
    for c0 in range(0, D_CONV, LANES):
        x = buf_ref[:, c0:c0 + LANES]
        for b in range(1, SUBLANES):
            sh_ref[b - 1, :, c0:c0 + LANES] = pltpu.roll(x, n_buf - b, 0)[0:n_sh, :]
    off = CONV_HALO - (CONV_K - 1)
    for c0 in range(0, D_CONV, CONV_CB):
        cols = slice(c0, c0 + CONV_CB)
        w_all = wb_ref[:, :, cols]
        bias = jnp.broadcast_to(b_ref[:, cols], (CONV_RB, CONV_CB))

        def row_block(rb, carry, cols=cols, w_all=w_all, bias=bias):
            r0 = pl.multiple_of(rb * CONV_RB, CONV_RB)
            acc = bias
            for b in range(SUBLANES):
                taps = [k for k in range(CONV_K) if (off + k) % SUBLANES == b]
                a_lo, a_hi = (off + taps[0]) // SUBLANES, (off + taps[-1]) // SUBLANES
                rows = pl.ds(r0 + a_lo * SUBLANES, (a_hi - a_lo) * SUBLANES + CONV_RB)
                slab = buf_ref[rows, cols] if b == 0 else sh_ref[b - 1, rows, cols]
                for k in taps:
                    s0 = ((off + k) // SUBLANES - a_lo) * SUBLANES
                    wk = jnp.concatenate([w_all[k]] * (CONV_RB // SUBLANES), axis=0)
                    acc = acc + slab[s0:s0 + CONV_RB, :] * wk
            acc_ref[pl.ds(r0, CONV_RB), cols] = acc
            return carry

        lax.fori_loop(0, tm // CONV_RB, row_block, 0)
    y = acc_ref[...]
    mu = jnp.mean(y, axis=-1, keepdims=True)
    yc = y - mu
    yn = yc * lax.rsqrt(jnp.mean(yc * yc, axis=-1, keepdims=True) + EPS) * g_ref[...] + beta_ref[...]
    o_ref[...] = (yn * jax.nn.sigmoid(yn)).astype(o_ref.dtype)


def conv_ln_silu(hglu, conv_w, conv_b3, ln_g3, ln_b3, layer, tm=256):
    S, D = hglu.shape
    row = _layer_row(D, layer)
    return pl.pallas_call(
        functools.partial(_conv_kernel, tm=tm),
        grid=(S // tm,),
        in_specs=[pl.BlockSpec((tm, D), lambda i: (i, 0)),
                  pl.BlockSpec((CONV_HALO, D), lambda i: (jnp.maximum(i * (tm // CONV_HALO) - 1, 0), 0)),
                  pl.BlockSpec((None, CONV_K, D), lambda i: (layer, 0, 0)),
                  row, row, row],
        out_specs=pl.BlockSpec((tm, D), lambda i: (i, 0)),
        out_shape=jax.ShapeDtypeStruct((S, D), BF16),
        scratch_shapes=[pltpu.VMEM((CONV_HALO + tm, D), F32),
                        pltpu.VMEM((SUBLANES - 1, CONV_HALO + tm - SUBLANES, D), F32),
                        pltpu.VMEM((tm, D), F32),
                        pltpu.VMEM((CONV_K, SUBLANES, D), F32)],
        name="conv_ln_silu",
        compiler_params=_cparams("arbitrary"),
    )(hglu, hglu, conv_w, conv_b3, ln_g3, ln_b3)


def _memkv_kernel(mem_ref, g_ref, w_ref, k_ref, v_ref):
    mn = _rms(mem_ref[...], g_ref[...]).astype(BF16)
    kv = jnp.dot(mn, w_ref[...], preferred_element_type=F32)
    n = X_HEADS * X_HEAD_DIM
    k_ref[...] = kv[:, :n].astype(k_ref.dtype)
    v_ref[...] = kv[:, n:].astype(v_ref.dtype)


def mem_kv(mem, g3, w, layer):
    n = X_HEADS * X_HEAD_DIM
    D = mem.shape[1]
    full = lambda shape: pl.BlockSpec(shape, lambda i: (0, 0))
    return pl.pallas_call(
        _memkv_kernel,
        grid=(1,),
        in_specs=[full(mem.shape), _layer_row(D, layer),
                  pl.BlockSpec((None, D, 2 * n), lambda i: (layer, 0, 0))],
        out_specs=[full((N_MEM, n)), full((N_MEM, n))],
        out_shape=[jax.ShapeDtypeStruct((N_MEM, n), BF16)] * 2,
        name="mem_kv",
        compiler_params=_cparams("arbitrary"),
    )(mem, g3, w)


def _xattn_kernel(h_ref, wq_ref, k_ref, v_ref, wo_ref, x_ref, g_ref, xo_ref, ho_ref):
    q = jnp.dot(h_ref[...], wq_ref[...], preferred_element_type=F32) * (X_HEAD_DIM ** -0.5)
    q = q.astype(BF16)
    outs = []
    for h in range(X_HEADS):
        sl = slice(h * X_HEAD_DIM, (h + 1) * X_HEAD_DIM)
        s = lax.dot_general(q[:, sl], k_ref[:, sl], (((1,), (1,)), ((), ())), preferred_element_type=F32)
        p = jnp.exp(s - jnp.max(s, axis=-1, keepdims=True))
        l = jnp.sum(p, axis=-1, keepdims=True)
        o = jnp.dot(p.astype(BF16), v_ref[:, sl], preferred_element_type=F32) / l
        outs.append(o.astype(BF16))
    o = jnp.concatenate(outs, axis=-1)
    x = x_ref[...] + jnp.dot(o, wo_ref[...], preferred_element_type=F32)
    xo_ref[...] = x
    ho_ref[...] = _rms(x, g_ref[...]).astype(ho_ref.dtype)


def cross_attention(h, wq, mk, mv, wo, x, g3, layer, tm=512):
    M, D = h.shape
    n = X_HEADS * X_HEAD_DIM
    const = lambda shape: pl.BlockSpec(shape, lambda i: (0, 0))
    rows = lambda w: pl.BlockSpec((tm, w), lambda i: (i, 0))
    return pl.pallas_call(
        _xattn_kernel,
        grid=(M // tm,),
        in_specs=[rows(D),
                  pl.BlockSpec((None, D, n), lambda i: (layer, 0, 0)),
                  const((N_MEM, n)), const((N_MEM, n)),
                  pl.BlockSpec((None, n, D), lambda i: (layer, 0, 0)),
                  rows(D), _layer_row(D, layer)],
        out_specs=[rows(D), rows(D)],
        out_shape=[jax.ShapeDtypeStruct((M, D), F32), jax.ShapeDtypeStruct((M, D), BF16)],
        name="cross_attention",
        compiler_params=_cparams("arbitrary"),
    )(h, wq, mk, mv, wo, x, g3)


def _mlp_kernel(h_ref, w1_ref, w2_ref, x_ref, g_ref, xo_ref, ho_ref):
    f = pl.program_id(1)

    @pl.when(f == 0)
    def _():
        xo_ref[...] = x_ref[...]

    a = jnp.dot(h_ref[...], w1_ref[...], preferred_element_type=F32)
    a = jnp.square(jnp.maximum(a, 0.0)).astype(BF16)
    xo_ref[...] += jnp.dot(a, w2_ref[...], preferred_element_type=F32)

    @pl.when(f == pl.num_programs(1) - 1)
    def _():
        ho_ref[...] = _rms(xo_ref[...], g_ref[...]).astype(ho_ref.dtype)


def mlp(h, w1, w2, layer, x, g3, g_layer, out_dtype, tm=512, tf=1024):
    M, D = h.shape
    F = w1.shape[2]
    rows = pl.BlockSpec((tm, D), lambda i, f: (i, 0))
    return pl.pallas_call(
        _mlp_kernel,
        grid=(M // tm, F // tf),
        in_specs=[rows,
                  pl.BlockSpec((None, D, tf), lambda i, f: (layer, 0, f)),
                  pl.BlockSpec((None, tf, D), lambda i, f: (layer, f, 0)),
                  rows,
                  _layer_row(D, g_layer)],
        out_specs=[rows, rows],
        out_shape=[jax.ShapeDtypeStruct((M, D), F32), jax.ShapeDtypeStruct((M, D), out_dtype)],
        name="mlp",
        compiler_params=_cparams("arbitrary", "arbitrary"),
    )(h, w1, w2, x, g3)


def _permute_q_weight(w_uq):
    L, K, _ = w_uq.shape
    half = QK_ROPE // 2
    w = w_uq.reshape(L, K, MLA_HEADS, QK_NOPE + QK_ROPE)
    pe = w[..., QK_NOPE:]
    w = jnp.concatenate([w, pe[..., half:], pe[..., :half]], axis=-1)
    return w.reshape(L, K, MLA_HEADS * QK_PAD)


def kernel(x, mem, positions, norm_mix_g, w_in, b_gate, conv_w, conv_b, conv_ln_g, conv_ln_b, w_conv_out, q_norm_g, w_uq, kv_norm_g, w_ukv, w_mla_out, w_out, norm_mem_g, mem_norm_g, w_xq, w_xkv, w_xo, norm_ffn_g, w_ff1, w_ff2, final_norm_g):
    B, S, D = x.shape
    assert (B, S, D) == (1, SEQ, D_MODEL)
    x2d = x.reshape(S, D)
    mem2d = mem.reshape(N_MEM, D)
    vec3 = lambda v: v.reshape(v.shape[0], 1, v.shape[1])

    inv_freq = 1.0 / (ROPE_THETA ** (jnp.arange(0, QK_ROPE, 2, dtype=F32) / QK_ROPE))
    cos, sin = rope_tables(positions.reshape(S, 1), jnp.tile(inv_freq, 4).reshape(1, LANES))

    w_in_lo = w_in.astype(BF16)
    w_gate = w_in_lo[:, :, COL_GATE:]
    w_q = _permute_q_weight(w_uq).astype(BF16)
    w_kv, w_co, w_mo, w_o = (w.astype(BF16) for w in (w_ukv, w_conv_out, w_mla_out, w_out))
    w_q_x, w_kv_x, w_o_x = (w.astype(BF16) for w in (w_xq, w_xkv, w_xo))
    w_1, w_2 = w_ff1.astype(BF16), w_ff2.astype(BF16)

    g_mix, g_mem, g_memn, g_ffn = vec3(norm_mix_g), vec3(norm_mem_g), vec3(mem_norm_g), vec3(norm_ffn_g)
    g_q, g_kv, b_g = vec3(q_norm_g), vec3(kv_norm_g), vec3(b_gate)
    c_b, ln_g, ln_b = vec3(conv_b), vec3(conv_ln_g), vec3(conv_ln_b)
    g_final = final_norm_g.reshape(1, 1, D)

    h = rms_norm_rows(x2d, g_mix, 0, BF16)
    xcur = x2d
    for l in range(DEPTH):
        (hglu,) = matmul("glu_proj", [h], [(0, w_in_lo, l, COL_GLU_A), (0, w_in_lo, l, COL_GLU_G)], [],
                         [F32], _epi_glu, D_CONV, 1024, 512)
        cq, ckv, kpe = latent_proj(h, w_in_lo, l, g_q, g_kv, cos, sin)

        hc = conv_ln_silu(hglu, conv_w, c_b, ln_g, ln_b, l)
        (gy,) = matmul("conv_out_gated", [hc, h], [(0, w_co, l, 0), (1, w_gate, l, 0)],
                       [("row", b_g, l, 0)], [BF16], _epi_gated, D, 1024, 512)

        q = q_proj(cq, w_q, l, cos, sin)
        k, v = kv_proj(ckv, w_kv, l, kpe)
        o = mla_attention(q, k, v)
        (merged,) = matmul("mla_out_gated_merge", [o, h], [(0, w_mo, l, 0), (1, w_gate, l, D)],
                           [("row", b_g, l, D), ("tile", gy, 0)], [BF16], _epi_gated_merge, D, 1024, 512)

        xcur, hm = matmul("mix_out", [merged], [(0, w_o, l, 0)], [("tile", xcur, 0), ("row", g_mem, l, 0)],
                          [F32, BF16], _epi_resid_norm, D, 512, D)

        mk, mv = mem_kv(mem2d, g_memn, w_kv_x, l)
        xcur, hf = cross_attention(hm, w_q_x, mk, mv, w_o_x, xcur, g_ffn, l)

        last = l == DEPTH - 1
        xcur, h = mlp(hf, w_1, w_2, l, xcur, g_final if last else g_mix, 0 if last else l + 1,
                      F32 if last else BF16)
    return h.reshape(B, S, D)
```

```python
import functools
import math

import jax
import jax.numpy as jnp
from jax import lax
from jax.experimental import pallas as pl
from jax.experimental.pallas import tpu as pltpu

D_MODEL = 2048
SEQ = 8192
DEPTH = 2
CHUNK = 64
N_MEM = 256
EPS = 1e-6
D_CONV = D_MODEL
CONV_K = 31
MLA_HEADS = 16
Q_LORA = 512
KV_LORA = 512
QK_NOPE = 128
QK_ROPE = 64
V_DIM = 128
ROPE_THETA = 10000.0
X_HEADS = 4
X_HEAD_DIM = 128
D_FF = 4 * D_MODEL

LANES = 128
SUBLANES = 8
QK_PAD = 256
CONV_HALO = 32
NEG_BIG = -1e30
VMEM_LIMIT = 56 * 1024 * 1024

COL_GLU_A = 0
COL_GLU_G = D_CONV
COL_LATENT = 2 * D_CONV
COL_KPE = COL_LATENT + Q_LORA + KV_LORA
COL_GATE = COL_KPE + QK_ROPE

F32 = jnp.float32
BF16 = jnp.bfloat16


def _cparams(*sem):
    return pltpu.CompilerParams(dimension_semantics=sem, vmem_limit_bytes=VMEM_LIMIT)


def _rms(x, g):
    return x * lax.rsqrt(jnp.mean(x * x, axis=-1, keepdims=True) + EPS) * g


def _layer_row(n, layer, col0=0):
    return pl.BlockSpec((None, 1, n), lambda *_: (layer, 0, col0 // n))


def _rope_table_kernel(pos_ref, freq_ref, cos_ref, sin_ref):
    ang = pos_ref[...].astype(F32) * freq_ref[...]
    lane = lax.broadcasted_iota(jnp.int32, ang.shape, 1)
    s = jnp.sin(ang)
    cos_ref[...] = jnp.where(lane < QK_ROPE, jnp.cos(ang), 0.0)
    sin_ref[...] = jnp.where(lane < QK_ROPE // 2, -s, jnp.where(lane < QK_ROPE, s, 0.0))


def rope_tables(pos_col, freq_row):
    S = pos_col.shape[0]
    tm = 1024
    tab = pl.BlockSpec((tm, LANES), lambda i: (i, 0))
    return pl.pallas_call(
        _rope_table_kernel,
        grid=(S // tm,),
        in_specs=[pl.BlockSpec((tm, 1), lambda i: (i, 0)),
                  pl.BlockSpec((1, LANES), lambda i: (0, 0))],
        out_specs=[tab, tab],
        out_shape=[jax.ShapeDtypeStruct((S, LANES), F32)] * 2,
        name="rope_table",
        compiler_params=_cparams("arbitrary"),
    )(pos_col, freq_row)


def _norm_kernel(x_ref, g_ref, o_ref):
    o_ref[...] = _rms(x_ref[...], g_ref[...]).astype(o_ref.dtype)


def rms_norm_rows(x, g3, layer, out_dtype, tm=512):
    M, D = x.shape
    return pl.pallas_call(
        _norm_kernel,
        grid=(M // tm,),
        in_specs=[pl.BlockSpec((tm, D), lambda i: (i, 0)), _layer_row(D, layer)],
        out_specs=pl.BlockSpec((tm, D), lambda i: (i, 0)),
        out_shape=jax.ShapeDtypeStruct((M, D), out_dtype),
        name="rms_norm",
        compiler_params=_cparams("arbitrary"),
    )(x, g3)


def _mm_kernel(*refs, a_index, n_a, n_extra, epilogue):
    n_b = len(a_index)
    acts = [r[...] for r in refs[:n_a]]
    accs = [jnp.dot(acts[ai], b[...], preferred_element_type=F32)
            for ai, b in zip(a_index, refs[n_a:n_a + n_b])]
    extras = [e[...] for e in refs[n_a + n_b:n_a + n_b + n_extra]]
    outs = epilogue(accs, extras)
    for o_ref, o in zip(refs[n_a + n_b + n_extra:], outs):
        o_ref[...] = o.astype(o_ref.dtype)


def matmul(name, acts, weights, extras, out_dtypes, epilogue, n_out, tm, tn):
    M = acts[0].shape[0]
    in_specs = [pl.BlockSpec((tm, a.shape[1]), lambda i, j: (i, 0)) for a in acts]
    operands = list(acts)
    for _, w, layer, col0 in weights:
        in_specs.append(pl.BlockSpec((None, w.shape[1], tn),
                                     lambda i, j, layer=layer, off=col0 // tn: (layer, 0, j + off)))
        operands.append(w)
    for e in extras:
        if e[0] == "row":
            _, arr, layer, col0 = e
            in_specs.append(pl.BlockSpec((None, 1, tn),
                                         lambda i, j, layer=layer, off=col0 // tn: (layer, 0, j + off)))
        else:
            _, arr, col0 = e
            in_specs.append(pl.BlockSpec((tm, tn), lambda i, j, off=col0 // tn: (i, j + off)))
        operands.append(arr)
    return pl.pallas_call(
        functools.partial(_mm_kernel, a_index=tuple(w[0] for w in weights), n_a=len(acts),
                          n_extra=len(extras), epilogue=epilogue),
        grid=(M // tm, n_out // tn),
        in_specs=in_specs,
        out_specs=[pl.BlockSpec((tm, tn), lambda i, j: (i, j)) for _ in out_dtypes],
        out_shape=[jax.ShapeDtypeStruct((M, n_out), dt) for dt in out_dtypes],
        name=name,
        compiler_params=_cparams("arbitrary", "arbitrary"),
    )(*operands)


def _epi_glu(accs, extras):
    return [accs[0] * jax.nn.sigmoid(accs[1])]


def _epi_gated(accs, extras):
    return [accs[0] * jax.nn.sigmoid(accs[1] + extras[0])]


def _epi_gated_merge(accs, extras):
    return [extras[1] + accs[0] * jax.nn.sigmoid(accs[1] + extras[0])]


def _epi_resid_norm(accs, extras):
    x = extras[0] + accs[0]
    return [x, _rms(x, extras[1])]


def _latent_kernel(h_ref, w_ref, wpe_ref, qg_ref, kvg_ref, cos_ref, sin_ref, cq_ref, ckv_ref, kpe_ref):
    h = h_ref[...]
    z = jnp.dot(h, w_ref[...], preferred_element_type=F32)
    cq_ref[...] = _rms(z[:, :Q_LORA], qg_ref[...]).astype(cq_ref.dtype)
    ckv_ref[...] = _rms(z[:, Q_LORA:], kvg_ref[...]).astype(ckv_ref.dtype)
    pe = jnp.dot(h, wpe_ref[...], preferred_element_type=F32)
    half = QK_ROPE // 2
    lane = lax.broadcasted_iota(jnp.int32, pe.shape, 1)
    swapped = jnp.where(lane < half, pltpu.roll(pe, LANES - half, 1), pltpu.roll(pe, half, 1))
    roped = pe * cos_ref[...] + swapped * sin_ref[...]
    kpe_ref[...] = jnp.where(lane < QK_ROPE, roped, 0.0).astype(kpe_ref.dtype)


def latent_proj(h, w_in_lo, layer, qg3, kvg3, cos, sin, tm=512):
    M, K = h.shape
    n_lat = Q_LORA + KV_LORA
    rows = lambda n: pl.BlockSpec((tm, n), lambda i: (i, 0))
    return pl.pallas_call(
        _latent_kernel,
        grid=(M // tm,),
        in_specs=[rows(K),
                  pl.BlockSpec((None, K, n_lat), lambda i: (layer, 0, COL_LATENT // n_lat)),
                  pl.BlockSpec((None, K, LANES), lambda i: (layer, 0, COL_KPE // LANES)),
                  _layer_row(Q_LORA, layer), _layer_row(KV_LORA, layer),
                  rows(LANES), rows(LANES)],
        out_specs=[rows(Q_LORA), rows(KV_LORA), rows(LANES)],
        out_shape=[jax.ShapeDtypeStruct((M, Q_LORA), BF16),
                   jax.ShapeDtypeStruct((M, KV_LORA), BF16),
                   jax.ShapeDtypeStruct((M, LANES), BF16)],
        name="latent_proj",
        compiler_params=_cparams("arbitrary"),
    )(h, w_in_lo, w_in_lo, qg3, kvg3, cos, sin)


def _qproj_kernel(c_ref, w_ref, cos_ref, sin_ref, q_ref):
    c = c_ref[...]
    cos, sin = cos_ref[...], sin_ref[...]
    scale = (QK_NOPE + QK_ROPE) ** -0.5 * math.log2(math.e)
    for h in range(MLA_HEADS):
        blk = jnp.dot(c, w_ref[:, h * QK_PAD:(h + 1) * QK_PAD], preferred_element_type=F32)
        q_ref[h, :, 0:QK_NOPE] = (blk[:, :QK_NOPE] * scale).astype(q_ref.dtype)
        pe = blk[:, QK_NOPE:]
        roped = pe * cos + pltpu.roll(pe, QK_ROPE, 1) * sin
        q_ref[h, :, QK_NOPE:QK_PAD] = (roped * scale).astype(q_ref.dtype)


def q_proj(cq, w, layer, cos, sin, tm=512):
    M, K = cq.shape
    rows = lambda n: pl.BlockSpec((tm, n), lambda i: (i, 0))
    return pl.pallas_call(
        _qproj_kernel,
        grid=(M // tm,),
        in_specs=[rows(K),
                  pl.BlockSpec((None, K, MLA_HEADS * QK_PAD), lambda i: (layer, 0, 0)),
                  rows(LANES), rows(LANES)],
        out_specs=pl.BlockSpec((MLA_HEADS, tm, QK_PAD), lambda i: (0, i, 0)),
        out_shape=jax.ShapeDtypeStruct((MLA_HEADS, M, QK_PAD), BF16),
        name="q_proj",
        compiler_params=_cparams("arbitrary"),
    )(cq, w, cos, sin)


def _kvproj_kernel(c_ref, w_ref, kpe_ref, k_ref, v_ref):
    c = c_ref[...]
    kpe = kpe_ref[...]
    width = QK_NOPE + V_DIM
    for h in range(MLA_HEADS):
        blk = jnp.dot(c, w_ref[:, h * width:(h + 1) * width], preferred_element_type=F32)
        k_ref[h, :, 0:QK_NOPE] = blk[:, :QK_NOPE].astype(k_ref.dtype)
        k_ref[h, :, QK_NOPE:QK_PAD] = kpe
        v_ref[:, h * V_DIM:(h + 1) * V_DIM] = blk[:, QK_NOPE:].astype(v_ref.dtype)


def kv_proj(ckv, w, layer, kpe, tm=512):
    M, K = ckv.shape
    rows = lambda n: pl.BlockSpec((tm, n), lambda i: (i, 0))
    return pl.pallas_call(
        _kvproj_kernel,
        grid=(M // tm,),
        in_specs=[rows(K),
                  pl.BlockSpec((None, K, MLA_HEADS * (QK_NOPE + V_DIM)), lambda i: (layer, 0, 0)),
                  rows(LANES)],
        out_specs=[pl.BlockSpec((MLA_HEADS, tm, QK_PAD), lambda i: (0, i, 0)),
                   rows(MLA_HEADS * V_DIM)],
        out_shape=[jax.ShapeDtypeStruct((MLA_HEADS, M, QK_PAD), BF16),
                   jax.ShapeDtypeStruct((M, MLA_HEADS * V_DIM), BF16)],
        name="kv_proj",
        compiler_params=_cparams("arbitrary"),
    )(ckv, w, kpe)


ATTN_RB = CHUNK
ATTN_UNROLL = 4


def _attn_kernel(q_ref, k_ref, v_ref, o_ref, vone_ref, *set_refs, tq):
    vone_ref[:, 0:V_DIM] = v_ref[...]
    vone_ref[:, V_DIM:] = jnp.ones((vone_ref.shape[0], V_DIM), vone_ref.dtype)
    n_set = len(set_refs) // 2
    sets = [_AttnBlock(q_ref, k_ref, vone_ref, o_ref, set_refs[i * n_set:(i + 1) * n_set], tq, i)
            for i in range(2)]
    n_q = q_ref.shape[1] // tq
    assert n_q % 2 == 0

    sets[0].first_block()
    sets[1].fill(jnp.int32(1))

    def query_block_pair(qq, carry):
        qa, qb = 2 * qq, 2 * qq + 1
        sets[1].drain(qa - 1)
        sets[0].fill(qa)
        sets[0].steady(qa)
        sets[0].drain(qa)
        sets[1].fill(qb)
        sets[1].steady(qb)
        return carry

    lax.fori_loop(1, n_q // 2, query_block_pair, 0)
    sets[1].drain(jnp.int32(n_q - 1))


class _AttnBlock:
    def __init__(self, q_ref, k_ref, v_ref, o_ref, refs, tq, par):
        s0, s1, p0, p1, a0, a1, self.m_ref, self.acc_ref = refs
        self.s_refs, self.p_refs, self.a_refs = (s0, s1), (p0, p1), (a0, a1)
        self.q_ref, self.k_ref, self.v_ref, self.o_ref = q_ref, k_ref, v_ref, o_ref
        self.tq, self.par = tq, par

    def _q_rows(self, qi):
        return pl.ds(pl.multiple_of(qi * self.tq, self.tq), self.tq)

    def _init(self):
        self.m_ref[...] = jnp.full(self.m_ref.shape, NEG_BIG, F32)
        self.acc_ref[...] = jnp.zeros(self.acc_ref.shape, F32)

    def scores(self, qi, j, par):
        start = pl.multiple_of(j * self.tq, self.tq)
        kb = self.k_ref[0, pl.ds(start, self.tq), :]
        self.s_refs[par][...] = lax.dot_general(self.q_ref[0, self._q_rows(qi), :], kb,
                                                (((1,), (1,)), ((), ())), preferred_element_type=F32)

    def softmax(self, par, masked):
        s_ref, p_ref, a_ref, m_ref = self.s_refs[par], self.p_refs[par], self.a_refs[par], self.m_ref
        for r in range(self.tq // ATTN_RB):
            rows = slice(r * ATTN_RB, (r + 1) * ATTN_RB)
            cols = []
            for c in range(self.tq // LANES):
                first_chunk = (c * LANES) // CHUNK
                if masked and first_chunk > r:
                    cols.append(None)
                    continue
                sc = s_ref[rows, c * LANES:(c + 1) * LANES]
                if masked and first_chunk == r:
                    lane = lax.broadcasted_iota(jnp.int32, sc.shape, 1)
                    sc = jnp.where(lane < CHUNK, sc, NEG_BIG)
                cols.append(sc)
            mx = functools.reduce(jnp.maximum, [sc for sc in cols if sc is not None])
            m_prev = m_ref[rows, :]
            m_new = jnp.maximum(m_prev, jnp.max(mx, axis=-1, keepdims=True))
            m_ref[rows, :] = m_new
            a_ref[rows, :] = jnp.exp2(m_prev - m_new)
            for c, sc in enumerate(cols):
                pc = jnp.zeros((ATTN_RB, LANES), BF16) if sc is None else jnp.exp2(sc - m_new).astype(BF16)
                p_ref[rows, c * LANES:(c + 1) * LANES] = pc

    def values(self, j, par):
        start = pl.multiple_of(j * self.tq, self.tq)
        vb = self.v_ref[pl.ds(start, self.tq), :]
        alpha = self.a_refs[par][...]
        self.acc_ref[...] = (jnp.concatenate([alpha, alpha], axis=1) * self.acc_ref[...]
                             + jnp.dot(self.p_refs[par][...], vb, preferred_element_type=F32))

    def _write(self, qi):
        acc = self.acc_ref
        self.o_ref[self._q_rows(qi), :] = (acc[:, 0:V_DIM] / acc[:, V_DIM:]).astype(self.o_ref.dtype)

    def first_block(self):
        qi = jnp.int32(0)
        self._init()
        self.scores(qi, qi, 0)
        self.softmax(0, True)
        self.values(qi, 0)
        self._write(qi)

    def fill(self, qi):
        self._init()
        self.scores(qi, 0, 0)
        self.scores(qi, 1, 1)
        self.softmax(0, False)

    def _tick(self, qi, t, par):
        self.scores(qi, t, par)
        self.softmax(1 - par, False)
        self.values(t - 2, par)

    def steady(self, qi):
        n_steady = jnp.maximum(qi - 1, 0)

        def unrolled(i, carry):
            for u in range(ATTN_UNROLL):
                self._tick(qi, 2 + ATTN_UNROLL * i + u, u % 2)
            return carry

        lax.fori_loop(0, n_steady // ATTN_UNROLL, unrolled, 0)
        t_pair = 2 + (n_steady // ATTN_UNROLL) * ATTN_UNROLL
        n_left = n_steady % ATTN_UNROLL

        def pair(i, carry):
            self._tick(qi, t_pair + 2 * i, 0)
            self._tick(qi, t_pair + 2 * i + 1, 1)
            return carry

        lax.fori_loop(0, n_left // 2, pair, 0)

        @pl.when(n_left % 2 == 1)
        def _():
            self._tick(qi, qi, 0)

    def drain(self, qi):
        self.softmax(self.par, True)
        self.values(qi - 1, 1 - self.par)
        self.values(qi, self.par)
        self._write(qi)


def mla_attention(q, k, v, tq=512):
    H, S, _ = q.shape
    buffer_set = ([pltpu.VMEM((tq, tq), F32)] * 2
                  + [pltpu.VMEM((tq, tq), BF16)] * 2
                  + [pltpu.VMEM((tq, LANES), F32)] * 2
                  + [pltpu.VMEM((tq, LANES), F32),
                     pltpu.VMEM((tq, 2 * V_DIM), F32)])
    return pl.pallas_call(
        functools.partial(_attn_kernel, tq=tq),
        grid=(H,),
        in_specs=[pl.BlockSpec((1, S, QK_PAD), lambda h: (h, 0, 0)),
                  pl.BlockSpec((1, S, QK_PAD), lambda h: (h, 0, 0)),
                  pl.BlockSpec((S, V_DIM), lambda h: (0, h))],
        out_specs=pl.BlockSpec((S, V_DIM), lambda h: (0, h)),
        out_shape=jax.ShapeDtypeStruct((S, H * V_DIM), BF16),
        scratch_shapes=[pltpu.VMEM((S, 2 * V_DIM), BF16)] + buffer_set + buffer_set,
        name="mla_attention",
        compiler_params=_cparams("arbitrary"),
    )(q, k, v)


CONV_RB = 64
CONV_CB = LANES


def _conv_kernel(cur_ref, halo_ref, w_ref, b_ref, g_ref, beta_ref, o_ref, buf_ref, sh_ref, acc_ref, wb_ref, *, tm):
    i = pl.program_id(0)

    @pl.when(i == 0)
    def _():
        for k in range(CONV_K):
            wb_ref[k] = jnp.broadcast_to(w_ref[k:k + 1, :], (SUBLANES, D_CONV))

    buf_ref[0:CONV_HALO, :] = jnp.where(i > 0, halo_ref[...], 0.0)
    buf_ref[CONV_HALO:CONV_HALO + tm, :] = cur_ref[...]
    n_buf = tm + CONV_HALO
    n_sh = n_buf - SUBLANES
    for c0 in range(0, D_CONV, LANES):
        x = buf_ref[:, c0:c0 + LANES]
        for b in range(1, SUBLANES):
            sh_ref[b - 1, :, c0:c0 + LANES] = pltpu.roll(x, n_buf - b, 0)[0:n_sh, :]
    off = CONV_HALO - (CONV_K - 1)
    for c0 in range(0, D_CONV, CONV_CB):
        cols = slice(c0, c0 + CONV_CB)
        w_all = wb_ref[:, :, cols]
        bias = jnp.broadcast_to(b_ref[:, cols], (CONV_RB, CONV_CB))

        def row_block(rb, carry, cols=cols, w_all=w_all, bias=bias):
            r0 = pl.multiple_of(rb * CONV_RB, CONV_RB)
            acc = bias
            for b in range(SUBLANES):
                taps = [k for k in range(CONV_K) if (off + k) % SUBLANES == b]
                a_lo, a_hi = (off + taps[0]) // SUBLANES, (off + taps[-1]) // SUBLANES
                rows = pl.ds(r0 + a_lo * SUBLANES, (a_hi - a_lo) * SUBLANES + CONV_RB)
                slab = buf_ref[rows, cols] if b == 0 else sh_ref[b - 1, rows, cols]
                for k in taps:
                    s0 = ((off + k) // SUBLANES - a_lo) * SUBLANES
                    wk = jnp.concatenate([w_all[k]] * (CONV_RB // SUBLANES), axis=0)
                    acc = acc + slab[s0:s0 + CONV_RB, :] * wk
            acc_ref[pl.ds(r0, CONV_RB), cols] = acc
            return carry

        lax.fori_loop(0, tm // CONV_RB, row_block, 0)
    y = acc_ref[...]
    mu = jnp.mean(y, axis=-1, keepdims=True)
    yc = y - mu
    yn = yc * lax.rsqrt(jnp.mean(yc * yc, axis=-1, keepdims=True) + EPS) * g_ref[...] + beta_ref[...]
    o_ref[...] = (yn * jax.nn.sigmoid(yn)).astype(o_ref.dtype)


def conv_ln_silu(hglu, conv_w, conv_b3, ln_g3, ln_b3, layer, tm=256):
    S, D = hglu.shape
    row = _layer_row(D, layer)
    return pl.pallas_call(
        functools.partial(_conv_kernel, tm=tm),
        grid=(S // tm,),
        in_specs=[pl.BlockSpec((tm, D), lambda i: (i, 0)),
                  pl.BlockSpec((CONV_HALO, D), lambda i: (jnp.maximum(i * (tm // CONV_HALO) - 1, 0), 0)),
                  pl.BlockSpec((None, CONV_K, D), lambda i: (layer, 0, 0)),
                  row, row, row],
        out_specs=pl.BlockSpec((tm, D), lambda i: (i, 0)),
        out_shape=jax.ShapeDtypeStruct((S, D), BF16),
        scratch_shapes=[pltpu.VMEM((CONV_HALO + tm, D), F32),
                        pltpu.VMEM((SUBLANES - 1, CONV_HALO + tm - SUBLANES, D), F32),
                        pltpu.VMEM((tm, D), F32),
                        pltpu.VMEM((CONV_K, SUBLANES, D), F32)],
        name="conv_ln_silu",
        compiler_params=_cparams("arbitrary"),
    )(hglu, hglu, conv_w, conv_b3, ln_g3, ln_b3)


def _memkv_kernel(mem_ref, g_ref, w_ref, k_ref, v_ref):
    mn = _rms(mem_ref[...], g_ref[...]).astype(BF16)
    kv = jnp.dot(mn, w_ref[...], preferred_element_type=F32)
    n = X_HEADS * X_HEAD_DIM
    k_ref[...] = kv[:, :n].astype(k_ref.dtype)
    v_ref[...] = kv[:, n:].astype(v_ref.dtype)


def mem_kv(mem, g3, w, layer):
    n = X_HEADS * X_HEAD_DIM
    D = mem.shape[1]
    full = lambda shape: pl.BlockSpec(shape, lambda i: (0, 0))
    return pl.pallas_call(
        _memkv_kernel,
        grid=(1,),
        in_specs=[full(mem.shape), _layer_row(D, layer),
                  pl.BlockSpec((None, D, 2 * n), lambda i: (layer, 0, 0))],
        out_specs=[full((N_MEM, n)), full((N_MEM, n))],
        out_shape=[jax.ShapeDtypeStruct((N_MEM, n), BF16)] * 2,
        name="mem_kv",
        compiler_params=_cparams("arbitrary"),
    )(mem, g3, w)


def _xattn_kernel(h_ref, wq_ref, k_ref, v_ref, wo_ref, x_ref, g_ref, xo_ref, ho_ref):
    q = jnp.dot(h_ref[...], wq_ref[...], preferred_element_type=F32) * (X_HEAD_DIM ** -0.5)
    q = q.astype(BF16)
    outs = []
    for h in range(X_HEADS):
        sl = slice(h * X_HEAD_DIM, (h + 1) * X_HEAD_DIM)
        s = lax.dot_general(q[:, sl], k_ref[:, sl], (((1,), (1,)), ((), ())), preferred_element_type=F32)
        p = jnp.exp(s - jnp.max(s, axis=-1, keepdims=True))
        l = jnp.sum(p, axis=-1, keepdims=True)
        o = jnp.dot(p.astype(BF16), v_ref[:, sl], preferred_element_type=F32) / l
        outs.append(o.astype(BF16))
    o = jnp.concatenate(outs, axis=-1)
    x = x_ref[...] + jnp.dot(o, wo_ref[...], preferred_element_type=F32)
    xo_ref[...] = x
    ho_ref[...] = _rms(x, g_ref[...]).astype(ho_ref.dtype)


def cross_attention(h, wq, mk, mv, wo, x, g3, layer, tm=512):
    M, D = h.shape
    n = X_HEADS * X_HEAD_DIM
    const = lambda shape: pl.BlockSpec(shape, lambda i: (0, 0))
    rows = lambda w: pl.BlockSpec((tm, w), lambda i: (i, 0))
    return pl.pallas_call(
        _xattn_kernel,
        grid=(M // tm,),
        in_specs=[rows(D),
                  pl.BlockSpec((None, D, n), lambda i: (layer, 0, 0)),
                  const((N_MEM, n)), const((N_MEM, n)),
                  pl.BlockSpec((None, n, D), lambda i: (layer, 0, 0)),
                  rows(D), _layer_row(D, layer)],
        out_specs=[rows(D), rows(D)],
        out_shape=[jax.ShapeDtypeStruct((M, D), F32), jax.ShapeDtypeStruct((M, D), BF16)],
        name="cross_attention",
        compiler_params=_cparams("arbitrary"),
    )(h, wq, mk, mv, wo, x, g3)


def _mlp_kernel(h_ref, w1_ref, w2_ref, x_ref, g_ref, xo_ref, ho_ref):
    f = pl.program_id(1)

    @pl.when(f == 0)
    def _():
        xo_ref[...] = x_ref[...]

    a = jnp.dot(h_ref[...], w1_ref[...], preferred_element_type=F32)
    a = jnp.square(jnp.maximum(a, 0.0)).astype(BF16)
    xo_ref[...] += jnp.dot(a, w2_ref[...], preferred_element_type=F32)

    @pl.when(f == pl.num_programs(1) - 1)
    def _():
        ho_ref[...] = _rms(xo_ref[...], g_ref[...]).astype(ho_ref.dtype)


def mlp(h, w1, w2, layer, x, g3, g_layer, out_dtype, tm=512, tf=1024):
    M, D = h.shape
    F = w1.shape[2]
    rows = pl.BlockSpec((tm, D), lambda i, f: (i, 0))
    return pl.pallas_call(
        _mlp_kernel,
        grid=(M // tm, F // tf),
        in_specs=[rows,
                  pl.BlockSpec((None, D, tf), lambda i, f: (layer, 0, f)),
                  pl.BlockSpec((None, tf, D), lambda i, f: (layer, f, 0)),
                  rows,
                  _layer_row(D, g_layer)],
        out_specs=[rows, rows],
        out_shape=[jax.ShapeDtypeStruct((M, D), F32), jax.ShapeDtypeStruct((M, D), out_dtype)],
        name="mlp",
        compiler_params=_cparams("arbitrary", "arbitrary"),
    )(h, w1, w2, x, g3)


def _permute_q_weight(w_uq):
    L, K, _ = w_uq.shape
    half = QK_ROPE // 2
    w = w_uq.reshape(L, K, MLA_HEADS, QK_NOPE + QK_ROPE)
    pe = w[..., QK_NOPE:]
    w = jnp.concatenate([w, pe[..., half:], pe[..., :half]], axis=-1)
    return w.reshape(L, K, MLA_HEADS * QK_PAD)


def kernel(x, mem, positions, norm_mix_g, w_in, b_gate, conv_w, conv_b, conv_ln_g, conv_ln_b, w_conv_out, q_norm_g, w_uq, kv_norm_g, w_ukv, w_mla_out, w_out, norm_mem_g, mem_norm_g, w_xq, w_xkv, w_xo, norm_ffn_g, w_ff1, w_ff2, final_norm_g):
    B, S, D = x.shape
    assert (B, S, D) == (1, SEQ, D_MODEL)
    x2d = x.reshape(S, D)
    mem2d = mem.reshape(N_MEM, D)
    vec3 = lambda v: v.reshape(v.shape[0], 1, v.shape[1])

    inv_freq = 1.0 / (ROPE_THETA ** (jnp.arange(0, QK_ROPE, 2, dtype=F32) / QK_ROPE))
    cos, sin = rope_tables(positions.reshape(S, 1), jnp.tile(inv_freq, 4).reshape(1, LANES))

    w_in_lo = w_in.astype(BF16)
    w_gate = w_in_lo[:, :, COL_GATE:]
    w_q = _permute_q_weight(w_uq).astype(BF16)
    w_kv, w_co, w_mo, w_o = (w.astype(BF16) for w in (w_ukv, w_conv_out, w_mla_out, w_out))
    w_q_x, w_kv_x, w_o_x = (w.astype(BF16) for w in (w_xq, w_xkv, w_xo))
    w_1, w_2 = w_ff1.astype(BF16), w_ff2.astype(BF16)

    g_mix, g_mem, g_memn, g_ffn = vec3(norm_mix_g), vec3(norm_mem_g), vec3(mem_norm_g), vec3(norm_ffn_g)
    g_q, g_kv, b_g = vec3(q_norm_g), vec3(kv_norm_g), vec3(b_gate)
    c_b, ln_g, ln_b = vec3(conv_b), vec3(conv_ln_g), vec3(conv_ln_b)
    g_final = final_norm_g.reshape(1, 1, D)

    h = rms_norm_rows(x2d, g_mix, 0, BF16)
    xcur = x2d
    for l in range(DEPTH):
        (hglu,) = matmul("glu_proj", [h], [(0, w_in_lo, l, COL_GLU_A), (0, w_in_lo, l, COL_GLU_G)], [],
                         [F32], _epi_glu, D_CONV, 1024, 512)
        cq, ckv, kpe = latent_proj(h, w_in_lo, l, g_q, g_kv, cos, sin)

        hc = conv_ln_silu(hglu, conv_w, c_b, ln_g, ln_b, l)
        (gy,) = matmul("conv_out_gated", [hc, h], [(0, w_co, l, 0), (1, w_gate, l, 0)],
                       [("row", b_g, l, 0)], [BF16], _epi_gated, D, 1024, 512)

        q = q_proj(cq, w_q, l, cos, sin)
        k, v = kv_proj(ckv, w_kv, l, kpe)
        o = mla_attention(q, k, v)
        (merged,) = matmul("mla_out_gated_merge", [o, h], [(0, w_mo, l, 0), (1, w_gate, l, D)],
                           [("row", b_g, l, D), ("tile", gy, 0)], [BF16], _epi_gated_merge, D, 1024, 512)

        xcur, hm = matmul("mix_out", [merged], [(0, w_o, l, 0)], [("tile", xcur, 0), ("row", g_mem, l, 0)],
                          [F32, BF16], _epi_resid_norm, D, 512, D)

        mk, mv = mem_kv(mem2d, g_memn, w_kv_x, l)
        xcur, hf = cross_attention(hm, w_q_x, mk, mv, w_o_x, xcur, g_ffn, l)

        last = l == DEPTH - 1
        xcur, h = mlp(hf, w_1, w_2, l, xcur, g_final if last else g_mix, 0 if last else l + 1,
                      F32 if last else BF16)
    return h.reshape(B, S, D)
```

```python
import functools
import math

import jax
import jax.numpy as jnp
from jax import lax
from jax.experimental import pallas as pl
from jax.experimental.pallas import tpu as pltpu

D_MODEL = 2048
SEQ = 8192
DEPTH = 2
CHUNK = 64
N_MEM = 256
EPS = 1e-6
D_CONV = D_MODEL
CONV_K = 31
MLA_HEADS = 16
Q_LORA = 512
KV_LORA = 512
QK_NOPE = 128
QK_ROPE = 64
V_DIM = 128
ROPE_THETA = 10000.0
X_HEADS = 4
X_HEAD_DIM = 128
D_FF = 4 * D_MODEL

LANES = 128
SUBLANES = 8
QK_PAD = 256
CONV_HALO = 32
NEG_BIG = -1e30
VMEM_LIMIT = 56 * 1024 * 1024

COL_GLU_A = 0
COL_GLU_G = D_CONV
COL_LATENT = 2 * D_CONV
COL_KPE = COL_LATENT + Q_LORA + KV_LORA
COL_GATE = COL_KPE + QK_ROPE

F32 = jnp.float32
BF16 = jnp.bfloat16


def _cparams(*sem):
    return pltpu.CompilerParams(dimension_semantics=sem, vmem_limit_bytes=VMEM_LIMIT)


def _rms(x, g):
    return x * lax.rsqrt(jnp.mean(x * x, axis=-1, keepdims=True) + EPS) * g


def _layer_row(n, layer, col0=0):
    return pl.BlockSpec((None, 1, n), lambda *_: (layer, 0, col0 // n))


def _rope_table_kernel(pos_ref, freq_ref, cos_ref, sin_ref):
    ang = pos_ref[...].astype(F32) * freq_ref[...]
    lane = lax.broadcasted_iota(jnp.int32, ang.shape, 1)
    s = jnp.sin(ang)
    cos_ref[...] = jnp.where(lane < QK_ROPE, jnp.cos(ang), 0.0)
    sin_ref[...] = jnp.where(lane < QK_ROPE // 2, -s, jnp.where(lane < QK_ROPE, s, 0.0))


def rope_tables(pos_col, freq_row):
    S = pos_col.shape[0]
    tm = 1024
    tab = pl.BlockSpec((tm, LANES), lambda i: (i, 0))
    return pl.pallas_call(
        _rope_table_kernel,
        grid=(S // tm,),
        in_specs=[pl.BlockSpec((tm, 1), lambda i: (i, 0)),
                  pl.BlockSpec((1, LANES), lambda i: (0, 0))],
        out_specs=[tab, tab],
        out_shape=[jax.ShapeDtypeStruct((S, LANES), F32)] * 2,
        name="rope_table",
        compiler_params=_cparams("arbitrary"),
    )(pos_col, freq_row)


def _norm_kernel(x_ref, g_ref, o_ref):
    o_ref[...] = _rms(x_ref[...], g_ref[...]).astype(o_ref.dtype)


def rms_norm_rows(x, g3, layer, out_dtype, tm=512):
    M, D = x.shape
    return pl.pallas_call(
        _norm_kernel,
        grid=(M // tm,),
        in_specs=[pl.BlockSpec((tm, D), lambda i: (i, 0)), _layer_row(D, layer)],
        out_specs=pl.BlockSpec((tm, D), lambda i: (i, 0)),
        out_shape=jax.ShapeDtypeStruct((M, D), out_dtype),
        name="rms_norm",
        compiler_params=_cparams("arbitrary"),
    )(x, g3)


def _mm_kernel(*refs, a_index, n_a, n_extra, epilogue):
    n_b = len(a_index)
    acts = [r[...] for r in refs[:n_a]]
    accs = [jnp.dot(acts[ai], b[...], preferred_element_type=F32)
            for ai, b in zip(a_index, refs[n_a:n_a + n_b])]
    extras = [e[...] for e in refs[n_a + n_b:n_a + n_b + n_extra]]
    outs = epilogue(accs, extras)
    for o_ref, o in zip(refs[n_a + n_b + n_extra:], outs):
        o_ref[...] = o.astype(o_ref.dtype)


def matmul(name, acts, weights, extras, out_dtypes, epilogue, n_out, tm, tn):
    M = acts[0].shape[0]
    in_specs = [pl.BlockSpec((tm, a.shape[1]), lambda i, j: (i, 0)) for a in acts]
    operands = list(acts)
    for _, w, layer, col0 in weights:
        in_specs.append(pl.BlockSpec((None, w.shape[1], tn),
                                     lambda i, j, layer=layer, off=col0 // tn: (layer, 0, j + off)))
        operands.append(w)
    for e in extras:
        if e[0] == "row":
            _, arr, layer, col0 = e
            in_specs.append(pl.BlockSpec((None, 1, tn),
                                         lambda i, j, layer=layer, off=col0 // tn: (layer, 0, j + off)))
        else:
            _, arr, col0 = e
            in_specs.append(pl.BlockSpec((tm, tn), lambda i, j, off=col0 // tn: (i, j + off)))
        operands.append(arr)
    return pl.pallas_call(
        functools.partial(_mm_kernel, a_index=tuple(w[0] for w in weights), n_a=len(acts),
                          n_extra=len(extras), epilogue=epilogue),
        grid=(M // tm, n_out // tn),
        in_specs=in_specs,
        out_specs=[pl.BlockSpec((tm, tn), lambda i, j: (i, j)) for _ in out_dtypes],
        out_shape=[jax.ShapeDtypeStruct((M, n_out), dt) for dt in out_dtypes],
        name=name,
        compiler_params=_cparams("arbitrary", "arbitrary"),
    )(*operands)


def _epi_glu(accs, extras):
    return [accs[0] * jax.nn.sigmoid(accs[1])]


def _epi_gated(accs, extras):
    return [accs[0] * jax.nn.sigmoid(accs[1] + extras[0])]


def _epi_gated_merge(accs, extras):
    return [extras[1] + accs[0] * jax.nn.sigmoid(accs[1] + extras[0])]


def _mla_proj_kernel(h_ref, wlat_ref, wpe_ref, wq_ref, wkv_ref, qg_ref, kvg_ref, cos_ref, sin_ref,
                     q_ref, k_ref, v_ref):
    h = h_ref[...]
    cos, sin = cos_ref[...], sin_ref[...]
    z = jnp.dot(h, wlat_ref[...], preferred_element_type=F32)
    cq = _rms(z[:, :Q_LORA], qg_ref[...]).astype(BF16)
    ckv = _rms(z[:, Q_LORA:], kvg_ref[...]).astype(BF16)
    pe = jnp.dot(h, wpe_ref[...], preferred_element_type=F32)
    half = QK_ROPE // 2
    lane = lax.broadcasted_iota(jnp.int32, pe.shape, 1)
    swapped = jnp.where(lane < half, pltpu.roll(pe, LANES - half, 1), pltpu.roll(pe, half, 1))
    kpe = jnp.where(lane < QK_ROPE, pe * cos + swapped * sin, 0.0).astype(k_ref.dtype)

    scale = (QK_NOPE + QK_ROPE) ** -0.5 * math.log2(math.e)
    kv_width = QK_NOPE + V_DIM
    for hd in range(MLA_HEADS):
        blk = jnp.dot(cq, wq_ref[:, hd * QK_PAD:(hd + 1) * QK_PAD], preferred_element_type=F32)
        q_ref[hd, :, 0:QK_NOPE] = (blk[:, :QK_NOPE] * scale).astype(q_ref.dtype)
        qpe = blk[:, QK_NOPE:]
        roped = qpe * cos + pltpu.roll(qpe, QK_ROPE, 1) * sin
        q_ref[hd, :, QK_NOPE:QK_PAD] = (roped * scale).astype(q_ref.dtype)

        blk = jnp.dot(ckv, wkv_ref[:, hd * kv_width:(hd + 1) * kv_width], preferred_element_type=F32)
        k_ref[hd, :, 0:QK_NOPE] = blk[:, :QK_NOPE].astype(k_ref.dtype)
        k_ref[hd, :, QK_NOPE:QK_PAD] = kpe
        v_ref[:, hd * V_DIM:(hd + 1) * V_DIM] = blk[:, QK_NOPE:].astype(v_ref.dtype)


def mla_proj(h, w_in_lo, w_q, w_kv, layer, qg3, kvg3, cos, sin, tm=512):
    M, K = h.shape
    n_lat = Q_LORA + KV_LORA
    rows = lambda n: pl.BlockSpec((tm, n), lambda i: (i, 0))
    once = pl.Buffered(1)
    heads = pl.BlockSpec((MLA_HEADS, tm, QK_PAD), lambda i: (0, i, 0))
    return pl.pallas_call(
        _mla_proj_kernel,
        grid=(M // tm,),
        in_specs=[rows(K),
                  pl.BlockSpec((None, K, n_lat), lambda i: (layer, 0, COL_LATENT // n_lat), pipeline_mode=once),
                  pl.BlockSpec((None, K, LANES), lambda i: (layer, 0, COL_KPE // LANES), pipeline_mode=once),
                  pl.BlockSpec((None, Q_LORA, MLA_HEADS * QK_PAD), lambda i: (layer, 0, 0), pipeline_mode=once),
                  pl.BlockSpec((None, KV_LORA, MLA_HEADS * (QK_NOPE + V_DIM)), lambda i: (layer, 0, 0),
                               pipeline_mode=once),
                  _layer_row(Q_LORA, layer), _layer_row(KV_LORA, layer),
                  rows(LANES), rows(LANES)],
        out_specs=[heads, heads, rows(MLA_HEADS * V_DIM)],
        out_shape=[jax.ShapeDtypeStruct((MLA_HEADS, M, QK_PAD), BF16),
                   jax.ShapeDtypeStruct((MLA_HEADS, M, QK_PAD), BF16),
                   jax.ShapeDtypeStruct((M, MLA_HEADS * V_DIM), BF16)],
        name="mla_proj",
        compiler_params=_cparams("arbitrary"),
    )(h, w_in_lo, w_in_lo, w_q, w_kv, qg3, kvg3, cos, sin)


ATTN_RB = CHUNK
ATTN_UNROLL = 4


def _attn_kernel(q_ref, k_ref, v_ref, o_ref, vone_ref, *set_refs, tq):
    vone_ref[:, 0:V_DIM] = v_ref[...]
    vone_ref[:, V_DIM:] = jnp.ones((vone_ref.shape[0], V_DIM), vone_ref.dtype)
    n_set = len(set_refs) // 2
    sets = [_AttnBlock(q_ref, k_ref, vone_ref, o_ref, set_refs[i * n_set:(i + 1) * n_set], tq, i)
            for i in range(2)]
    n_q = q_ref.shape[1] // tq
    assert n_q % 2 == 0

    sets[0].first_block()
    sets[1].fill(jnp.int32(1))

    def query_block_pair(qq, carry):
        qa, qb = 2 * qq, 2 * qq + 1
        sets[1].drain(qa - 1)
        sets[0].fill(qa)
        sets[0].steady(qa)
        sets[0].drain(qa)
        sets[1].fill(qb)
        sets[1].steady(qb)
        return carry

    lax.fori_loop(1, n_q // 2, query_block_pair, 0)
    sets[1].drain(jnp.int32(n_q - 1))


class _AttnBlock:
    def __init__(self, q_ref, k_ref, v_ref, o_ref, refs, tq, par):
        s0, s1, p0, p1, a0, a1, self.m_ref, self.acc_ref = refs
        self.s_refs, self.p_refs, self.a_refs = (s0, s1), (p0, p1), (a0, a1)
        self.q_ref, self.k_ref, self.v_ref, self.o_ref = q_ref, k_ref, v_ref, o_ref
        self.tq, self.par = tq, par

    def _q_rows(self, qi):
        return pl.ds(pl.multiple_of(qi * self.tq, self.tq), self.tq)

    def _init(self):
        self.m_ref[...] = jnp.full(self.m_ref.shape, NEG_BIG, F32)
        self.acc_ref[...] = jnp.zeros(self.acc_ref.shape, F32)

    def scores(self, qi, j, par):
        start = pl.multiple_of(j * self.tq, self.tq)
        kb = self.k_ref[0, pl.ds(start, self.tq), :]
        self.s_refs[par][...] = lax.dot_general(self.q_ref[0, self._q_rows(qi), :], kb,
                                                (((1,), (1,)), ((), ())), preferred_element_type=F32)

    def softmax(self, par, masked):
        s_ref, p_ref, a_ref, m_ref = self.s_refs[par], self.p_refs[par], self.a_refs[par], self.m_ref
        for r in range(self.tq // ATTN_RB):
            rows = slice(r * ATTN_RB, (r + 1) * ATTN_RB)
            cols = []
            for c in range(self.tq // LANES):
                first_chunk = (c * LANES) // CHUNK
                if masked and first_chunk > r:
                    cols.append(None)
                    continue
                sc = s_ref[rows, c * LANES:(c + 1) * LANES]
                if masked and first_chunk == r:
                    lane = lax.broadcasted_iota(jnp.int32, sc.shape, 1)
                    sc = jnp.where(lane < CHUNK, sc, NEG_BIG)
                cols.append(sc)
            mx = functools.reduce(jnp.maximum, [sc for sc in cols if sc is not None])
            m_prev = m_ref[rows, :]
            m_new = jnp.maximum(m_prev, jnp.max(mx, axis=-1, keepdims=True))
            m_ref[rows, :] = m_new
            a_ref[rows, :] = jnp.exp2(m_prev - m_new)
            for c, sc in enumerate(cols):
                pc = jnp.zeros((ATTN_RB, LANES), BF16) if sc is None else jnp.exp2(sc - m_new).astype(BF16)
                p_ref[rows, c * LANES:(c + 1) * LANES] = pc

    def values(self, j, par):
        start = pl.multiple_of(j * self.tq, self.tq)
        vb = self.v_ref[pl.ds(start, self.tq), :]
        alpha = self.a_refs[par][...]
        self.acc_ref[...] = (jnp.concatenate([alpha, alpha], axis=1) * self.acc_ref[...]
                             + jnp.dot(self.p_refs[par][...], vb, preferred_element_type=F32))

    def _write(self, qi):
        acc = self.acc_ref
        self.o_ref[self._q_rows(qi), :] = (acc[:, 0:V_DIM] / acc[:, V_DIM:]).astype(self.o_ref.dtype)

    def first_block(self):
        qi = jnp.int32(0)
        self._init()
        self.scores(qi, qi, 0)
        self.softmax(0, True)
        self.values(qi, 0)
        self._write(qi)

    def fill(self, qi):
        self._init()
        self.scores(qi, 0, 0)
        self.scores(qi, 1, 1)
        self.softmax(0, False)

    def _tick(self, qi, t, par):
        self.scores(qi, t, par)
        self.softmax(1 - par, False)
        self.values(t - 2, par)

    def steady(self, qi):
        n_steady = jnp.maximum(qi - 1, 0)

        def unrolled(i, carry):
            for u in range(ATTN_UNROLL):
                self._tick(qi, 2 + ATTN_UNROLL * i + u, u % 2)
            return carry

        lax.fori_loop(0, n_steady // ATTN_UNROLL, unrolled, 0)
        t_pair = 2 + (n_steady // ATTN_UNROLL) * ATTN_UNROLL
        n_left = n_steady % ATTN_UNROLL

        def pair(i, carry):
            self._tick(qi, t_pair + 2 * i, 0)
            self._tick(qi, t_pair + 2 * i + 1, 1)
            return carry

        lax.fori_loop(0, n_left // 2, pair, 0)

        @pl.when(n_left % 2 == 1)
        def _():
            self._tick(qi, qi, 0)

    def drain(self, qi):
        self.softmax(self.par, True)
        self.values(qi - 1, 1 - self.par)
        self.values(qi, self.par)
        self._write(qi)


def mla_attention(q, k, v, tq=512):
    H, S, _ = q.shape
    buffer_set = ([pltpu.VMEM((tq, tq), F32)] * 2
                  + [pltpu.VMEM((tq, tq), BF16)] * 2
                  + [pltpu.VMEM((tq, LANES), F32)] * 2
                  + [pltpu.VMEM((tq, LANES), F32),
                     pltpu.VMEM((tq, 2 * V_DIM), F32)])
    return pl.pallas_call(
        functools.partial(_attn_kernel, tq=tq),
        grid=(H,),
        in_specs=[pl.BlockSpec((1, S, QK_PAD), lambda h: (h, 0, 0)),
                  pl.BlockSpec((1, S, QK_PAD), lambda h: (h, 0, 0)),
                  pl.BlockSpec((S, V_DIM), lambda h: (0, h))],
        out_specs=pl.BlockSpec((S, V_DIM), lambda h: (0, h)),
        out_shape=jax.ShapeDtypeStruct((S, H * V_DIM), BF16),
        scratch_shapes=[pltpu.VMEM((S, 2 * V_DIM), BF16)] + buffer_set + buffer_set,
        name="mla_attention",
        compiler_params=_cparams("arbitrary"),
    )(q, k, v)


CONV_RB = 64
CONV_CB = LANES


def _conv_kernel(cur_ref, halo_ref, w_ref, b_ref, g_ref, beta_ref, o_ref, buf_ref, sh_ref, acc_ref, wb_ref, *, tm):
    i = pl.program_id(0)

    @pl.when(i == 0)
    def _():
        for k in range(CONV_K):
            wb_ref[k] = jnp.broadcast_to(w_ref[k:k + 1, :], (SUBLANES, D_CONV))

    buf_ref[0:CONV_HALO, :] = jnp.where(i > 0, halo_ref[...], 0.0)
    buf_ref[CONV_HALO:CONV_HALO + tm, :] = cur_ref[...]
    n_buf = tm + CONV_HALO
    n_sh = n_buf - SUBLANES
    for c0 in range(0, D_CONV, LANES):
        x = buf_ref[:, c0:c0 + LANES]
        for b in range(1, SUBLANES):
            sh_ref[b - 1, :, c0:c0 + LANES] = pltpu.roll(x, n_buf - b, 0)[0:n_sh, :]
    off = CONV_HALO - (CONV_K - 1)
    for c0 in range(0, D_CONV, CONV_CB):
        cols = slice(c0, c0 + CONV_CB)
        w_all = wb_ref[:, :, cols]
        bias = jnp.broadcast_to(b_ref[:, cols], (CONV_RB, CONV_CB))

        def row_block(rb, carry, cols=cols, w_all=w_all, bias=bias):
            r0 = pl.multiple_of(rb * CONV_RB, CONV_RB)
            acc = bias
            for b in range(SUBLANES):
                taps = [k for k in range(CONV_K) if (off + k) % SUBLANES == b]
                a_lo, a_hi = (off + taps[0]) // SUBLANES, (off + taps[-1]) // SUBLANES
                rows = pl.ds(r0 + a_lo * SUBLANES, (a_hi - a_lo) * SUBLANES + CONV_RB)
                slab = buf_ref[rows, cols] if b == 0 else sh_ref[b - 1, rows, cols]
                for k in taps:
                    s0 = ((off + k) // SUBLANES - a_lo) * SUBLANES
                    wk = jnp.concatenate([w_all[k]] * (CONV_RB // SUBLANES), axis=0)
                    acc = acc + slab[s0:s0 + CONV_RB, :] * wk
            acc_ref[pl.ds(r0, CONV_RB), cols] = acc
            return carry

        lax.fori_loop(0, tm // CONV_RB, row_block, 0)
    y = acc_ref[...]
    mu = jnp.mean(y, axis=-1, keepdims=True)
    yc = y - mu
    yn = yc * lax.rsqrt(jnp.mean(yc * yc, axis=-1, keepdims=True) + EPS) * g_ref[...] + beta_ref[...]
    o_ref[...] = (yn * jax.nn.sigmoid(yn)).astype(o_ref.dtype)


def conv_ln_silu(hglu, conv_w, conv_b3, ln_g3, ln_b3, layer, tm=256):
    S, D = hglu.shape
    row = _layer_row(D, layer)
    return pl.pallas_call(
        functools.partial(_conv_kernel, tm=tm),
        grid=(S // tm,),
        in_specs=[pl.BlockSpec((tm, D), lambda i: (i, 0)),
                  pl.BlockSpec((CONV_HALO, D), lambda i: (jnp.maximum(i * (tm // CONV_HALO) - 1, 0), 0)),
                  pl.BlockSpec((None, CONV_K, D), lambda i: (layer, 0, 0)),
                  row, row, row],
        out_specs=pl.BlockSpec((tm, D), lambda i: (i, 0)),
        out_shape=jax.ShapeDtypeStruct((S, D), BF16),
        scratch_shapes=[pltpu.VMEM((CONV_HALO + tm, D), F32),
                        pltpu.VMEM((SUBLANES - 1, CONV_HALO + tm - SUBLANES, D), F32),
                        pltpu.VMEM((tm, D), F32),
                        pltpu.VMEM((CONV_K, SUBLANES, D), F32)],
        name="conv_ln_silu",
        compiler_params=_cparams("arbitrary"),
    )(hglu, hglu, conv_w, conv_b3, ln_g3, ln_b3)


def _memkv_kernel(mem_ref, g_ref, w_ref, k_ref, v_ref):
    mn = _rms(mem_ref[...], g_ref[...]).astype(BF16)
    kv = jnp.dot(mn, w_ref[...], preferred_element_type=F32)
    n = X_HEADS * X_HEAD_DIM
    k_ref[...] = kv[:, :n].astype(k_ref.dtype)
    v_ref[...] = kv[:, n:].astype(v_ref.dtype)


def mem_kv(mem, g3, w, layer):
    n = X_HEADS * X_HEAD_DIM
    D = mem.shape[1]
    full = lambda shape: pl.BlockSpec(shape, lambda i: (0, 0))
    return pl.pallas_call(
        _memkv_kernel,
        grid=(1,),
        in_specs=[full(mem.shape), _layer_row(D, layer),
                  pl.BlockSpec((None, D, 2 * n), lambda i: (layer, 0, 0))],
        out_specs=[full((N_MEM, n)), full((N_MEM, n))],
        out_shape=[jax.ShapeDtypeStruct((N_MEM, n), BF16)] * 2,
        name="mem_kv",
        compiler_params=_cparams("arbitrary"),
    )(mem, g3, w)


MIX_SUB = 2


def _mix_cross_kernel(m_ref, x_ref, wo_ref, gmem_ref, wq_ref, k_ref, v_ref, wxo_ref, gffn_ref, xo_ref, ho_ref):
    sub = m_ref.shape[0] // MIX_SUB
    for r in range(MIX_SUB):
        rows = slice(r * sub, (r + 1) * sub)
        x1 = x_ref[rows, :] + jnp.dot(m_ref[rows, :], wo_ref[...], preferred_element_type=F32)
        hm = _rms(x1, gmem_ref[...]).astype(BF16)
        q = jnp.dot(hm, wq_ref[...], preferred_element_type=F32) * (X_HEAD_DIM ** -0.5)
        q = q.astype(BF16)
        outs = []
        for h in range(X_HEADS):
            sl = slice(h * X_HEAD_DIM, (h + 1) * X_HEAD_DIM)
            s = lax.dot_general(q[:, sl], k_ref[:, sl], (((1,), (1,)), ((), ())), preferred_element_type=F32)
            p = jnp.exp(s - jnp.max(s, axis=-1, keepdims=True))
            l = jnp.sum(p, axis=-1, keepdims=True)
            o = jnp.dot(p.astype(BF16), v_ref[:, sl], preferred_element_type=F32) / l
            outs.append(o.astype(BF16))
        o = jnp.concatenate(outs, axis=-1)
        x2 = x1 + jnp.dot(o, wxo_ref[...], preferred_element_type=F32)
        xo_ref[rows, :] = x2
        ho_ref[rows, :] = _rms(x2, gffn_ref[...]).astype(ho_ref.dtype)


def mix_cross(merged, x, w_out, g_mem3, w_xq, mk, mv, w_xo, g_ffn3, layer, tm=512):
    M, D = merged.shape
    n = X_HEADS * X_HEAD_DIM
    const = lambda shape: pl.BlockSpec(shape, lambda i: (0, 0))
    rows = pl.BlockSpec((tm, D), lambda i: (i, 0))
    once = pl.Buffered(1)
    weight = lambda k, c: pl.BlockSpec((None, k, c), lambda i: (layer, 0, 0), pipeline_mode=once)
    return pl.pallas_call(
        _mix_cross_kernel,
        grid=(M // tm,),
        in_specs=[rows, rows, weight(D, D), _layer_row(D, layer), weight(D, n),
                  const((N_MEM, n)), const((N_MEM, n)), weight(n, D), _layer_row(D, layer)],
        out_specs=[rows, rows],
        out_shape=[jax.ShapeDtypeStruct((M, D), F32), jax.ShapeDtypeStruct((M, D), BF16)],
        name="mix_cross",
        compiler_params=_cparams("arbitrary"),
    )(merged, x, w_out, g_mem3, w_xq, mk, mv, w_xo, g_ffn3)


def _mlp_kernel(h_ref, w1_ref, w2_ref, x_ref, g_ref, xo_ref, ho_ref):
    f = pl.program_id(1)

    @pl.when(f == 0)
    def _():
        xo_ref[...] = x_ref[...]

    a = jnp.dot(h_ref[...], w1_ref[...], preferred_element_type=F32)
    a = jnp.square(jnp.maximum(a, 0.0)).astype(BF16)
    xo_ref[...] += jnp.dot(a, w2_ref[...], preferred_element_type=F32)

    @pl.when(f == pl.num_programs(1) - 1)
    def _():
        ho_ref[...] = _rms(xo_ref[...], g_ref[...]).astype(ho_ref.dtype)


def mlp(h, w1, w2, layer, x, g3, g_layer, out_dtype, tm=512, tf=1024):
    M, D = h.shape
    F = w1.shape[2]
    rows = pl.BlockSpec((tm, D), lambda i, f: (i, 0))
    return pl.pallas_call(
        _mlp_kernel,
        grid=(M // tm, F // tf),
        in_specs=[rows,
                  pl.BlockSpec((None, D, tf), lambda i, f: (layer, 0, f)),
                  pl.BlockSpec((None, tf, D), lambda i, f: (layer, f, 0)),
                  rows,
                  _layer_row(D, g_layer)],
        out_specs=[rows, rows],
        out_shape=[jax.ShapeDtypeStruct((M, D), F32), jax.ShapeDtypeStruct((M, D), out_dtype)],
        name="mlp",
        compiler_params=_cparams("arbitrary", "arbitrary"),
    )(h, w1, w2, x, g3)


def _permute_q_weight(w_uq):
    L, K, _ = w_uq.shape
    half = QK_ROPE // 2
    w = w_uq.reshape(L, K, MLA_HEADS, QK_NOPE + QK_ROPE)
    pe = w[..., QK_NOPE:]
    w = jnp.concatenate([w, pe[..., half:], pe[..., :half]], axis=-1)
    return w.reshape(L, K, MLA_HEADS * QK_PAD)


def kernel(x, mem, positions, norm_mix_g, w_in, b_gate, conv_w, conv_b, conv_ln_g, conv_ln_b, w_conv_out, q_norm_g, w_uq, kv_norm_g, w_ukv, w_mla_out, w_out, norm_mem_g, mem_norm_g, w_xq, w_xkv, w_xo, norm_ffn_g, w_ff1, w_ff2, final_norm_g):
    B, S, D = x.shape
    assert (B, S, D) == (1, SEQ, D_MODEL)
    x2d = x.reshape(S, D)
    mem2d = mem.reshape(N_MEM, D)
    vec3 = lambda v: v.reshape(v.shape[0], 1, v.shape[1])

    inv_freq = 1.0 / (ROPE_THETA ** (jnp.arange(0, QK_ROPE, 2, dtype=F32) / QK_ROPE))
    cos, sin = rope_tables(positions.reshape(S, 1), jnp.tile(inv_freq, 4).reshape(1, LANES))

    w_in_lo = w_in.astype(BF16)
    w_gate = w_in_lo[:, :, COL_GATE:]
    w_q = _permute_q_weight(w_uq).astype(BF16)
    w_kv, w_co, w_mo, w_o = (w.astype(BF16) for w in (w_ukv, w_conv_out, w_mla_out, w_out))
    w_q_x, w_kv_x, w_o_x = (w.astype(BF16) for w in (w_xq, w_xkv, w_xo))
    w_1, w_2 = w_ff1.astype(BF16), w_ff2.astype(BF16)

    g_mix, g_mem, g_memn, g_ffn = vec3(norm_mix_g), vec3(norm_mem_g), vec3(mem_norm_g), vec3(norm_ffn_g)
    g_q, g_kv, b_g = vec3(q_norm_g), vec3(kv_norm_g), vec3(b_gate)
    c_b, ln_g, ln_b = vec3(conv_b), vec3(conv_ln_g), vec3(conv_ln_b)
    g_final = final_norm_g.reshape(1, 1, D)

    h = rms_norm_rows(x2d, g_mix, 0, BF16)
    xcur = x2d
    for l in range(DEPTH):
        (hglu,) = matmul("glu_proj", [h], [(0, w_in_lo, l, COL_GLU_A), (0, w_in_lo, l, COL_GLU_G)], [],
                         [F32], _epi_glu, D_CONV, 1024, 512)

        hc = conv_ln_silu(hglu, conv_w, c_b, ln_g, ln_b, l)
        (gy,) = matmul("conv_out_gated", [hc, h], [(0, w_co, l, 0), (1, w_gate, l, 0)],
                       [("row", b_g, l, 0)], [BF16], _epi_gated, D, 1024, 512)

        q, k, v = mla_proj(h, w_in_lo, w_q, w_kv, l, g_q, g_kv, cos, sin)
        o = mla_attention(q, k, v)
        (merged,) = matmul("mla_out_gated_merge", [o, h], [(0, w_mo, l, 0), (1, w_gate, l, D)],
                           [("row", b_g, l, D), ("tile", gy, 0)], [BF16], _epi_gated_merge, D, 1024, 512)

        mk, mv = mem_kv(mem2d, g_memn, w_kv_x, l)
        xcur, hf = mix_cross(merged, xcur, w_o, g_mem, w_q_x, mk, mv, w_o_x, g_ffn, l)

        last = l == DEPTH - 1
        xcur, h = mlp(hf, w_1, w_2, l, xcur, g_final if last else g_mix, 0 if last else l + 1,
                      F32 if last else BF16)
    return h.reshape(B, S, D)
```

```python
import functools
import math

import jax
import jax.numpy as jnp
from jax import lax
from jax.experimental import pallas as pl
from jax.experimental.pallas import tpu as pltpu

D_MODEL = 2048
SEQ = 8192
DEPTH = 2
CHUNK = 64
N_MEM = 256
EPS = 1e-6
D_CONV = D_MODEL
CONV_K = 31
MLA_HEADS = 16
Q_LORA = 512
KV_LORA = 512
QK_NOPE = 128
QK_ROPE = 64
V_DIM = 128
ROPE_THETA = 10000.0
X_HEADS = 4
X_HEAD_DIM = 128
D_FF = 4 * D_MODEL

LANES = 128
SUBLANES = 8
QK_PAD = 256
CONV_HALO = 32
NEG_BIG = -1e30
VMEM_LIMIT = 56 * 1024 * 1024

COL_GLU_A = 0
COL_GLU_G = D_CONV
COL_LATENT = 2 * D_CONV
COL_KPE = COL_LATENT + Q_LORA + KV_LORA
COL_GATE = COL_KPE + QK_ROPE

F32 = jnp.float32
BF16 = jnp.bfloat16


def _cparams(*sem):
    return pltpu.CompilerParams(dimension_semantics=sem, vmem_limit_bytes=VMEM_LIMIT)


def _rms(x, g):
    return x * lax.rsqrt(jnp.mean(x * x, axis=-1, keepdims=True) + EPS) * g


def _layer_row(n, layer, col0=0):
    return pl.BlockSpec((None, 1, n), lambda *_: (layer, 0, col0 // n))


def _rope_table_kernel(pos_ref, freq_ref, cos_ref, sin_ref):
    ang = pos_ref[...].astype(F32) * freq_ref[...]
    lane = lax.broadcasted_iota(jnp.int32, ang.shape, 1)
    s = jnp.sin(ang)
    cos_ref[...] = jnp.where(lane < QK_ROPE, jnp.cos(ang), 0.0)
    sin_ref[...] = jnp.where(lane < QK_ROPE // 2, -s, jnp.where(lane < QK_ROPE, s, 0.0))


def rope_tables(pos_col, freq_row):
    S = pos_col.shape[0]
    tm = 1024
    tab = pl.BlockSpec((tm, LANES), lambda i: (i, 0))
    return pl.pallas_call(
        _rope_table_kernel,
        grid=(S // tm,),
        in_specs=[pl.BlockSpec((tm, 1), lambda i: (i, 0)),
                  pl.BlockSpec((1, LANES), lambda i: (0, 0))],
        out_specs=[tab, tab],
        out_shape=[jax.ShapeDtypeStruct((S, LANES), F32)] * 2,
        name="rope_table",
        compiler_params=_cparams("arbitrary"),
    )(pos_col, freq_row)


def _norm_kernel(x_ref, g_ref, o_ref):
    o_ref[...] = _rms(x_ref[...], g_ref[...]).astype(o_ref.dtype)


def rms_norm_rows(x, g3, layer, out_dtype, tm=512):
    M, D = x.shape
    return pl.pallas_call(
        _norm_kernel,
        grid=(M // tm,),
        in_specs=[pl.BlockSpec((tm, D), lambda i: (i, 0)), _layer_row(D, layer)],
        out_specs=pl.BlockSpec((tm, D), lambda i: (i, 0)),
        out_shape=jax.ShapeDtypeStruct((M, D), out_dtype),
        name="rms_norm",
        compiler_params=_cparams("arbitrary"),
    )(x, g3)


def _mm_kernel(*refs, a_index, n_a, n_extra, epilogue):
    n_b = len(a_index)
    acts = [r[...] for r in refs[:n_a]]
    accs = [jnp.dot(acts[ai], b[...], preferred_element_type=F32)
            for ai, b in zip(a_index, refs[n_a:n_a + n_b])]
    extras = [e[...] for e in refs[n_a + n_b:n_a + n_b + n_extra]]
    outs = epilogue(accs, extras)
    for o_ref, o in zip(refs[n_a + n_b + n_extra:], outs):
        o_ref[...] = o.astype(o_ref.dtype)


def matmul(name, acts, weights, extras, out_dtypes, epilogue, n_out, tm, tn):
    M = acts[0].shape[0]
    in_specs = [pl.BlockSpec((tm, a.shape[1]), lambda i, j: (i, 0)) for a in acts]
    operands = list(acts)
    for _, w, layer, col0 in weights:
        in_specs.append(pl.BlockSpec((None, w.shape[1], tn),
                                     lambda i, j, layer=layer, off=col0 // tn: (layer, 0, j + off)))
        operands.append(w)
    for e in extras:
        if e[0] == "row":
            _, arr, layer, col0 = e
            in_specs.append(pl.BlockSpec((None, 1, tn),
                                         lambda i, j, layer=layer, off=col0 // tn: (layer, 0, j + off)))
        else:
            _, arr, col0 = e
            in_specs.append(pl.BlockSpec((tm, tn), lambda i, j, off=col0 // tn: (i, j + off)))
        operands.append(arr)
    return pl.pallas_call(
        functools.partial(_mm_kernel, a_index=tuple(w[0] for w in weights), n_a=len(acts),
                          n_extra=len(extras), epilogue=epilogue),
        grid=(M // tm, n_out // tn),
        in_specs=in_specs,
        out_specs=[pl.BlockSpec((tm, tn), lambda i, j: (i, j)) for _ in out_dtypes],
        out_shape=[jax.ShapeDtypeStruct((M, n_out), dt) for dt in out_dtypes],
        name=name,
        compiler_params=_cparams("arbitrary", "arbitrary"),
    )(*operands)


def _epi_glu(accs, extras):
    return [accs[0] * jax.nn.sigmoid(accs[1])]


def _epi_gated(accs, extras):
    return [accs[0] * jax.nn.sigmoid(accs[1] + extras[0])]


def _epi_gated_merge(accs, extras):
    return [extras[1] + accs[0] * jax.nn.sigmoid(accs[1] + extras[0])]


def _mla_proj_kernel(h_ref, wlat_ref, wpe_ref, wq_ref, wkv_ref, qg_ref, kvg_ref, cos_ref, sin_ref,
                     q_ref, k_ref, v_ref):
    h = h_ref[...]
    cos, sin = cos_ref[...], sin_ref[...]
    z = jnp.dot(h, wlat_ref[...], preferred_element_type=F32)
    cq = _rms(z[:, :Q_LORA], qg_ref[...]).astype(BF16)
    ckv = _rms(z[:, Q_LORA:], kvg_ref[...]).astype(BF16)
    pe = jnp.dot(h, wpe_ref[...], preferred_element_type=F32)
    half = QK_ROPE // 2
    lane = lax.broadcasted_iota(jnp.int32, pe.shape, 1)
    swapped = jnp.where(lane < half, pltpu.roll(pe, LANES - half, 1), pltpu.roll(pe, half, 1))
    kpe = jnp.where(lane < QK_ROPE, pe * cos + swapped * sin, 0.0).astype(k_ref.dtype)

    scale = (QK_NOPE + QK_ROPE) ** -0.5 * math.log2(math.e)
    kv_width = QK_NOPE + V_DIM
    for hd in range(MLA_HEADS):
        blk = jnp.dot(cq, wq_ref[:, hd * QK_PAD:(hd + 1) * QK_PAD], preferred_element_type=F32)
        q_ref[hd, :, 0:QK_NOPE] = (blk[:, :QK_NOPE] * scale).astype(q_ref.dtype)
        qpe = blk[:, QK_NOPE:]
        roped = qpe * cos + pltpu.roll(qpe, QK_ROPE, 1) * sin
        q_ref[hd, :, QK_NOPE:QK_PAD] = (roped * scale).astype(q_ref.dtype)

        blk = jnp.dot(ckv, wkv_ref[:, hd * kv_width:(hd + 1) * kv_width], preferred_element_type=F32)
        k_ref[hd, :, 0:QK_NOPE] = blk[:, :QK_NOPE].astype(k_ref.dtype)
        k_ref[hd, :, QK_NOPE:QK_PAD] = kpe
        v_ref[:, hd * V_DIM:(hd + 1) * V_DIM] = blk[:, QK_NOPE:].astype(v_ref.dtype)


def mla_proj(h, w_in_lo, w_q, w_kv, layer, qg3, kvg3, cos, sin, tm=512):
    M, K = h.shape
    n_lat = Q_LORA + KV_LORA
    rows = lambda n: pl.BlockSpec((tm, n), lambda i: (i, 0))
    once = pl.Buffered(1)
    heads = pl.BlockSpec((MLA_HEADS, tm, QK_PAD), lambda i: (0, i, 0))
    return pl.pallas_call(
        _mla_proj_kernel,
        grid=(M // tm,),
        in_specs=[rows(K),
                  pl.BlockSpec((None, K, n_lat), lambda i: (layer, 0, COL_LATENT // n_lat), pipeline_mode=once),
                  pl.BlockSpec((None, K, LANES), lambda i: (layer, 0, COL_KPE // LANES), pipeline_mode=once),
                  pl.BlockSpec((None, Q_LORA, MLA_HEADS * QK_PAD), lambda i: (layer, 0, 0), pipeline_mode=once),
                  pl.BlockSpec((None, KV_LORA, MLA_HEADS * (QK_NOPE + V_DIM)), lambda i: (layer, 0, 0),
                               pipeline_mode=once),
                  _layer_row(Q_LORA, layer), _layer_row(KV_LORA, layer),
                  rows(LANES), rows(LANES)],
        out_specs=[heads, heads, rows(MLA_HEADS * V_DIM)],
        out_shape=[jax.ShapeDtypeStruct((MLA_HEADS, M, QK_PAD), BF16),
                   jax.ShapeDtypeStruct((MLA_HEADS, M, QK_PAD), BF16),
                   jax.ShapeDtypeStruct((M, MLA_HEADS * V_DIM), BF16)],
        name="mla_proj",
        compiler_params=_cparams("arbitrary"),
    )(h, w_in_lo, w_in_lo, w_q, w_kv, qg3, kvg3, cos, sin)


ATTN_RB = CHUNK
ATTN_UNROLLS = (8, 4, 2)


def _attn_kernel(q_ref, k_ref, v_ref, o_ref, vone_ref, *set_refs, tq):
    vone_ref[:, 0:V_DIM] = v_ref[...]
    vone_ref[:, V_DIM:] = jnp.ones((vone_ref.shape[0], V_DIM), vone_ref.dtype)
    n_set = len(set_refs) // 2
    sets = [_AttnBlock(q_ref, k_ref, vone_ref, o_ref, set_refs[i * n_set:(i + 1) * n_set], tq, i)
            for i in range(2)]
    n_q = q_ref.shape[1] // tq
    assert n_q % 2 == 0

    sets[0].first_block()
    sets[1].fill(jnp.int32(1))

    def query_block_pair(qq, carry):
        qa, qb = 2 * qq, 2 * qq + 1
        sets[1].drain(qa - 1)
        sets[0].fill(qa)
        sets[0].steady(qa)
        sets[0].drain(qa)
        sets[1].fill(qb)
        sets[1].steady(qb)
        return carry

    lax.fori_loop(1, n_q // 2, query_block_pair, 0)
    sets[1].drain(jnp.int32(n_q - 1))


class _AttnBlock:
    def __init__(self, q_ref, k_ref, v_ref, o_ref, refs, tq, par):
        s0, s1, p0, p1, a0, a1, self.m_ref, self.acc_ref = refs
        self.s_refs, self.p_refs, self.a_refs = (s0, s1), (p0, p1), (a0, a1)
        self.q_ref, self.k_ref, self.v_ref, self.o_ref = q_ref, k_ref, v_ref, o_ref
        self.tq, self.par = tq, par

    def _q_rows(self, qi):
        return pl.ds(pl.multiple_of(qi * self.tq, self.tq), self.tq)

    def _init(self):
        self.m_ref[...] = jnp.full(self.m_ref.shape, NEG_BIG, F32)
        self.acc_ref[...] = jnp.zeros(self.acc_ref.shape, F32)

    def scores(self, qi, j, par):
        start = pl.multiple_of(j * self.tq, self.tq)
        kb = self.k_ref[0, pl.ds(start, self.tq), :]
        self.s_refs[par][...] = lax.dot_general(self.q_ref[0, self._q_rows(qi), :], kb,
                                                (((1,), (1,)), ((), ())), preferred_element_type=F32)

    def softmax(self, par, masked):
        s_ref, p_ref, a_ref, m_ref = self.s_refs[par], self.p_refs[par], self.a_refs[par], self.m_ref
        for r in range(self.tq // ATTN_RB):
            rows = slice(r * ATTN_RB, (r + 1) * ATTN_RB)
            cols = []
            for c in range(self.tq // LANES):
                first_chunk = (c * LANES) // CHUNK
                if masked and first_chunk > r:
                    cols.append(None)
                    continue
                sc = s_ref[rows, c * LANES:(c + 1) * LANES]
                if masked and first_chunk == r:
                    lane = lax.broadcasted_iota(jnp.int32, sc.shape, 1)
                    sc = jnp.where(lane < CHUNK, sc, NEG_BIG)
                cols.append(sc)
            mx = functools.reduce(jnp.maximum, [sc for sc in cols if sc is not None])
            m_prev = m_ref[rows, :]
            m_new = jnp.maximum(m_prev, jnp.max(mx, axis=-1, keepdims=True))
            m_ref[rows, :] = m_new
            a_ref[rows, :] = jnp.exp2(m_prev - m_new)
            for c, sc in enumerate(cols):
                pc = jnp.zeros((ATTN_RB, LANES), BF16) if sc is None else jnp.exp2(sc - m_new).astype(BF16)
                p_ref[rows, c * LANES:(c + 1) * LANES] = pc

    def values(self, j, par):
        start = pl.multiple_of(j * self.tq, self.tq)
        vb = self.v_ref[pl.ds(start, self.tq), :]
        alpha = self.a_refs[par][...]
        self.acc_ref[...] = (jnp.concatenate([alpha, alpha], axis=1) * self.acc_ref[...]
                             + jnp.dot(self.p_refs[par][...], vb, preferred_element_type=F32))

    def _write(self, qi):
        acc = self.acc_ref
        self.o_ref[self._q_rows(qi), :] = (acc[:, 0:V_DIM] / acc[:, V_DIM:]).astype(self.o_ref.dtype)

    def first_block(self):
        qi = jnp.int32(0)
        self._init()
        self.scores(qi, qi, 0)
        self.softmax(0, True)
        self.values(qi, 0)
        self._write(qi)

    def fill(self, qi):
        self._init()
        self.scores(qi, 0, 0)
        self.scores(qi, 1, 1)
        self.softmax(0, False)

    def _tick(self, qi, t, par):
        self.scores(qi, t, par)
        self.softmax(1 - par, False)
        self.values(t - 2, par)

    def steady(self, qi):
        n_left = jnp.maximum(qi - 1, 0)
        t0 = 2
        for unroll in ATTN_UNROLLS:

            def unrolled(i, carry, t0=t0, unroll=unroll):
                for u in range(unroll):
                    self._tick(qi, t0 + unroll * i + u, u % 2)
                return carry

            lax.fori_loop(0, n_left // unroll, unrolled, 0)
            t0 = t0 + (n_left // unroll) * unroll
            n_left = n_left % unroll

        @pl.when(n_left == 1)
        def _():
            self._tick(qi, qi, 0)

    def drain(self, qi):
        self.softmax(self.par, True)
        self.values(qi - 1, 1 - self.par)
        self.values(qi, self.par)
        self._write(qi)


def mla_attention(q, k, v, tq=512):
    H, S, _ = q.shape
    buffer_set = ([pltpu.VMEM((tq, tq), F32)] * 2
                  + [pltpu.VMEM((tq, tq), BF16)] * 2
                  + [pltpu.VMEM((tq, LANES), F32)] * 2
                  + [pltpu.VMEM((tq, LANES), F32),
                     pltpu.VMEM((tq, 2 * V_DIM), F32)])
    return pl.pallas_call(
        functools.partial(_attn_kernel, tq=tq),
        grid=(H,),
        in_specs=[pl.BlockSpec((1, S, QK_PAD), lambda h: (h, 0, 0)),
                  pl.BlockSpec((1, S, QK_PAD), lambda h: (h, 0, 0)),
                  pl.BlockSpec((S, V_DIM), lambda h: (0, h))],
        out_specs=pl.BlockSpec((S, V_DIM), lambda h: (0, h)),
        out_shape=jax.ShapeDtypeStruct((S, H * V_DIM), BF16),
        scratch_shapes=[pltpu.VMEM((S, 2 * V_DIM), BF16)] + buffer_set + buffer_set,
        name="mla_attention",
        compiler_params=_cparams("arbitrary"),
    )(q, k, v)


CONV_RB = 64
CONV_CB = LANES


def _conv_kernel(cur_ref, halo_ref, w_ref, b_ref, g_ref, beta_ref, o_ref, buf_ref, sh_ref, acc_ref, wb_ref, *, tm):
    i = pl.program_id(0)

    @pl.when(i == 0)
    def _():
        for k in range(CONV_K):
            wb_ref[k] = jnp.broadcast_to(w_ref[k:k + 1, :], (SUBLANES, D_CONV))

    buf_ref[0:CONV_HALO, :] = jnp.where(i > 0, halo_ref[...], 0.0)
    buf_ref[CONV_HALO:CONV_HALO + tm, :] = cur_ref[...]
    n_buf = tm + CONV_HALO
    n_sh = n_buf - SUBLANES
    for c0 in range(0, D_CONV, LANES):
        x = buf_ref[:, c0:c0 + LANES]
        for b in range(1, SUBLANES):
            sh_ref[b - 1, :, c0:c0 + LANES] = pltpu.roll(x, n_buf - b, 0)[0:n_sh, :]
    off = CONV_HALO - (CONV_K - 1)
    for c0 in range(0, D_CONV, CONV_CB):
        cols = slice(c0, c0 + CONV_CB)
        w_all = wb_ref[:, :, cols]
        bias = jnp.broadcast_to(b_ref[:, cols], (CONV_RB, CONV_CB))

        def row_block(rb, carry, cols=cols, w_all=w_all, bias=bias):
            r0 = pl.multiple_of(rb * CONV_RB, CONV_RB)
            acc = bias
            for b in range(SUBLANES):
                taps = [k for k in range(CONV_K) if (off + k) % SUBLANES == b]
                a_lo, a_hi = (off + taps[0]) // SUBLANES, (off + taps[-1]) // SUBLANES
                rows = pl.ds(r0 + a_lo * SUBLANES, (a_hi - a_lo) * SUBLANES + CONV_RB)
                slab = buf_ref[rows, cols] if b == 0 else sh_ref[b - 1, rows, cols]
                for k in taps:
                    s0 = ((off + k) // SUBLANES - a_lo) * SUBLANES
                    wk = jnp.concatenate([w_all[k]] * (CONV_RB // SUBLANES), axis=0)
                    acc = acc + slab[s0:s0 + CONV_RB, :] * wk
            acc_ref[pl.ds(r0, CONV_RB), cols] = acc
            return carry

        lax.fori_loop(0, tm // CONV_RB, row_block, 0)
    y = acc_ref[...]
    mu = jnp.mean(y, axis=-1, keepdims=True)
    yc = y - mu
    yn = yc * lax.rsqrt(jnp.mean(yc * yc, axis=-1, keepdims=True) + EPS) * g_ref[...] + beta_ref[...]
    o_ref[...] = (yn * jax.nn.sigmoid(yn)).astype(o_ref.dtype)


def conv_ln_silu(hglu, conv_w, conv_b3, ln_g3, ln_b3, layer, tm=256):
    S, D = hglu.shape
    row = _layer_row(D, layer)
    return pl.pallas_call(
        functools.partial(_conv_kernel, tm=tm),
        grid=(S // tm,),
        in_specs=[pl.BlockSpec((tm, D), lambda i: (i, 0)),
                  pl.BlockSpec((CONV_HALO, D), lambda i: (jnp.maximum(i * (tm // CONV_HALO) - 1, 0), 0)),
                  pl.BlockSpec((None, CONV_K, D), lambda i: (layer, 0, 0)),
                  row, row, row],
        out_specs=pl.BlockSpec((tm, D), lambda i: (i, 0)),
        out_shape=jax.ShapeDtypeStruct((S, D), BF16),
        scratch_shapes=[pltpu.VMEM((CONV_HALO + tm, D), F32),
                        pltpu.VMEM((SUBLANES - 1, CONV_HALO + tm - SUBLANES, D), F32),
                        pltpu.VMEM((tm, D), F32),
                        pltpu.VMEM((CONV_K, SUBLANES, D), F32)],
        name="conv_ln_silu",
        compiler_params=_cparams("arbitrary"),
    )(hglu, hglu, conv_w, conv_b3, ln_g3, ln_b3)


def _memkv_kernel(mem_ref, g_ref, w_ref, k_ref, v_ref):
    mn = _rms(mem_ref[...], g_ref[...]).astype(BF16)
    kv = jnp.dot(mn, w_ref[...], preferred_element_type=F32)
    n = X_HEADS * X_HEAD_DIM
    k_ref[...] = kv[:, :n].astype(k_ref.dtype)
    v_ref[...] = kv[:, n:].astype(v_ref.dtype)


def mem_kv(mem, g3, w, layer):
    n = X_HEADS * X_HEAD_DIM
    D = mem.shape[1]
    full = lambda shape: pl.BlockSpec(shape, lambda i: (0, 0))
    return pl.pallas_call(
        _memkv_kernel,
        grid=(1,),
        in_specs=[full(mem.shape), _layer_row(D, layer),
                  pl.BlockSpec((None, D, 2 * n), lambda i: (layer, 0, 0))],
        out_specs=[full((N_MEM, n)), full((N_MEM, n))],
        out_shape=[jax.ShapeDtypeStruct((N_MEM, n), BF16)] * 2,
        name="mem_kv",
        compiler_params=_cparams("arbitrary"),
    )(mem, g3, w)


def _mix_cross_kernel(m_ref, x_ref, wo_ref, gmem_ref, wq_ref, k_ref, v_ref, wxo_ref, gffn_ref, xo_ref, ho_ref):
    x1 = x_ref[...] + jnp.dot(m_ref[...], wo_ref[...], preferred_element_type=F32)
    hm = _rms(x1, gmem_ref[...]).astype(BF16)
    q = jnp.dot(hm, wq_ref[...], preferred_element_type=F32) * (X_HEAD_DIM ** -0.5)
    q = q.astype(BF16)
    outs = []
    for h in range(X_HEADS):
        sl = slice(h * X_HEAD_DIM, (h + 1) * X_HEAD_DIM)
        s = lax.dot_general(q[:, sl], k_ref[:, sl], (((1,), (1,)), ((), ())), preferred_element_type=F32)
        p = jnp.exp(s - jnp.max(s, axis=-1, keepdims=True))
        l = jnp.sum(p, axis=-1, keepdims=True)
        o = jnp.dot(p.astype(BF16), v_ref[:, sl], preferred_element_type=F32) / l
        outs.append(o.astype(BF16))
    o = jnp.concatenate(outs, axis=-1)
    x2 = x1 + jnp.dot(o, wxo_ref[...], preferred_element_type=F32)
    xo_ref[...] = x2
    ho_ref[...] = _rms(x2, gffn_ref[...]).astype(ho_ref.dtype)


def mix_cross(merged, x, w_out, g_mem3, w_xq, mk, mv, w_xo, g_ffn3, layer, tm=512):
    M, D = merged.shape
    n = X_HEADS * X_HEAD_DIM
    const = lambda shape: pl.BlockSpec(shape, lambda i: (0, 0))
    rows = pl.BlockSpec((tm, D), lambda i: (i, 0))
    once = pl.Buffered(1)
    weight = lambda k, c: pl.BlockSpec((None, k, c), lambda i: (layer, 0, 0), pipeline_mode=once)
    return pl.pallas_call(
        _mix_cross_kernel,
        grid=(M // tm,),
        in_specs=[rows, rows, weight(D, D), _layer_row(D, layer), weight(D, n),
                  const((N_MEM, n)), const((N_MEM, n)), weight(n, D), _layer_row(D, layer)],
        out_specs=[rows, rows],
        out_shape=[jax.ShapeDtypeStruct((M, D), F32), jax.ShapeDtypeStruct((M, D), BF16)],
        name="mix_cross",
        compiler_params=_cparams("arbitrary"),
    )(merged, x, w_out, g_mem3, w_xq, mk, mv, w_xo, g_ffn3)


def _mlp_kernel(h_ref, w1_ref, w2_ref, x_ref, g_ref, xo_ref, ho_ref):
    f = pl.program_id(1)

    @pl.when(f == 0)
    def _():
        xo_ref[...] = x_ref[...]

    a = jnp.dot(h_ref[...], w1_ref[...], preferred_element_type=F32)
    a = jnp.square(jnp.maximum(a, 0.0)).astype(BF16)
    xo_ref[...] += jnp.dot(a, w2_ref[...], preferred_element_type=F32)

    @pl.when(f == pl.num_programs(1) - 1)
    def _():
        ho_ref[...] = _rms(xo_ref[...], g_ref[...]).astype(ho_ref.dtype)


def mlp(h, w1, w2, layer, x, g3, g_layer, out_dtype, tm=512, tf=1024):
    M, D = h.shape
    F = w1.shape[2]
    rows = pl.BlockSpec((tm, D), lambda i, f: (i, 0))
    return pl.pallas_call(
        _mlp_kernel,
        grid=(M // tm, F // tf),
        in_specs=[rows,
                  pl.BlockSpec((None, D, tf), lambda i, f: (layer, 0, f)),
                  pl.BlockSpec((None, tf, D), lambda i, f: (layer, f, 0)),
                  rows,
                  _layer_row(D, g_layer)],
        out_specs=[rows, rows],
        out_shape=[jax.ShapeDtypeStruct((M, D), F32), jax.ShapeDtypeStruct((M, D), out_dtype)],
        name="mlp",
        compiler_params=_cparams("arbitrary", "arbitrary"),
    )(h, w1, w2, x, g3)


def _permute_q_weight(w_uq):
    L, K, _ = w_uq.shape
    half = QK_ROPE // 2
    w = w_uq.reshape(L, K, MLA_HEADS, QK_NOPE + QK_ROPE)
    pe = w[..., QK_NOPE:]
    w = jnp.concatenate([w, pe[..., half:], pe[..., :half]], axis=-1)
    return w.reshape(L, K, MLA_HEADS * QK_PAD)


def kernel(x, mem, positions, norm_mix_g, w_in, b_gate, conv_w, conv_b, conv_ln_g, conv_ln_b, w_conv_out, q_norm_g, w_uq, kv_norm_g, w_ukv, w_mla_out, w_out, norm_mem_g, mem_norm_g, w_xq, w_xkv, w_xo, norm_ffn_g, w_ff1, w_ff2, final_norm_g):
    B, S, D = x.shape
    assert (B, S, D) == (1, SEQ, D_MODEL)
    x2d = x.reshape(S, D)
    mem2d = mem.reshape(N_MEM, D)
    vec3 = lambda v: v.reshape(v.shape[0], 1, v.shape[1])

    inv_freq = 1.0 / (ROPE_THETA ** (jnp.arange(0, QK_ROPE, 2, dtype=F32) / QK_ROPE))
    cos, sin = rope_tables(positions.reshape(S, 1), jnp.tile(inv_freq, 4).reshape(1, LANES))

    w_in_lo = w_in.astype(BF16)
    w_gate = w_in_lo[:, :, COL_GATE:]
    w_q = _permute_q_weight(w_uq).astype(BF16)
    w_kv, w_co, w_mo, w_o = (w.astype(BF16) for w in (w_ukv, w_conv_out, w_mla_out, w_out))
    w_q_x, w_kv_x, w_o_x = (w.astype(BF16) for w in (w_xq, w_xkv, w_xo))
    w_1, w_2 = w_ff1.astype(BF16), w_ff2.astype(BF16)

    g_mix, g_mem, g_memn, g_ffn = vec3(norm_mix_g), vec3(norm_mem_g), vec3(mem_norm_g), vec3(norm_ffn_g)
    g_q, g_kv, b_g = vec3(q_norm_g), vec3(kv_norm_g), vec3(b_gate)
    c_b, ln_g, ln_b = vec3(conv_b), vec3(conv_ln_g), vec3(conv_ln_b)
    g_final = final_norm_g.reshape(1, 1, D)

    h = rms_norm_rows(x2d, g_mix, 0, BF16)
    xcur = x2d
    for l in range(DEPTH):
        (hglu,) = matmul("glu_proj", [h], [(0, w_in_lo, l, COL_GLU_A), (0, w_in_lo, l, COL_GLU_G)], [],
                         [F32], _epi_glu, D_CONV, 1024, 1024)

        hc = conv_ln_silu(hglu, conv_w, c_b, ln_g, ln_b, l)
        (gy,) = matmul("conv_out_gated", [hc, h], [(0, w_co, l, 0), (1, w_gate, l, 0)],
                       [("row", b_g, l, 0)], [BF16], _epi_gated, D, 1024, 512)

        q, k, v = mla_proj(h, w_in_lo, w_q, w_kv, l, g_q, g_kv, cos, sin)
        o = mla_attention(q, k, v)
        (merged,) = matmul("mla_out_gated_merge", [o, h], [(0, w_mo, l, 0), (1, w_gate, l, D)],
                           [("row", b_g, l, D), ("tile", gy, 0)], [BF16], _epi_gated_merge, D, 1024, 512)

        mk, mv = mem_kv(mem2d, g_memn, w_kv_x, l)
        xcur, hf = mix_cross(merged, xcur, w_o, g_mem, w_q_x, mk, mv, w_o_x, g_ffn, l)

        last = l == DEPTH - 1
        xcur, h = mlp(hf, w_1, w_2, l, xcur, g_final if last else g_mix, 0 if last else l + 1,
                      F32 if last else BF16)
    return h.reshape(B, S, D)
```

```python
import functools
import math

import jax
import jax.numpy as jnp
from jax import lax
from jax.experimental import pallas as pl
from jax.experimental.pallas import tpu as pltpu

D_MODEL = 2048
SEQ = 8192
DEPTH = 2
CHUNK = 64
N_MEM = 256
EPS = 1e-6
D_CONV = D_MODEL
CONV_K = 31
MLA_HEADS = 16
Q_LORA = 512
KV_LORA = 512
QK_NOPE = 128
QK_ROPE = 64
V_DIM = 128
ROPE_THETA = 10000.0
X_HEADS = 4
X_HEAD_DIM = 128
D_FF = 4 * D_MODEL

LANES = 128
SUBLANES = 8
QK_PAD = 256
CONV_HALO = 32
NEG_BIG = -1e30
VMEM_LIMIT = 56 * 1024 * 1024

COL_GLU_A = 0
COL_GLU_G = D_CONV
COL_LATENT = 2 * D_CONV
COL_KPE = COL_LATENT + Q_LORA + KV_LORA
COL_GATE = COL_KPE + QK_ROPE

F32 = jnp.float32
BF16 = jnp.bfloat16


def _cparams(*sem):
    return pltpu.CompilerParams(dimension_semantics=sem, vmem_limit_bytes=VMEM_LIMIT)


def _rms(x, g):
    return x * lax.rsqrt(jnp.mean(x * x, axis=-1, keepdims=True) + EPS) * g


def _layer_row(n, layer, col0=0):
    return pl.BlockSpec((None, 1, n), lambda *_: (layer, 0, col0 // n))


def _rope_table_kernel(pos_ref, freq_ref, cos_ref, sin_ref):
    ang = pos_ref[...].astype(F32) * freq_ref[...]
    lane = lax.broadcasted_iota(jnp.int32, ang.shape, 1)
    s = jnp.sin(ang)
    cos_ref[...] = jnp.where(lane < QK_ROPE, jnp.cos(ang), 0.0)
    sin_ref[...] = jnp.where(lane < QK_ROPE // 2, -s, jnp.where(lane < QK_ROPE, s, 0.0))


def rope_tables(pos_col, freq_row):
    S = pos_col.shape[0]
    tm = 1024
    tab = pl.BlockSpec((tm, LANES), lambda i: (i, 0))
    return pl.pallas_call(
        _rope_table_kernel,
        grid=(S // tm,),
        in_specs=[pl.BlockSpec((tm, 1), lambda i: (i, 0)),
                  pl.BlockSpec((1, LANES), lambda i: (0, 0))],
        out_specs=[tab, tab],
        out_shape=[jax.ShapeDtypeStruct((S, LANES), F32)] * 2,
        name="rope_table",
        compiler_params=_cparams("arbitrary"),
    )(pos_col, freq_row)


W_IN_LO_COLS = COL_KPE + LANES


def _cast_w_in_kernel(w_ref, lo_ref, gate_ref):
    lo_ref[...] = w_ref[:, 0:W_IN_LO_COLS].astype(lo_ref.dtype)
    gate_ref[...] = w_ref[:, COL_GATE:].astype(gate_ref.dtype)


def cast_w_in(w_in, tk=256):
    L, K, N = w_in.shape
    n_gate = N - COL_GATE
    return pl.pallas_call(
        _cast_w_in_kernel,
        grid=(L, K // tk),
        in_specs=[pl.BlockSpec((None, tk, N), lambda l, i: (l, i, 0))],
        out_specs=[pl.BlockSpec((None, tk, W_IN_LO_COLS), lambda l, i: (l, i, 0)),
                   pl.BlockSpec((None, tk, n_gate), lambda l, i: (l, i, 0))],
        out_shape=[jax.ShapeDtypeStruct((L, K, W_IN_LO_COLS), BF16),
                   jax.ShapeDtypeStruct((L, K, n_gate), BF16)],
        name="cast_w_in",
        compiler_params=_cparams("arbitrary", "arbitrary"),
    )(w_in)


def _norm_kernel(x_ref, g_ref, o_ref):
    o_ref[...] = _rms(x_ref[...], g_ref[...]).astype(o_ref.dtype)


def rms_norm_rows(x, g3, layer, out_dtype, tm=512):
    M, D = x.shape
    return pl.pallas_call(
        _norm_kernel,
        grid=(M // tm,),
        in_specs=[pl.BlockSpec((tm, D), lambda i: (i, 0)), _layer_row(D, layer)],
        out_specs=pl.BlockSpec((tm, D), lambda i: (i, 0)),
        out_shape=jax.ShapeDtypeStruct((M, D), out_dtype),
        name="rms_norm",
        compiler_params=_cparams("arbitrary"),
    )(x, g3)


def _mm_kernel(*refs, a_index, n_a, n_extra, epilogue):
    n_b = len(a_index)
    acts = [r[...] for r in refs[:n_a]]
    accs = [jnp.dot(acts[ai], b[...], preferred_element_type=F32)
            for ai, b in zip(a_index, refs[n_a:n_a + n_b])]
    extras = [e[...] for e in refs[n_a + n_b:n_a + n_b + n_extra]]
    outs = epilogue(accs, extras)
    for o_ref, o in zip(refs[n_a + n_b + n_extra:], outs):
        o_ref[...] = o.astype(o_ref.dtype)


def matmul(name, acts, weights, extras, out_dtypes, epilogue, n_out, tm, tn):
    M = acts[0].shape[0]
    in_specs = [pl.BlockSpec((tm, a.shape[1]), lambda i, j: (i, 0)) for a in acts]
    operands = list(acts)
    for _, w, layer, col0 in weights:
        in_specs.append(pl.BlockSpec((None, w.shape[1], tn),
                                     lambda i, j, layer=layer, off=col0 // tn: (layer, 0, j + off)))
        operands.append(w)
    for e in extras:
        if e[0] == "row":
            _, arr, layer, col0 = e
            in_specs.append(pl.BlockSpec((None, 1, tn),
                                         lambda i, j, layer=layer, off=col0 // tn: (layer, 0, j + off)))
        else:
            _, arr, col0 = e
            in_specs.append(pl.BlockSpec((tm, tn), lambda i, j, off=col0 // tn: (i, j + off)))
        operands.append(arr)
    return pl.pallas_call(
        functools.partial(_mm_kernel, a_index=tuple(w[0] for w in weights), n_a=len(acts),
                          n_extra=len(extras), epilogue=epilogue),
        grid=(M // tm, n_out // tn),
        in_specs=in_specs,
        out_specs=[pl.BlockSpec((tm, tn), lambda i, j: (i, j)) for _ in out_dtypes],
        out_shape=[jax.ShapeDtypeStruct((M, n_out), dt) for dt in out_dtypes],
        name=name,
        compiler_params=_cparams("arbitrary", "arbitrary"),
    )(*operands)


def _epi_glu(accs, extras):
    return [accs[0] * jax.nn.sigmoid(accs[1])]


def _epi_gated(accs, extras):
    return [accs[0] * jax.nn.sigmoid(accs[1] + extras[0])]


def _epi_gated_merge(accs, extras):
    return [extras[1] + accs[0] * jax.nn.sigmoid(accs[1] + extras[0])]


def _mla_proj_kernel(h_ref, wlat_ref, wpe_ref, wq_ref, wkv_ref, qg_ref, kvg_ref, cos_ref, sin_ref,
                     q_ref, k_ref, v_ref):
    h = h_ref[...]
    cos, sin = cos_ref[...], sin_ref[...]
    z = jnp.dot(h, wlat_ref[...], preferred_element_type=F32)
    cq = _rms(z[:, :Q_LORA], qg_ref[...]).astype(BF16)
    ckv = _rms(z[:, Q_LORA:], kvg_ref[...]).astype(BF16)
    pe = jnp.dot(h, wpe_ref[...], preferred_element_type=F32)
    half = QK_ROPE // 2
    lane = lax.broadcasted_iota(jnp.int32, pe.shape, 1)
    swapped = jnp.where(lane < half, pltpu.roll(pe, LANES - half, 1), pltpu.roll(pe, half, 1))
    kpe = jnp.where(lane < QK_ROPE, pe * cos + swapped * sin, 0.0).astype(k_ref.dtype)

    scale = (QK_NOPE + QK_ROPE) ** -0.5 * math.log2(math.e)
    kv_width = QK_NOPE + V_DIM
    for hd in range(MLA_HEADS):
        blk = jnp.dot(cq, wq_ref[:, hd * QK_PAD:(hd + 1) * QK_PAD], preferred_element_type=F32)
        q_ref[hd, :, 0:QK_NOPE] = (blk[:, :QK_NOPE] * scale).astype(q_ref.dtype)
        qpe = blk[:, QK_NOPE:]
        roped = qpe * cos + pltpu.roll(qpe, QK_ROPE, 1) * sin
        q_ref[hd, :, QK_NOPE:QK_PAD] = (roped * scale).astype(q_ref.dtype)

        blk = jnp.dot(ckv, wkv_ref[:, hd * kv_width:(hd + 1) * kv_width], preferred_element_type=F32)
        k_ref[hd, :, 0:QK_NOPE] = blk[:, :QK_NOPE].astype(k_ref.dtype)
        k_ref[hd, :, QK_NOPE:QK_PAD] = kpe
        v_ref[:, hd * V_DIM:(hd + 1) * V_DIM] = blk[:, QK_NOPE:].astype(v_ref.dtype)


def mla_proj(h, w_in_lo, w_q, w_kv, layer, qg3, kvg3, cos, sin, tm=512):
    M, K = h.shape
    n_lat = Q_LORA + KV_LORA
    rows = lambda n: pl.BlockSpec((tm, n), lambda i: (i, 0))
    once = pl.Buffered(1)
    heads = pl.BlockSpec((MLA_HEADS, tm, QK_PAD), lambda i: (0, i, 0))
    return pl.pallas_call(
        _mla_proj_kernel,
        grid=(M // tm,),
        in_specs=[rows(K),
                  pl.BlockSpec((None, K, n_lat), lambda i: (layer, 0, COL_LATENT // n_lat), pipeline_mode=once),
                  pl.BlockSpec((None, K, LANES), lambda i: (layer, 0, COL_KPE // LANES), pipeline_mode=once),
                  pl.BlockSpec((None, Q_LORA, MLA_HEADS * QK_PAD), lambda i: (layer, 0, 0), pipeline_mode=once),
                  pl.BlockSpec((None, KV_LORA, MLA_HEADS * (QK_NOPE + V_DIM)), lambda i: (layer, 0, 0),
                               pipeline_mode=once),
                  _layer_row(Q_LORA, layer), _layer_row(KV_LORA, layer),
                  rows(LANES), rows(LANES)],
        out_specs=[heads, heads, rows(MLA_HEADS * V_DIM)],
        out_shape=[jax.ShapeDtypeStruct((MLA_HEADS, M, QK_PAD), BF16),
                   jax.ShapeDtypeStruct((MLA_HEADS, M, QK_PAD), BF16),
                   jax.ShapeDtypeStruct((M, MLA_HEADS * V_DIM), BF16)],
        name="mla_proj",
        compiler_params=_cparams("arbitrary"),
    )(h, w_in_lo, w_in_lo, w_q, w_kv, qg3, kvg3, cos, sin)


ATTN_RB = CHUNK
ATTN_UNROLLS = (8, 4, 2)


def _attn_kernel(q_ref, k_ref, v_ref, o_ref, vone_ref, *set_refs, tq):
    vone_ref[:, 0:V_DIM] = v_ref[...]
    vone_ref[:, V_DIM:] = jnp.ones((vone_ref.shape[0], V_DIM), vone_ref.dtype)
    n_set = len(set_refs) // 2
    sets = [_AttnBlock(q_ref, k_ref, vone_ref, o_ref, set_refs[i * n_set:(i + 1) * n_set], tq, i)
            for i in range(2)]
    n_q = q_ref.shape[1] // tq
    assert n_q % 2 == 0

    sets[0].first_block()
    sets[1].fill(jnp.int32(1))

    def query_block_pair(qq, carry):
        qa, qb = 2 * qq, 2 * qq + 1
        sets[1].drain(qa - 1)
        sets[0].fill(qa)
        sets[0].steady(qa)
        sets[0].drain(qa)
        sets[1].fill(qb)
        sets[1].steady(qb)
        return carry

    lax.fori_loop(1, n_q // 2, query_block_pair, 0)
    sets[1].drain(jnp.int32(n_q - 1))


class _AttnBlock:
    def __init__(self, q_ref, k_ref, v_ref, o_ref, refs, tq, par):
        s0, s1, p0, p1, a0, a1, self.m_ref, self.acc_ref = refs
        self.s_refs, self.p_refs, self.a_refs = (s0, s1), (p0, p1), (a0, a1)
        self.q_ref, self.k_ref, self.v_ref, self.o_ref = q_ref, k_ref, v_ref, o_ref
        self.tq, self.par = tq, par

    def _q_rows(self, qi):
        return pl.ds(pl.multiple_of(qi * self.tq, self.tq), self.tq)

    def _init(self):
        self.m_ref[...] = jnp.full(self.m_ref.shape, NEG_BIG, F32)
        self.acc_ref[...] = jnp.zeros(self.acc_ref.shape, F32)

    def scores(self, qi, j, par):
        start = pl.multiple_of(j * self.tq, self.tq)
        kb = self.k_ref[0, pl.ds(start, self.tq), :]
        self.s_refs[par][...] = lax.dot_general(self.q_ref[0, self._q_rows(qi), :], kb,
                                                (((1,), (1,)), ((), ())), preferred_element_type=F32)

    def softmax(self, par, masked):
        s_ref, p_ref, a_ref, m_ref = self.s_refs[par], self.p_refs[par], self.a_refs[par], self.m_ref
        for r in range(self.tq // ATTN_RB):
            rows = slice(r * ATTN_RB, (r + 1) * ATTN_RB)
            cols = []
            for c in range(self.tq // LANES):
                first_chunk = (c * LANES) // CHUNK
                if masked and first_chunk > r:
                    cols.append(None)
                    continue
                sc = s_ref[rows, c * LANES:(c + 1) * LANES]
                if masked and first_chunk == r:
                    lane = lax.broadcasted_iota(jnp.int32, sc.shape, 1)
                    sc = jnp.where(lane < CHUNK, sc, NEG_BIG)
                cols.append(sc)
            mx = functools.reduce(jnp.maximum, [sc for sc in cols if sc is not None])
            m_prev = m_ref[rows, :]
            m_new = jnp.maximum(m_prev, jnp.max(mx, axis=-1, keepdims=True))
            m_ref[rows, :] = m_new
            a_ref[rows, :] = jnp.exp2(m_prev - m_new)
            for c, sc in enumerate(cols):
                pc = jnp.zeros((ATTN_RB, LANES), BF16) if sc is None else jnp.exp2(sc - m_new).astype(BF16)
                p_ref[rows, c * LANES:(c + 1) * LANES] = pc

    def values(self, j, par):
        start = pl.multiple_of(j * self.tq, self.tq)
        vb = self.v_ref[pl.ds(start, self.tq), :]
        alpha = self.a_refs[par][...]
        self.acc_ref[...] = (jnp.concatenate([alpha, alpha], axis=1) * self.acc_ref[...]
                             + jnp.dot(self.p_refs[par][...], vb, preferred_element_type=F32))

    def _write(self, qi):
        acc = self.acc_ref
        self.o_ref[self._q_rows(qi), :] = (acc[:, 0:V_DIM] / acc[:, V_DIM:]).astype(self.o_ref.dtype)

    def first_block(self):
        qi = jnp.int32(0)
        self._init()
        self.scores(qi, qi, 0)
        self.softmax(0, True)
        self.values(qi, 0)
        self._write(qi)

    def fill(self, qi):
        self._init()
        self.scores(qi, 0, 0)
        self.scores(qi, 1, 1)
        self.softmax(0, False)

    def _tick(self, qi, t, par):
        self.scores(qi, t, par)
        self.softmax(1 - par, False)
        self.values(t - 2, par)

    def steady(self, qi):
        n_left = jnp.maximum(qi - 1, 0)
        t0 = 2
        for unroll in ATTN_UNROLLS:

            def unrolled(i, carry, t0=t0, unroll=unroll):
                for u in range(unroll):
                    self._tick(qi, t0 + unroll * i + u, u % 2)
                return carry

            lax.fori_loop(0, n_left // unroll, unrolled, 0)
            t0 = t0 + (n_left // unroll) * unroll
            n_left = n_left % unroll

        @pl.when(n_left == 1)
        def _():
            self._tick(qi, qi, 0)

    def drain(self, qi):
        self.softmax(self.par, True)
        self.values(qi - 1, 1 - self.par)
        self.values(qi, self.par)
        self._write(qi)


def mla_attention(q, k, v, tq=512):
    H, S, _ = q.shape
    buffer_set = ([pltpu.VMEM((tq, tq), F32)] * 2
                  + [pltpu.VMEM((tq, tq), BF16)] * 2
                  + [pltpu.VMEM((tq, LANES), F32)] * 2
                  + [pltpu.VMEM((tq, LANES), F32),
                     pltpu.VMEM((tq, 2 * V_DIM), F32)])
    return pl.pallas_call(
        functools.partial(_attn_kernel, tq=tq),
        grid=(H,),
        in_specs=[pl.BlockSpec((1, S, QK_PAD), lambda h: (h, 0, 0)),
                  pl.BlockSpec((1, S, QK_PAD), lambda h: (h, 0, 0)),
                  pl.BlockSpec((S, V_DIM), lambda h: (0, h))],
        out_specs=pl.BlockSpec((S, V_DIM), lambda h: (0, h)),
        out_shape=jax.ShapeDtypeStruct((S, H * V_DIM), BF16),
        scratch_shapes=[pltpu.VMEM((S, 2 * V_DIM), BF16)] + buffer_set + buffer_set,
        name="mla_attention",
        compiler_params=_cparams("arbitrary"),
    )(q, k, v)


CONV_RB = 64
CONV_CB = LANES


def _conv_kernel(cur_ref, halo_ref, w_ref, b_ref, g_ref, beta_ref, o_ref, buf_ref, sh_ref, acc_ref, wb_ref, *, tm):
    i = pl.program_id(0)

    @pl.when(i == 0)
    def _():
        for k in range(CONV_K):
            wb_ref[k] = jnp.broadcast_to(w_ref[k:k + 1, :], (SUBLANES, D_CONV))

    buf_ref[0:CONV_HALO, :] = jnp.where(i > 0, halo_ref[...], 0.0)
    buf_ref[CONV_HALO:CONV_HALO + tm, :] = cur_ref[...]
    n_buf = tm + CONV_HALO
    n_sh = n_buf - SUBLANES
    for c0 in range(0, D_CONV, LANES):
        x = buf_ref[:, c0:c0 + LANES]
        for b in range(1, SUBLANES):
            sh_ref[b - 1, :, c0:c0 + LANES] = pltpu.roll(x, n_buf - b, 0)[0:n_sh, :]
    off = CONV_HALO - (CONV_K - 1)
    for c0 in range(0, D_CONV, CONV_CB):
        cols = slice(c0, c0 + CONV_CB)
        w_all = wb_ref[:, :, cols]
        bias = jnp.broadcast_to(b_ref[:, cols], (CONV_RB, CONV_CB))

        def row_block(rb, carry, cols=cols, w_all=w_all, bias=bias):
            r0 = pl.multiple_of(rb * CONV_RB, CONV_RB)
            acc = bias
            for b in range(SUBLANES):
                taps = [k for k in range(CONV_K) if (off + k) % SUBLANES == b]
                a_lo, a_hi = (off + taps[0]) // SUBLANES, (off + taps[-1]) // SUBLANES
                rows = pl.ds(r0 + a_lo * SUBLANES, (a_hi - a_lo) * SUBLANES + CONV_RB)
                slab = buf_ref[rows, cols] if b == 0 else sh_ref[b - 1, rows, cols]
                for k in taps:
                    s0 = ((off + k) // SUBLANES - a_lo) * SUBLANES
                    wk = jnp.concatenate([w_all[k]] * (CONV_RB // SUBLANES), axis=0)
                    acc = acc + slab[s0:s0 + CONV_RB, :] * wk
            acc_ref[pl.ds(r0, CONV_RB), cols] = acc
            return carry

        lax.fori_loop(0, tm // CONV_RB, row_block, 0)
    y = acc_ref[...]
    mu = jnp.mean(y, axis=-1, keepdims=True)
    yc = y - mu
    yn = yc * lax.rsqrt(jnp.mean(yc * yc, axis=-1, keepdims=True) + EPS) * g_ref[...] + beta_ref[...]
    o_ref[...] = (yn * jax.nn.sigmoid(yn)).astype(o_ref.dtype)


def conv_ln_silu(hglu, conv_w, conv_b3, ln_g3, ln_b3, layer, tm=256):
    S, D = hglu.shape
    row = _layer_row(D, layer)
    return pl.pallas_call(
        functools.partial(_conv_kernel, tm=tm),
        grid=(S // tm,),
        in_specs=[pl.BlockSpec((tm, D), lambda i: (i, 0)),
                  pl.BlockSpec((CONV_HALO, D), lambda i: (jnp.maximum(i * (tm // CONV_HALO) - 1, 0), 0)),
                  pl.BlockSpec((None, CONV_K, D), lambda i: (layer, 0, 0)),
                  row, row, row],
        out_specs=pl.BlockSpec((tm, D), lambda i: (i, 0)),
        out_shape=jax.ShapeDtypeStruct((S, D), BF16),
        scratch_shapes=[pltpu.VMEM((CONV_HALO + tm, D), F32),
                        pltpu.VMEM((SUBLANES - 1, CONV_HALO + tm - SUBLANES, D), F32),
                        pltpu.VMEM((tm, D), F32),
                        pltpu.VMEM((CONV_K, SUBLANES, D), F32)],
        name="conv_ln_silu",
        compiler_params=_cparams("arbitrary"),
    )(hglu, hglu, conv_w, conv_b3, ln_g3, ln_b3)


def _memkv_kernel(mem_ref, g_ref, w_ref, k_ref, v_ref):
    mn = _rms(mem_ref[...], g_ref[...]).astype(BF16)
    kv = jnp.dot(mn, w_ref[...], preferred_element_type=F32)
    n = X_HEADS * X_HEAD_DIM
    k_ref[...] = kv[:, :n].astype(k_ref.dtype)
    v_ref[...] = kv[:, n:].astype(v_ref.dtype)


def mem_kv(mem, g3, w, layer):
    n = X_HEADS * X_HEAD_DIM
    D = mem.shape[1]
    full = lambda shape: pl.BlockSpec(shape, lambda i: (0, 0))
    return pl.pallas_call(
        _memkv_kernel,
        grid=(1,),
        in_specs=[full(mem.shape), _layer_row(D, layer),
                  pl.BlockSpec((None, D, 2 * n), lambda i: (layer, 0, 0))],
        out_specs=[full((N_MEM, n)), full((N_MEM, n))],
        out_shape=[jax.ShapeDtypeStruct((N_MEM, n), BF16)] * 2,
        name="mem_kv",
        compiler_params=_cparams("arbitrary"),
    )(mem, g3, w)


def _mix_cross_kernel(m_ref, x_ref, wo_ref, gmem_ref, wq_ref, k_ref, v_ref, wxo_ref, gffn_ref, xo_ref, ho_ref):
    x1 = x_ref[...] + jnp.dot(m_ref[...], wo_ref[...], preferred_element_type=F32)
    hm = _rms(x1, gmem_ref[...]).astype(BF16)
    q = jnp.dot(hm, wq_ref[...], preferred_element_type=F32) * (X_HEAD_DIM ** -0.5)
    q = q.astype(BF16)
    outs = []
    for h in range(X_HEADS):
        sl = slice(h * X_HEAD_DIM, (h + 1) * X_HEAD_DIM)
        s = lax.dot_general(q[:, sl], k_ref[:, sl], (((1,), (1,)), ((), ())), preferred_element_type=F32)
        p = jnp.exp(s - jnp.max(s, axis=-1, keepdims=True))
        l = jnp.sum(p, axis=-1, keepdims=True)
        o = jnp.dot(p.astype(BF16), v_ref[:, sl], preferred_element_type=F32) / l
        outs.append(o.astype(BF16))
    o = jnp.concatenate(outs, axis=-1)
    x2 = x1 + jnp.dot(o, wxo_ref[...], preferred_element_type=F32)
    xo_ref[...] = x2
    ho_ref[...] = _rms(x2, gffn_ref[...]).astype(ho_ref.dtype)


def mix_cross(merged, x, w_out, g_mem3, w_xq, mk, mv, w_xo, g_ffn3, layer, tm=512):
    M, D = merged.shape
    n = X_HEADS * X_HEAD_DIM
    const = lambda shape: pl.BlockSpec(shape, lambda i: (0, 0))
    rows = pl.BlockSpec((tm, D), lambda i: (i, 0))
    once = pl.Buffered(1)
    weight = lambda k, c: pl.BlockSpec((None, k, c), lambda i: (layer, 0, 0), pipeline_mode=once)
    return pl.pallas_call(
        _mix_cross_kernel,
        grid=(M // tm,),
        in_specs=[rows, rows, weight(D, D), _layer_row(D, layer), weight(D, n),
                  const((N_MEM, n)), const((N_MEM, n)), weight(n, D), _layer_row(D, layer)],
        out_specs=[rows, rows],
        out_shape=[jax.ShapeDtypeStruct((M, D), F32), jax.ShapeDtypeStruct((M, D), BF16)],
        name="mix_cross",
        compiler_params=_cparams("arbitrary"),
    )(merged, x, w_out, g_mem3, w_xq, mk, mv, w_xo, g_ffn3)


def _mlp_kernel(h_ref, w1_ref, w2_ref, x_ref, g_ref, xo_ref, ho_ref):
    f = pl.program_id(1)

    @pl.when(f == 0)
    def _():
        xo_ref[...] = x_ref[...]

    a = jnp.dot(h_ref[...], w1_ref[...], preferred_element_type=F32)
    a = jnp.square(jnp.maximum(a, 0.0)).astype(BF16)
    xo_ref[...] += jnp.dot(a, w2_ref[...], preferred_element_type=F32)

    @pl.when(f == pl.num_programs(1) - 1)
    def _():
        ho_ref[...] = _rms(xo_ref[...], g_ref[...]).astype(ho_ref.dtype)


def mlp(h, w1, w2, layer, x, g3, g_layer, out_dtype, tm=512, tf=1024):
    M, D = h.shape
    F = w1.shape[2]
    rows = pl.BlockSpec((tm, D), lambda i, f: (i, 0))
    return pl.pallas_call(
        _mlp_kernel,
        grid=(M // tm, F // tf),
        in_specs=[rows,
                  pl.BlockSpec((None, D, tf), lambda i, f: (layer, 0, f)),
                  pl.BlockSpec((None, tf, D), lambda i, f: (layer, f, 0)),
                  rows,
                  _layer_row(D, g_layer)],
        out_specs=[rows, rows],
        out_shape=[jax.ShapeDtypeStruct((M, D), F32), jax.ShapeDtypeStruct((M, D), out_dtype)],
        name="mlp",
        compiler_params=_cparams("arbitrary", "arbitrary"),
    )(h, w1, w2, x, g3)


def _permute_q_weight(w_uq):
    L, K, _ = w_uq.shape
    half = QK_ROPE // 2
    w = w_uq.reshape(L, K, MLA_HEADS, QK_NOPE + QK_ROPE)
    pe = w[..., QK_NOPE:]
    w = jnp.concatenate([w, pe[..., half:], pe[..., :half]], axis=-1)
    return w.reshape(L, K, MLA_HEADS * QK_PAD)


def kernel(x, mem, positions, norm_mix_g, w_in, b_gate, conv_w, conv_b, conv_ln_g, conv_ln_b, w_conv_out, q_norm_g, w_uq, kv_norm_g, w_ukv, w_mla_out, w_out, norm_mem_g, mem_norm_g, w_xq, w_xkv, w_xo, norm_ffn_g, w_ff1, w_ff2, final_norm_g):
    B, S, D = x.shape
    assert (B, S, D) == (1, SEQ, D_MODEL)
    x2d = x.reshape(S, D)
    mem2d = mem.reshape(N_MEM, D)
    vec3 = lambda v: v.reshape(v.shape[0], 1, v.shape[1])

    inv_freq = 1.0 / (ROPE_THETA ** (jnp.arange(0, QK_ROPE, 2, dtype=F32) / QK_ROPE))
    cos, sin = rope_tables(positions.reshape(S, 1), jnp.tile(inv_freq, 4).reshape(1, LANES))

    w_in_lo, w_gate = cast_w_in(w_in)
    w_q = _permute_q_weight(w_uq).astype(BF16)
    w_kv, w_co, w_mo, w_o = (w.astype(BF16) for w in (w_ukv, w_conv_out, w_mla_out, w_out))
    w_q_x, w_kv_x, w_o_x = (w.astype(BF16) for w in (w_xq, w_xkv, w_xo))
    w_1, w_2 = w_ff1.astype(BF16), w_ff2.astype(BF16)

    g_mix, g_mem, g_memn, g_ffn = vec3(norm_mix_g), vec3(norm_mem_g), vec3(mem_norm_g), vec3(norm_ffn_g)
    g_q, g_kv, b_g = vec3(q_norm_g), vec3(kv_norm_g), vec3(b_gate)
    c_b, ln_g, ln_b = vec3(conv_b), vec3(conv_ln_g), vec3(conv_ln_b)
    g_final = final_norm_g.reshape(1, 1, D)

    h = rms_norm_rows(x2d, g_mix, 0, BF16)
    xcur = x2d
    for l in range(DEPTH):
        (hglu,) = matmul("glu_proj", [h], [(0, w_in_lo, l, COL_GLU_A), (0, w_in_lo, l, COL_GLU_G)], [],
                         [F32], _epi_glu, D_CONV, 1024, 1024)

        hc = conv_ln_silu(hglu, conv_w, c_b, ln_g, ln_b, l)
        (gy,) = matmul("conv_out_gated", [hc, h], [(0, w_co, l, 0), (1, w_gate, l, 0)],
                       [("row", b_g, l, 0)], [BF16], _epi_gated, D, 1024, 512)

        q, k, v = mla_proj(h, w_in_lo, w_q, w_kv, l, g_q, g_kv, cos, sin)
        o = mla_attention(q, k, v)
        (merged,) = matmul("mla_out_gated_merge", [o, h], [(0, w_mo, l, 0), (1, w_gate, l, D)],
                           [("row", b_g, l, D), ("tile", gy, 0)], [BF16], _epi_gated_merge, D, 1024, 512)

        mk, mv = mem_kv(mem2d, g_memn, w_kv_x, l)
        xcur, hf = mix_cross(merged, xcur, w_o, g_mem, w_q_x, mk, mv, w_o_x, g_ffn, l)

        last = l == DEPTH - 1
        xcur, h = mlp(hf, w_1, w_2, l, xcur, g_final if last else g_mix, 0 if last else l + 1,
                      F32 if last else BF16)
    return h.reshape(B, S, D)
```

```python
import functools
import math

import jax
import jax.numpy as jnp
from jax import lax
from jax.experimental import pallas as pl
from jax.experimental.pallas import tpu as pltpu

D_MODEL = 2048
SEQ = 8192
DEPTH = 2
CHUNK = 64
N_MEM = 256
EPS = 1e-6
D_CONV = D_MODEL
CONV_K = 31
MLA_HEADS = 16
Q_LORA = 512
KV_LORA = 512
QK_NOPE = 128
QK_ROPE = 64
V_DIM = 128
ROPE_THETA = 10000.0
X_HEADS = 4
X_HEAD_DIM = 128
D_FF = 4 * D_MODEL

LANES = 128
SUBLANES = 8
QK_PAD = 256
CONV_HALO = 32
NEG_BIG = -1e30
VMEM_LIMIT = 56 * 1024 * 1024

COL_GLU_A = 0
COL_GLU_G = D_CONV
COL_LATENT = 2 * D_CONV
COL_KPE = COL_LATENT + Q_LORA + KV_LORA
COL_GATE = COL_KPE + QK_ROPE

F32 = jnp.float32
BF16 = jnp.bfloat16


def _cparams(*sem):
    return pltpu.CompilerParams(dimension_semantics=sem, vmem_limit_bytes=VMEM_LIMIT)


def _rms(x, g):
    return x * lax.rsqrt(jnp.mean(x * x, axis=-1, keepdims=True) + EPS) * g


def _layer_row(n, layer, col0=0):
    return pl.BlockSpec((None, 1, n), lambda *_: (layer, 0, col0 // n))


def _rope_table_kernel(pos_ref, freq_ref, cos_ref, sin_ref):
    ang = pos_ref[...].astype(F32) * freq_ref[...]
    lane = lax.broadcasted_iota(jnp.int32, ang.shape, 1)
    s = jnp.sin(ang)
    cos_ref[...] = jnp.where(lane < QK_ROPE, jnp.cos(ang), 0.0)
    sin_ref[...] = jnp.where(lane < QK_ROPE // 2, -s, jnp.where(lane < QK_ROPE, s, 0.0))


def rope_tables(pos_col, freq_row):
    S = pos_col.shape[0]
    tm = 1024
    tab = pl.BlockSpec((tm, LANES), lambda i: (i, 0))
    return pl.pallas_call(
        _rope_table_kernel,
        grid=(S // tm,),
        in_specs=[pl.BlockSpec((tm, 1), lambda i: (i, 0)),
                  pl.BlockSpec((1, LANES), lambda i: (0, 0))],
        out_specs=[tab, tab],
        out_shape=[jax.ShapeDtypeStruct((S, LANES), F32)] * 2,
        name="rope_table",
        compiler_params=_cparams("arbitrary"),
    )(pos_col, freq_row)


def _norm_kernel(x_ref, g_ref, o_ref):
    o_ref[...] = _rms(x_ref[...], g_ref[...]).astype(o_ref.dtype)


def rms_norm_rows(x, g3, layer, out_dtype, tm=512):
    M, D = x.shape
    return pl.pallas_call(
        _norm_kernel,
        grid=(M // tm,),
        in_specs=[pl.BlockSpec((tm, D), lambda i: (i, 0)), _layer_row(D, layer)],
        out_specs=pl.BlockSpec((tm, D), lambda i: (i, 0)),
        out_shape=jax.ShapeDtypeStruct((M, D), out_dtype),
        name="rms_norm",
        compiler_params=_cparams("arbitrary"),
    )(x, g3)


def _dot_nt(a, w_t):
    return lax.dot_general(a, w_t, (((1,), (1,)), ((), ())), preferred_element_type=F32)


def _mm_kernel(*refs, a_index, w_transposed, n_a, n_extra, epilogue):
    n_b = len(a_index)
    acts = [r[...] for r in refs[:n_a]]
    accs = [_dot_nt(acts[ai], b[...]) if t else jnp.dot(acts[ai], b[...], preferred_element_type=F32)
            for ai, t, b in zip(a_index, w_transposed, refs[n_a:n_a + n_b])]
    extras = [e[...] for e in refs[n_a + n_b:n_a + n_b + n_extra]]
    outs = epilogue(accs, extras)
    for o_ref, o in zip(refs[n_a + n_b + n_extra:], outs):
        o_ref[...] = o.astype(o_ref.dtype)


def matmul(name, acts, weights, extras, out_dtypes, epilogue, n_out, tm, tn):
    M = acts[0].shape[0]
    in_specs = [pl.BlockSpec((tm, a.shape[1]), lambda i, j: (i, 0)) for a in acts]
    operands = list(acts)
    for _, w, layer, col0, transposed in weights:
        if transposed:
            in_specs.append(pl.BlockSpec((None, tn, w.shape[2]),
                                         lambda i, j, layer=layer, off=col0 // tn: (layer, j + off, 0)))
        else:
            in_specs.append(pl.BlockSpec((None, w.shape[1], tn),
                                         lambda i, j, layer=layer, off=col0 // tn: (layer, 0, j + off)))
        operands.append(w)
    for e in extras:
        if e[0] == "row":
            _, arr, layer, col0 = e
            in_specs.append(pl.BlockSpec((None, 1, tn),
                                         lambda i, j, layer=layer, off=col0 // tn: (layer, 0, j + off)))
        else:
            _, arr, col0 = e
            in_specs.append(pl.BlockSpec((tm, tn), lambda i, j, off=col0 // tn: (i, j + off)))
        operands.append(arr)
    return pl.pallas_call(
        functools.partial(_mm_kernel, a_index=tuple(w[0] for w in weights),
                          w_transposed=tuple(w[4] for w in weights), n_a=len(acts),
                          n_extra=len(extras), epilogue=epilogue),
        grid=(M // tm, n_out // tn),
        in_specs=in_specs,
        out_specs=[pl.BlockSpec((tm, tn), lambda i, j: (i, j)) for _ in out_dtypes],
        out_shape=[jax.ShapeDtypeStruct((M, n_out), dt) for dt in out_dtypes],
        name=name,
        compiler_params=_cparams("arbitrary", "arbitrary"),
    )(*operands)


def _epi_glu(accs, extras):
    return [accs[0] * jax.nn.sigmoid(accs[1])]


def _epi_gated(accs, extras):
    return [accs[0] * jax.nn.sigmoid(accs[1] + extras[0])]


def _epi_gated_merge(accs, extras):
    return [extras[1] + accs[0] * jax.nn.sigmoid(accs[1] + extras[0])]


def _mla_proj_kernel(h_ref, wlat_ref, wpe_ref, wq_ref, wkv_ref, qg_ref, kvg_ref, cos_ref, sin_ref,
                     q_ref, k_ref, v_ref):
    h = h_ref[...]
    cos, sin = cos_ref[...], sin_ref[...]
    z = _dot_nt(h, wlat_ref[...])
    cq = _rms(z[:, :Q_LORA], qg_ref[...]).astype(BF16)
    ckv = _rms(z[:, Q_LORA:], kvg_ref[...]).astype(BF16)
    pe = _dot_nt(h, wpe_ref[...])
    half = QK_ROPE // 2
    lane = lax.broadcasted_iota(jnp.int32, pe.shape, 1)
    swapped = jnp.where(lane < half, pltpu.roll(pe, LANES - half, 1), pltpu.roll(pe, half, 1))
    kpe = jnp.where(lane < QK_ROPE, pe * cos + swapped * sin, 0.0).astype(k_ref.dtype)

    scale = (QK_NOPE + QK_ROPE) ** -0.5 * math.log2(math.e)
    kv_width = QK_NOPE + V_DIM
    for hd in range(MLA_HEADS):
        blk = jnp.dot(cq, wq_ref[:, hd * QK_PAD:(hd + 1) * QK_PAD], preferred_element_type=F32)
        q_ref[hd, :, 0:QK_NOPE] = (blk[:, :QK_NOPE] * scale).astype(q_ref.dtype)
        qpe = blk[:, QK_NOPE:]
        roped = qpe * cos + pltpu.roll(qpe, QK_ROPE, 1) * sin
        q_ref[hd, :, QK_NOPE:QK_PAD] = (roped * scale).astype(q_ref.dtype)

        blk = jnp.dot(ckv, wkv_ref[:, hd * kv_width:(hd + 1) * kv_width], preferred_element_type=F32)
        k_ref[hd, :, 0:QK_NOPE] = blk[:, :QK_NOPE].astype(k_ref.dtype)
        k_ref[hd, :, QK_NOPE:QK_PAD] = kpe
        v_ref[:, hd * V_DIM:(hd + 1) * V_DIM] = blk[:, QK_NOPE:].astype(v_ref.dtype)


def mla_proj(h, w_in_lo, w_q, w_kv, layer, qg3, kvg3, cos, sin, tm=512):
    M, K = h.shape
    n_lat = Q_LORA + KV_LORA
    rows = lambda n: pl.BlockSpec((tm, n), lambda i: (i, 0))
    once = pl.Buffered(1)
    heads = pl.BlockSpec((MLA_HEADS, tm, QK_PAD), lambda i: (0, i, 0))
    return pl.pallas_call(
        _mla_proj_kernel,
        grid=(M // tm,),
        in_specs=[rows(K),
                  pl.BlockSpec((None, n_lat, K), lambda i: (layer, COL_LATENT // n_lat, 0), pipeline_mode=once),
                  pl.BlockSpec((None, LANES, K), lambda i: (layer, COL_KPE // LANES, 0), pipeline_mode=once),
                  pl.BlockSpec((None, Q_LORA, MLA_HEADS * QK_PAD), lambda i: (layer, 0, 0), pipeline_mode=once),
                  pl.BlockSpec((None, KV_LORA, MLA_HEADS * (QK_NOPE + V_DIM)), lambda i: (layer, 0, 0),
                               pipeline_mode=once),
                  _layer_row(Q_LORA, layer), _layer_row(KV_LORA, layer),
                  rows(LANES), rows(LANES)],
        out_specs=[heads, heads, rows(MLA_HEADS * V_DIM)],
        out_shape=[jax.ShapeDtypeStruct((MLA_HEADS, M, QK_PAD), BF16),
                   jax.ShapeDtypeStruct((MLA_HEADS, M, QK_PAD), BF16),
                   jax.ShapeDtypeStruct((M, MLA_HEADS * V_DIM), BF16)],
        name="mla_proj",
        compiler_params=_cparams("arbitrary"),
    )(h, w_in_lo, w_in_lo, w_q, w_kv, qg3, kvg3, cos, sin)


ATTN_RB = CHUNK
ATTN_UNROLLS = (8, 4, 2)


def _attn_kernel(q_ref, k_ref, v_ref, o_ref, vone_ref, *set_refs, tq):
    vone_ref[:, 0:V_DIM] = v_ref[...]
    vone_ref[:, V_DIM:] = jnp.ones((vone_ref.shape[0], V_DIM), vone_ref.dtype)
    n_set = len(set_refs) // 2
    sets = [_AttnBlock(q_ref, k_ref, vone_ref, o_ref, set_refs[i * n_set:(i + 1) * n_set], tq, i)
            for i in range(2)]
    n_q = q_ref.shape[1] // tq
    assert n_q % 2 == 0

    sets[0].first_block()
    sets[1].fill(jnp.int32(1))

    def query_block_pair(qq, carry):
        qa, qb = 2 * qq, 2 * qq + 1
        sets[1].drain(qa - 1)
        sets[0].fill(qa)
        sets[0].steady(qa)
        sets[0].drain(qa)
        sets[1].fill(qb)
        sets[1].steady(qb)
        return carry

    lax.fori_loop(1, n_q // 2, query_block_pair, 0)
    sets[1].drain(jnp.int32(n_q - 1))


class _AttnBlock:
    def __init__(self, q_ref, k_ref, v_ref, o_ref, refs, tq, par):
        s0, s1, p0, p1, a0, a1, self.m_ref, self.acc_ref = refs
        self.s_refs, self.p_refs, self.a_refs = (s0, s1), (p0, p1), (a0, a1)
        self.q_ref, self.k_ref, self.v_ref, self.o_ref = q_ref, k_ref, v_ref, o_ref
        self.tq, self.par = tq, par

    def _q_rows(self, qi):
        return pl.ds(pl.multiple_of(qi * self.tq, self.tq), self.tq)

    def _init(self):
        self.m_ref[...] = jnp.full(self.m_ref.shape, NEG_BIG, F32)
        self.acc_ref[...] = jnp.zeros(self.acc_ref.shape, F32)

    def scores(self, qi, j, par):
        start = pl.multiple_of(j * self.tq, self.tq)
        kb = self.k_ref[0, pl.ds(start, self.tq), :]
        self.s_refs[par][...] = lax.dot_general(self.q_ref[0, self._q_rows(qi), :], kb,
                                                (((1,), (1,)), ((), ())), preferred_element_type=F32)

    def softmax(self, par, masked):
        s_ref, p_ref, a_ref, m_ref = self.s_refs[par], self.p_refs[par], self.a_refs[par], self.m_ref
        for r in range(self.tq // ATTN_RB):
            rows = slice(r * ATTN_RB, (r + 1) * ATTN_RB)
            cols = []
            for c in range(self.tq // LANES):
                first_chunk = (c * LANES) // CHUNK
                if masked and first_chunk > r:
                    cols.append(None)
                    continue
                sc = s_ref[rows, c * LANES:(c + 1) * LANES]
                if masked and first_chunk == r:
                    lane = lax.broadcasted_iota(jnp.int32, sc.shape, 1)
                    sc = jnp.where(lane < CHUNK, sc, NEG_BIG)
                cols.append(sc)
            mx = functools.reduce(jnp.maximum, [sc for sc in cols if sc is not None])
            m_prev = m_ref[rows, :]
            m_new = jnp.maximum(m_prev, jnp.max(mx, axis=-1, keepdims=True))
            m_ref[rows, :] = m_new
            a_ref[rows, :] = jnp.exp2(m_prev - m_new)
            for c, sc in enumerate(cols):
                pc = jnp.zeros((ATTN_RB, LANES), BF16) if sc is None else jnp.exp2(sc - m_new).astype(BF16)
                p_ref[rows, c * LANES:(c + 1) * LANES] = pc

    def values(self, j, par):
        start = pl.multiple_of(j * self.tq, self.tq)
        vb = self.v_ref[pl.ds(start, self.tq), :]
        alpha = self.a_refs[par][...]
        self.acc_ref[...] = (jnp.concatenate([alpha, alpha], axis=1) * self.acc_ref[...]
                             + jnp.dot(self.p_refs[par][...], vb, preferred_element_type=F32))

    def _write(self, qi):
        acc = self.acc_ref
        self.o_ref[self._q_rows(qi), :] = (acc[:, 0:V_DIM] / acc[:, V_DIM:]).astype(self.o_ref.dtype)

    def first_block(self):
        qi = jnp.int32(0)
        self._init()
        self.scores(qi, qi, 0)
        self.softmax(0, True)
        self.values(qi, 0)
        self._write(qi)

    def fill(self, qi):
        self._init()
        self.scores(qi, 0, 0)
        self.scores(qi, 1, 1)
        self.softmax(0, False)

    def _tick(self, qi, t, par):
        self.scores(qi, t, par)
        self.softmax(1 - par, False)
        self.values(t - 2, par)

    def steady(self, qi):
        n_left = jnp.maximum(qi - 1, 0)
        t0 = 2
        for unroll in ATTN_UNROLLS:

            def unrolled(i, carry, t0=t0, unroll=unroll):
                for u in range(unroll):
                    self._tick(qi, t0 + unroll * i + u, u % 2)
                return carry

            lax.fori_loop(0, n_left // unroll, unrolled, 0)
            t0 = t0 + (n_left // unroll) * unroll
            n_left = n_left % unroll

        @pl.when(n_left == 1)
        def _():
            self._tick(qi, qi, 0)

    def drain(self, qi):
        self.softmax(self.par, True)
        self.values(qi - 1, 1 - self.par)
        self.values(qi, self.par)
        self._write(qi)


def mla_attention(q, k, v, tq=512):
    H, S, _ = q.shape
    buffer_set = ([pltpu.VMEM((tq, tq), F32)] * 2
                  + [pltpu.VMEM((tq, tq), BF16)] * 2
                  + [pltpu.VMEM((tq, LANES), F32)] * 2
                  + [pltpu.VMEM((tq, LANES), F32),
                     pltpu.VMEM((tq, 2 * V_DIM), F32)])
    return pl.pallas_call(
        functools.partial(_attn_kernel, tq=tq),
        grid=(H,),
        in_specs=[pl.BlockSpec((1, S, QK_PAD), lambda h: (h, 0, 0)),
                  pl.BlockSpec((1, S, QK_PAD), lambda h: (h, 0, 0)),
                  pl.BlockSpec((S, V_DIM), lambda h: (0, h))],
        out_specs=pl.BlockSpec((S, V_DIM), lambda h: (0, h)),
        out_shape=jax.ShapeDtypeStruct((S, H * V_DIM), BF16),
        scratch_shapes=[pltpu.VMEM((S, 2 * V_DIM), BF16)] + buffer_set + buffer_set,
        name="mla_attention",
        compiler_params=_cparams("arbitrary"),
    )(q, k, v)


CONV_RB = 64
CONV_CB = LANES


def _conv_kernel(cur_ref, halo_ref, w_ref, b_ref, g_ref, beta_ref, o_ref, buf_ref, sh_ref, acc_ref, wb_ref, *, tm):
    i = pl.program_id(0)

    @pl.when(i == 0)
    def _():
        for k in range(CONV_K):
            wb_ref[k] = jnp.broadcast_to(w_ref[k:k + 1, :], (SUBLANES, D_CONV))

    buf_ref[0:CONV_HALO, :] = jnp.where(i > 0, halo_ref[...], 0.0)
    buf_ref[CONV_HALO:CONV_HALO + tm, :] = cur_ref[...]
    n_buf = tm + CONV_HALO
    n_sh = n_buf - SUBLANES
    for c0 in range(0, D_CONV, LANES):
        x = buf_ref[:, c0:c0 + LANES]
        for b in range(1, SUBLANES):
            sh_ref[b - 1, :, c0:c0 + LANES] = pltpu.roll(x, n_buf - b, 0)[0:n_sh, :]
    off = CONV_HALO - (CONV_K - 1)
    for c0 in range(0, D_CONV, CONV_CB):
        cols = slice(c0, c0 + CONV_CB)
        w_all = wb_ref[:, :, cols]
        bias = jnp.broadcast_to(b_ref[:, cols], (CONV_RB, CONV_CB))

        def row_block(rb, carry, cols=cols, w_all=w_all, bias=bias):
            r0 = pl.multiple_of(rb * CONV_RB, CONV_RB)
            acc = bias
            for b in range(SUBLANES):
                taps = [k for k in range(CONV_K) if (off + k) % SUBLANES == b]
                a_lo, a_hi = (off + taps[0]) // SUBLANES, (off + taps[-1]) // SUBLANES
                rows = pl.ds(r0 + a_lo * SUBLANES, (a_hi - a_lo) * SUBLANES + CONV_RB)
                slab = buf_ref[rows, cols] if b == 0 else sh_ref[b - 1, rows, cols]
                for k in taps:
                    s0 = ((off + k) // SUBLANES - a_lo) * SUBLANES
                    wk = jnp.concatenate([w_all[k]] * (CONV_RB // SUBLANES), axis=0)
                    acc = acc + slab[s0:s0 + CONV_RB, :] * wk
            acc_ref[pl.ds(r0, CONV_RB), cols] = acc
            return carry

        lax.fori_loop(0, tm // CONV_RB, row_block, 0)
    y = acc_ref[...]
    mu = jnp.mean(y, axis=-1, keepdims=True)
    yc = y - mu
    yn = yc * lax.rsqrt(jnp.mean(yc * yc, axis=-1, keepdims=True) + EPS) * g_ref[...] + beta_ref[...]
    o_ref[...] = (yn * jax.nn.sigmoid(yn)).astype(o_ref.dtype)


def conv_ln_silu(hglu, conv_w, conv_b3, ln_g3, ln_b3, layer, tm=256):
    S, D = hglu.shape
    row = _layer_row(D, layer)
    return pl.pallas_call(
        functools.partial(_conv_kernel, tm=tm),
        grid=(S // tm,),
        in_specs=[pl.BlockSpec((tm, D), lambda i: (i, 0)),
                  pl.BlockSpec((CONV_HALO, D), lambda i: (jnp.maximum(i * (tm // CONV_HALO) - 1, 0), 0)),
                  pl.BlockSpec((None, CONV_K, D), lambda i: (layer, 0, 0)),
                  row, row, row],
        out_specs=pl.BlockSpec((tm, D), lambda i: (i, 0)),
        out_shape=jax.ShapeDtypeStruct((S, D), BF16),
        scratch_shapes=[pltpu.VMEM((CONV_HALO + tm, D), F32),
                        pltpu.VMEM((SUBLANES - 1, CONV_HALO + tm - SUBLANES, D), F32),
                        pltpu.VMEM((tm, D), F32),
                        pltpu.VMEM((CONV_K, SUBLANES, D), F32)],
        name="conv_ln_silu",
        compiler_params=_cparams("arbitrary"),
    )(hglu, hglu, conv_w, conv_b3, ln_g3, ln_b3)


def _memkv_kernel(mem_ref, g_ref, w_ref, k_ref, v_ref):
    mn = _rms(mem_ref[...], g_ref[...]).astype(BF16)
    kv = jnp.dot(mn, w_ref[...], preferred_element_type=F32)
    n = X_HEADS * X_HEAD_DIM
    k_ref[...] = kv[:, :n].astype(k_ref.dtype)
    v_ref[...] = kv[:, n:].astype(v_ref.dtype)


def mem_kv(mem, g3, w, layer):
    n = X_HEADS * X_HEAD_DIM
    D = mem.shape[1]
    full = lambda shape: pl.BlockSpec(shape, lambda i: (0, 0))
    return pl.pallas_call(
        _memkv_kernel,
        grid=(1,),
        in_specs=[full(mem.shape), _layer_row(D, layer),
                  pl.BlockSpec((None, D, 2 * n), lambda i: (layer, 0, 0))],
        out_specs=[full((N_MEM, n)), full((N_MEM, n))],
        out_shape=[jax.ShapeDtypeStruct((N_MEM, n), BF16)] * 2,
        name="mem_kv",
        compiler_params=_cparams("arbitrary"),
    )(mem, g3, w)


def _mix_cross_kernel(m_ref, x_ref, wo_ref, gmem_ref, wq_ref, k_ref, v_ref, wxo_ref, gffn_ref, xo_ref, ho_ref):
    x1 = x_ref[...] + jnp.dot(m_ref[...], wo_ref[...], preferred_element_type=F32)
    hm = _rms(x1, gmem_ref[...]).astype(BF16)
    q = jnp.dot(hm, wq_ref[...], preferred_element_type=F32) * (X_HEAD_DIM ** -0.5)
    q = q.astype(BF16)
    outs = []
    for h in range(X_HEADS):
        sl = slice(h * X_HEAD_DIM, (h + 1) * X_HEAD_DIM)
        s = lax.dot_general(q[:, sl], k_ref[:, sl], (((1,), (1,)), ((), ())), preferred_element_type=F32)
        p = jnp.exp(s - jnp.max(s, axis=-1, keepdims=True))
        l = jnp.sum(p, axis=-1, keepdims=True)
        o = jnp.dot(p.astype(BF16), v_ref[:, sl], preferred_element_type=F32) / l
        outs.append(o.astype(BF16))
    o = jnp.concatenate(outs, axis=-1)
    x2 = x1 + jnp.dot(o, wxo_ref[...], preferred_element_type=F32)
    xo_ref[...] = x2
    ho_ref[...] = _rms(x2, gffn_ref[...]).astype(ho_ref.dtype)


def mix_cross(merged, x, w_out, g_mem3, w_xq, mk, mv, w_xo, g_ffn3, layer, tm=512):
    M, D = merged.shape
    n = X_HEADS * X_HEAD_DIM
    const = lambda shape: pl.BlockSpec(shape, lambda i: (0, 0))
    rows = pl.BlockSpec((tm, D), lambda i: (i, 0))
    once = pl.Buffered(1)
    weight = lambda k, c: pl.BlockSpec((None, k, c), lambda i: (layer, 0, 0), pipeline_mode=once)
    return pl.pallas_call(
        _mix_cross_kernel,
        grid=(M // tm,),
        in_specs=[rows, rows, weight(D, D), _layer_row(D, layer), weight(D, n),
                  const((N_MEM, n)), const((N_MEM, n)), weight(n, D), _layer_row(D, layer)],
        out_specs=[rows, rows],
        out_shape=[jax.ShapeDtypeStruct((M, D), F32), jax.ShapeDtypeStruct((M, D), BF16)],
        name="mix_cross",
        compiler_params=_cparams("arbitrary"),
    )(merged, x, w_out, g_mem3, w_xq, mk, mv, w_xo, g_ffn3)


def _mlp_kernel(h_ref, w1_ref, w2_ref, x_ref, g_ref, xo_ref, ho_ref):
    f = pl.program_id(1)

    @pl.when(f == 0)
    def _():
        xo_ref[...] = x_ref[...]

    a = jnp.dot(h_ref[...], w1_ref[...], preferred_element_type=F32)
    a = jnp.square(jnp.maximum(a, 0.0)).astype(BF16)
    xo_ref[...] += jnp.dot(a, w2_ref[...], preferred_element_type=F32)

    @pl.when(f == pl.num_programs(1) - 1)
    def _():
        ho_ref[...] = _rms(xo_ref[...], g_ref[...]).astype(ho_ref.dtype)


def mlp(h, w1, w2, layer, x, g3, g_layer, out_dtype, tm=512, tf=1024):
    M, D = h.shape
    F = w1.shape[2]
    rows = pl.BlockSpec((tm, D), lambda i, f: (i, 0))
    return pl.pallas_call(
        _mlp_kernel,
        grid=(M // tm, F // tf),
        in_specs=[rows,
                  pl.BlockSpec((None, D, tf), lambda i, f: (layer, 0, f)),
                  pl.BlockSpec((None, tf, D), lambda i, f: (layer, f, 0)),
                  rows,
                  _layer_row(D, g_layer)],
        out_specs=[rows, rows],
        out_shape=[jax.ShapeDtypeStruct((M, D), F32), jax.ShapeDtypeStruct((M, D), out_dtype)],
        name="mlp",
        compiler_params=_cparams("arbitrary", "arbitrary"),
    )(h, w1, w2, x, g3)


def _permute_q_weight(w_uq):
    L, K, _ = w_uq.shape
    half = QK_ROPE // 2
    w = w_uq.reshape(L, K, MLA_HEADS, QK_NOPE + QK_ROPE)
    pe = w[..., QK_NOPE:]
    w = jnp.concatenate([w, pe[..., half:], pe[..., :half]], axis=-1)
    return w.reshape(L, K, MLA_HEADS * QK_PAD)


def kernel(x, mem, positions, norm_mix_g, w_in, b_gate, conv_w, conv_b, conv_ln_g, conv_ln_b, w_conv_out, q_norm_g, w_uq, kv_norm_g, w_ukv, w_mla_out, w_out, norm_mem_g, mem_norm_g, w_xq, w_xkv, w_xo, norm_ffn_g, w_ff1, w_ff2, final_norm_g):
    B, S, D = x.shape
    assert (B, S, D) == (1, SEQ, D_MODEL)
    x2d = x.reshape(S, D)
    mem2d = mem.reshape(N_MEM, D)
    vec3 = lambda v: v.reshape(v.shape[0], 1, v.shape[1])

    inv_freq = 1.0 / (ROPE_THETA ** (jnp.arange(0, QK_ROPE, 2, dtype=F32) / QK_ROPE))
    cos, sin = rope_tables(positions.reshape(S, 1), jnp.tile(inv_freq, 4).reshape(1, LANES))

    w_in_t = jnp.swapaxes(w_in, 1, 2)
    w_in_lo = w_in_t[:, :COL_KPE + LANES, :].astype(BF16)
    w_gate = w_in_t[:, COL_GATE:, :].astype(BF16)
    w_q = _permute_q_weight(w_uq).astype(BF16)
    w_kv, w_co, w_mo, w_o = (w.astype(BF16) for w in (w_ukv, w_conv_out, w_mla_out, w_out))
    w_q_x, w_kv_x, w_o_x = (w.astype(BF16) for w in (w_xq, w_xkv, w_xo))
    w_1, w_2 = w_ff1.astype(BF16), w_ff2.astype(BF16)

    g_mix, g_mem, g_memn, g_ffn = vec3(norm_mix_g), vec3(norm_mem_g), vec3(mem_norm_g), vec3(norm_ffn_g)
    g_q, g_kv, b_g = vec3(q_norm_g), vec3(kv_norm_g), vec3(b_gate)
    c_b, ln_g, ln_b = vec3(conv_b), vec3(conv_ln_g), vec3(conv_ln_b)
    g_final = final_norm_g.reshape(1, 1, D)

    h = rms_norm_rows(x2d, g_mix, 0, BF16)
    xcur = x2d
    for l in range(DEPTH):
        (hglu,) = matmul("glu_proj", [h], [(0, w_in_lo, l, COL_GLU_A, True), (0, w_in_lo, l, COL_GLU_G, True)],
                         [], [F32], _epi_glu, D_CONV, 1024, 1024)

        hc = conv_ln_silu(hglu, conv_w, c_b, ln_g, ln_b, l)
        (gy,) = matmul("conv_out_gated", [hc, h], [(0, w_co, l, 0, False), (1, w_gate, l, 0, True)],
                       [("row", b_g, l, 0)], [BF16], _epi_gated, D, 1024, 512)

        q, k, v = mla_proj(h, w_in_lo, w_q, w_kv, l, g_q, g_kv, cos, sin)
        o = mla_attention(q, k, v)
        (merged,) = matmul("mla_out_gated_merge", [o, h], [(0, w_mo, l, 0, False), (1, w_gate, l, D, True)],
                           [("row", b_g, l, D), ("tile", gy, 0)], [BF16], _epi_gated_merge, D, 1024, 512)

        mk, mv = mem_kv(mem2d, g_memn, w_kv_x, l)
        xcur, hf = mix_cross(merged, xcur, w_o, g_mem, w_q_x, mk, mv, w_o_x, g_ffn, l)

        last = l == DEPTH - 1
        xcur, h = mlp(hf, w_1, w_2, l, xcur, g_final if last else g_mix, 0 if last else l + 1,
                      F32 if last else BF16)
    return h.reshape(B, S, D)
```

```python
import functools
import math

import jax
import jax.numpy as jnp
from jax import lax
from jax.experimental import pallas as pl
from jax.experimental.pallas import tpu as pltpu

D_MODEL = 2048
SEQ = 8192
DEPTH = 2
CHUNK = 64
N_MEM = 256
EPS = 1e-6
D_CONV = D_MODEL
CONV_K = 31
MLA_HEADS = 16
Q_LORA = 512
KV_LORA = 512
QK_NOPE = 128
QK_ROPE = 64
V_DIM = 128
ROPE_THETA = 10000.0
X_HEADS = 4
X_HEAD_DIM = 128
D_FF = 4 * D_MODEL

LANES = 128
SUBLANES = 8
QK_PAD = 256
CONV_HALO = 32
NEG_BIG = -1e30
VMEM_LIMIT = 56 * 1024 * 1024

COL_GLU_A = 0
COL_GLU_G = D_CONV
COL_LATENT = 2 * D_CONV
COL_KPE = COL_LATENT + Q_LORA + KV_LORA
COL_GATE = COL_KPE + QK_ROPE

F32 = jnp.float32
BF16 = jnp.bfloat16


def _cparams(*sem):
    return pltpu.CompilerParams(dimension_semantics=sem, vmem_limit_bytes=VMEM_LIMIT)


def _rms(x, g):
    return x * lax.rsqrt(jnp.mean(x * x, axis=-1, keepdims=True) + EPS) * g


def _layer_row(n, layer, col0=0):
    return pl.BlockSpec((None, 1, n), lambda *_: (layer, 0, col0 // n))


def _rope_table_kernel(pos_ref, freq_ref, cos_ref, sin_ref):
    ang = pos_ref[...].astype(F32) * freq_ref[...]
    lane = lax.broadcasted_iota(jnp.int32, ang.shape, 1)
    s = jnp.sin(ang)
    cos_ref[...] = jnp.where(lane < QK_ROPE, jnp.cos(ang), 0.0)
    sin_ref[...] = jnp.where(lane < QK_ROPE // 2, -s, jnp.where(lane < QK_ROPE, s, 0.0))


def rope_tables(pos_col, freq_row):
    S = pos_col.shape[0]
    tm = 1024
    tab = pl.BlockSpec((tm, LANES), lambda i: (i, 0))
    return pl.pallas_call(
        _rope_table_kernel,
        grid=(S // tm,),
        in_specs=[pl.BlockSpec((tm, 1), lambda i: (i, 0)),
                  pl.BlockSpec((1, LANES), lambda i: (0, 0))],
        out_specs=[tab, tab],
        out_shape=[jax.ShapeDtypeStruct((S, LANES), F32)] * 2,
        name="rope_table",
        compiler_params=_cparams("arbitrary"),
    )(pos_col, freq_row)


def _norm_kernel(x_ref, g_ref, o_ref):
    o_ref[...] = _rms(x_ref[...], g_ref[...]).astype(o_ref.dtype)


def rms_norm_rows(x, g3, layer, out_dtype, tm=512):
    M, D = x.shape
    return pl.pallas_call(
        _norm_kernel,
        grid=(M // tm,),
        in_specs=[pl.BlockSpec((tm, D), lambda i: (i, 0)), _layer_row(D, layer)],
        out_specs=pl.BlockSpec((tm, D), lambda i: (i, 0)),
        out_shape=jax.ShapeDtypeStruct((M, D), out_dtype),
        name="rms_norm",
        compiler_params=_cparams("arbitrary"),
    )(x, g3)


def _dot_nt(a, w_t):
    return lax.dot_general(a, w_t, (((1,), (1,)), ((), ())), preferred_element_type=F32)


def _mm_kernel(*refs, a_index, w_transposed, n_a, n_extra, epilogue):
    n_b = len(a_index)
    acts = [r[...] for r in refs[:n_a]]
    accs = [_dot_nt(acts[ai], b[0]) if t else jnp.dot(acts[ai], b[...], preferred_element_type=F32)
            for ai, t, b in zip(a_index, w_transposed, refs[n_a:n_a + n_b])]
    extras = [e[...] for e in refs[n_a + n_b:n_a + n_b + n_extra]]
    outs = epilogue(accs, extras)
    for o_ref, o in zip(refs[n_a + n_b + n_extra:], outs):
        o_ref[...] = o.astype(o_ref.dtype)


def matmul(name, acts, weights, extras, out_dtypes, epilogue, n_out, tm, tn):
    M = acts[0].shape[0]
    in_specs = [pl.BlockSpec((tm, a.shape[1]), lambda i, j: (i, 0)) for a in acts]
    operands = list(acts)
    for _, w, layer, col0, transposed in weights:
        if transposed:
            in_specs.append(pl.BlockSpec((pl.Element(1), pl.Element(tn), pl.Element(w.shape[2])),
                                         lambda i, j, layer=layer, col0=col0:
                                         (layer, pl.multiple_of(col0 + j * tn, LANES // 2), 0)))
        else:
            in_specs.append(pl.BlockSpec((None, w.shape[1], tn),
                                         lambda i, j, layer=layer, off=col0 // tn: (layer, 0, j + off)))
        operands.append(w)
    for e in extras:
        if e[0] == "row":
            _, arr, layer, col0 = e
            in_specs.append(pl.BlockSpec((None, 1, tn),
                                         lambda i, j, layer=layer, off=col0 // tn: (layer, 0, j + off)))
        else:
            _, arr, col0 = e
            in_specs.append(pl.BlockSpec((tm, tn), lambda i, j, off=col0 // tn: (i, j + off)))
        operands.append(arr)
    return pl.pallas_call(
        functools.partial(_mm_kernel, a_index=tuple(w[0] for w in weights),
                          w_transposed=tuple(w[4] for w in weights), n_a=len(acts),
                          n_extra=len(extras), epilogue=epilogue),
        grid=(M // tm, n_out // tn),
        in_specs=in_specs,
        out_specs=[pl.BlockSpec((tm, tn), lambda i, j: (i, j)) for _ in out_dtypes],
        out_shape=[jax.ShapeDtypeStruct((M, n_out), dt) for dt in out_dtypes],
        name=name,
        compiler_params=_cparams("arbitrary", "arbitrary"),
    )(*operands)


def _epi_glu(accs, extras):
    return [accs[0] * jax.nn.sigmoid(accs[1])]


def _epi_gated(accs, extras):
    return [accs[0] * jax.nn.sigmoid(accs[1] + extras[0])]


def _epi_gated_merge(accs, extras):
    return [extras[1] + accs[0] * jax.nn.sigmoid(accs[1] + extras[0])]


def _mla_proj_kernel(h_ref, wlat_ref, wpe_ref, wq_ref, wkv_ref, qg_ref, kvg_ref, cos_ref, sin_ref,
                     q_ref, k_ref, v_ref):
    h = h_ref[...]
    cos, sin = cos_ref[...], sin_ref[...]
    z = _dot_nt(h, wlat_ref[...])
    cq = _rms(z[:, :Q_LORA], qg_ref[...]).astype(BF16)
    ckv = _rms(z[:, Q_LORA:], kvg_ref[...]).astype(BF16)
    pe = _dot_nt(h, wpe_ref[...])
    half = QK_ROPE // 2
    lane = lax.broadcasted_iota(jnp.int32, pe.shape, 1)
    swapped = jnp.where(lane < half, pltpu.roll(pe, LANES - half, 1), pltpu.roll(pe, half, 1))
    kpe = jnp.where(lane < QK_ROPE, pe * cos + swapped * sin, 0.0).astype(k_ref.dtype)

    scale = (QK_NOPE + QK_ROPE) ** -0.5 * math.log2(math.e)
    kv_width = QK_NOPE + V_DIM
    for hd in range(MLA_HEADS):
        blk = jnp.dot(cq, wq_ref[:, hd * QK_PAD:(hd + 1) * QK_PAD], preferred_element_type=F32)
        q_ref[hd, :, 0:QK_NOPE] = (blk[:, :QK_NOPE] * scale).astype(q_ref.dtype)
        qpe = blk[:, QK_NOPE:]
        roped = qpe * cos + pltpu.roll(qpe, QK_ROPE, 1) * sin
        q_ref[hd, :, QK_NOPE:QK_PAD] = (roped * scale).astype(q_ref.dtype)

        blk = jnp.dot(ckv, wkv_ref[:, hd * kv_width:(hd + 1) * kv_width], preferred_element_type=F32)
        k_ref[hd, :, 0:QK_NOPE] = blk[:, :QK_NOPE].astype(k_ref.dtype)
        k_ref[hd, :, QK_NOPE:QK_PAD] = kpe
        v_ref[:, hd * V_DIM:(hd + 1) * V_DIM] = blk[:, QK_NOPE:].astype(v_ref.dtype)


def mla_proj(h, w_in_lo, w_q, w_kv, layer, qg3, kvg3, cos, sin, tm=512):
    M, K = h.shape
    n_lat = Q_LORA + KV_LORA
    rows = lambda n: pl.BlockSpec((tm, n), lambda i: (i, 0))
    once = pl.Buffered(1)
    heads = pl.BlockSpec((MLA_HEADS, tm, QK_PAD), lambda i: (0, i, 0))
    return pl.pallas_call(
        _mla_proj_kernel,
        grid=(M // tm,),
        in_specs=[rows(K),
                  pl.BlockSpec((None, n_lat, K), lambda i: (layer, COL_LATENT // n_lat, 0), pipeline_mode=once),
                  pl.BlockSpec((None, LANES, K), lambda i: (layer, COL_KPE // LANES, 0), pipeline_mode=once),
                  pl.BlockSpec((None, Q_LORA, MLA_HEADS * QK_PAD), lambda i: (layer, 0, 0), pipeline_mode=once),
                  pl.BlockSpec((None, KV_LORA, MLA_HEADS * (QK_NOPE + V_DIM)), lambda i: (layer, 0, 0),
                               pipeline_mode=once),
                  _layer_row(Q_LORA, layer), _layer_row(KV_LORA, layer),
                  rows(LANES), rows(LANES)],
        out_specs=[heads, heads, rows(MLA_HEADS * V_DIM)],
        out_shape=[jax.ShapeDtypeStruct((MLA_HEADS, M, QK_PAD), BF16),
                   jax.ShapeDtypeStruct((MLA_HEADS, M, QK_PAD), BF16),
                   jax.ShapeDtypeStruct((M, MLA_HEADS * V_DIM), BF16)],
        name="mla_proj",
        compiler_params=_cparams("arbitrary"),
    )(h, w_in_lo, w_in_lo, w_q, w_kv, qg3, kvg3, cos, sin)


ATTN_RB = CHUNK
ATTN_UNROLLS = (8, 4, 2)


def _attn_kernel(q_ref, k_ref, v_ref, o_ref, vone_ref, *set_refs, tq):
    vone_ref[:, 0:V_DIM] = v_ref[...]
    vone_ref[:, V_DIM:] = jnp.ones((vone_ref.shape[0], V_DIM), vone_ref.dtype)
    n_set = len(set_refs) // 2
    sets = [_AttnBlock(q_ref, k_ref, vone_ref, o_ref, set_refs[i * n_set:(i + 1) * n_set], tq, i)
            for i in range(2)]
    n_q = q_ref.shape[1] // tq
    assert n_q % 2 == 0

    sets[0].first_block()
    sets[1].fill(jnp.int32(1))

    def query_block_pair(qq, carry):
        qa, qb = 2 * qq, 2 * qq + 1
        sets[1].drain(qa - 1)
        sets[0].fill(qa)
        sets[0].steady(qa)
        sets[0].drain(qa)
        sets[1].fill(qb)
        sets[1].steady(qb)
        return carry

    lax.fori_loop(1, n_q // 2, query_block_pair, 0)
    sets[1].drain(jnp.int32(n_q - 1))


class _AttnBlock:
    def __init__(self, q_ref, k_ref, v_ref, o_ref, refs, tq, par):
        s0, s1, p0, p1, a0, a1, self.m_ref, self.acc_ref = refs
        self.s_refs, self.p_refs, self.a_refs = (s0, s1), (p0, p1), (a0, a1)
        self.q_ref, self.k_ref, self.v_ref, self.o_ref = q_ref, k_ref, v_ref, o_ref
        self.tq, self.par = tq, par

    def _q_rows(self, qi):
        return pl.ds(pl.multiple_of(qi * self.tq, self.tq), self.tq)

    def _init(self):
        self.m_ref[...] = jnp.full(self.m_ref.shape, NEG_BIG, F32)
        self.acc_ref[...] = jnp.zeros(self.acc_ref.shape, F32)

    def scores(self, qi, j, par):
        start = pl.multiple_of(j * self.tq, self.tq)
        kb = self.k_ref[0, pl.ds(start, self.tq), :]
        self.s_refs[par][...] = lax.dot_general(self.q_ref[0, self._q_rows(qi), :], kb,
                                                (((1,), (1,)), ((), ())), preferred_element_type=F32)

    def softmax(self, par, masked):
        s_ref, p_ref, a_ref, m_ref = self.s_refs[par], self.p_refs[par], self.a_refs[par], self.m_ref
        for r in range(self.tq // ATTN_RB):
            rows = slice(r * ATTN_RB, (r + 1) * ATTN_RB)
            cols = []
            for c in range(self.tq // LANES):
                first_chunk = (c * LANES) // CHUNK
                if masked and first_chunk > r:
                    cols.append(None)
                    continue
                sc = s_ref[rows, c * LANES:(c + 1) * LANES]
                if masked and first_chunk == r:
                    lane = lax.broadcasted_iota(jnp.int32, sc.shape, 1)
                    sc = jnp.where(lane < CHUNK, sc, NEG_BIG)
                cols.append(sc)
            mx = functools.reduce(jnp.maximum, [sc for sc in cols if sc is not None])
            m_prev = m_ref[rows, :]
            m_new = jnp.maximum(m_prev, jnp.max(mx, axis=-1, keepdims=True))
            m_ref[rows, :] = m_new
            a_ref[rows, :] = jnp.exp2(m_prev - m_new)
            for c, sc in enumerate(cols):
                pc = jnp.zeros((ATTN_RB, LANES), BF16) if sc is None else jnp.exp2(sc - m_new).astype(BF16)
                p_ref[rows, c * LANES:(c + 1) * LANES] = pc

    def values(self, j, par):
        start = pl.multiple_of(j * self.tq, self.tq)
        vb = self.v_ref[pl.ds(start, self.tq), :]
        alpha = self.a_refs[par][...]
        self.acc_ref[...] = (jnp.concatenate([alpha, alpha], axis=1) * self.acc_ref[...]
                             + jnp.dot(self.p_refs[par][...], vb, preferred_element_type=F32))

    def _write(self, qi):
        acc = self.acc_ref
        self.o_ref[self._q_rows(qi), :] = (acc[:, 0:V_DIM] / acc[:, V_DIM:]).astype(self.o_ref.dtype)

    def first_block(self):
        qi = jnp.int32(0)
        self._init()
        self.scores(qi, qi, 0)
        self.softmax(0, True)
        self.values(qi, 0)
        self._write(qi)

    def fill(self, qi):
        self._init()
        self.scores(qi, 0, 0)
        self.scores(qi, 1, 1)
        self.softmax(0, False)

    def _tick(self, qi, t, par):
        self.scores(qi, t, par)
        self.softmax(1 - par, False)
        self.values(t - 2, par)

    def steady(self, qi):
        n_left = jnp.maximum(qi - 1, 0)
        t0 = 2
        for unroll in ATTN_UNROLLS:

            def unrolled(i, carry, t0=t0, unroll=unroll):
                for u in range(unroll):
                    self._tick(qi, t0 + unroll * i + u, u % 2)
                return carry

            lax.fori_loop(0, n_left // unroll, unrolled, 0)
            t0 = t0 + (n_left // unroll) * unroll
            n_left = n_left % unroll

        @pl.when(n_left == 1)
        def _():
            self._tick(qi, qi, 0)

    def drain(self, qi):
        self.softmax(self.par, True)
        self.values(qi - 1, 1 - self.par)
        self.values(qi, self.par)
        self._write(qi)


def mla_attention(q, k, v, tq=512):
    H, S, _ = q.shape
    buffer_set = ([pltpu.VMEM((tq, tq), F32)] * 2
                  + [pltpu.VMEM((tq, tq), BF16)] * 2
                  + [pltpu.VMEM((tq, LANES), F32)] * 2
                  + [pltpu.VMEM((tq, LANES), F32),
                     pltpu.VMEM((tq, 2 * V_DIM), F32)])
    return pl.pallas_call(
        functools.partial(_attn_kernel, tq=tq),
        grid=(H,),
        in_specs=[pl.BlockSpec((1, S, QK_PAD), lambda h: (h, 0, 0)),
                  pl.BlockSpec((1, S, QK_PAD), lambda h: (h, 0, 0)),
                  pl.BlockSpec((S, V_DIM), lambda h: (0, h))],
        out_specs=pl.BlockSpec((S, V_DIM), lambda h: (0, h)),
        out_shape=jax.ShapeDtypeStruct((S, H * V_DIM), BF16),
        scratch_shapes=[pltpu.VMEM((S, 2 * V_DIM), BF16)] + buffer_set + buffer_set,
        name="mla_attention",
        compiler_params=_cparams("arbitrary"),
    )(q, k, v)


CONV_RB = 64
CONV_CB = LANES


def _conv_kernel(cur_ref, halo_ref, w_ref, b_ref, g_ref, beta_ref, o_ref, buf_ref, sh_ref, acc_ref, wb_ref, *, tm):
    i = pl.program_id(0)

    @pl.when(i == 0)
    def _():
        for k in range(CONV_K):
            wb_ref[k] = jnp.broadcast_to(w_ref[k:k + 1, :], (SUBLANES, D_CONV))

    buf_ref[0:CONV_HALO, :] = jnp.where(i > 0, halo_ref[...], 0.0)
    buf_ref[CONV_HALO:CONV_HALO + tm, :] = cur_ref[...]
    n_buf = tm + CONV_HALO
    n_sh = n_buf - SUBLANES
    for c0 in range(0, D_CONV, LANES):
        x = buf_ref[:, c0:c0 + LANES]
        for b in range(1, SUBLANES):
            sh_ref[b - 1, :, c0:c0 + LANES] = pltpu.roll(x, n_buf - b, 0)[0:n_sh, :]
    off = CONV_HALO - (CONV_K - 1)
    for c0 in range(0, D_CONV, CONV_CB):
        cols = slice(c0, c0 + CONV_CB)
        w_all = wb_ref[:, :, cols]
        bias = jnp.broadcast_to(b_ref[:, cols], (CONV_RB, CONV_CB))

        def row_block(rb, carry, cols=cols, w_all=w_all, bias=bias):
            r0 = pl.multiple_of(rb * CONV_RB, CONV_RB)
            acc = bias
            for b in range(SUBLANES):
                taps = [k for k in range(CONV_K) if (off + k) % SUBLANES == b]
                a_lo, a_hi = (off + taps[0]) // SUBLANES, (off + taps[-1]) // SUBLANES
                rows = pl.ds(r0 + a_lo * SUBLANES, (a_hi - a_lo) * SUBLANES + CONV_RB)
                slab = buf_ref[rows, cols] if b == 0 else sh_ref[b - 1, rows, cols]
                for k in taps:
                    s0 = ((off + k) // SUBLANES - a_lo) * SUBLANES
                    wk = jnp.concatenate([w_all[k]] * (CONV_RB // SUBLANES), axis=0)
                    acc = acc + slab[s0:s0 + CONV_RB, :] * wk
            acc_ref[pl.ds(r0, CONV_RB), cols] = acc
            return carry

        lax.fori_loop(0, tm // CONV_RB, row_block, 0)
    y = acc_ref[...]
    mu = jnp.mean(y, axis=-1, keepdims=True)
    yc = y - mu
    yn = yc * lax.rsqrt(jnp.mean(yc * yc, axis=-1, keepdims=True) + EPS) * g_ref[...] + beta_ref[...]
    o_ref[...] = (yn * jax.nn.sigmoid(yn)).astype(o_ref.dtype)


def conv_ln_silu(hglu, conv_w, conv_b3, ln_g3, ln_b3, layer, tm=256):
    S, D = hglu.shape
    row = _layer_row(D, layer)
    return pl.pallas_call(
        functools.partial(_conv_kernel, tm=tm),
        grid=(S // tm,),
        in_specs=[pl.BlockSpec((tm, D), lambda i: (i, 0)),
                  pl.BlockSpec((CONV_HALO, D), lambda i: (jnp.maximum(i * (tm // CONV_HALO) - 1, 0), 0)),
                  pl.BlockSpec((None, CONV_K, D), lambda i: (layer, 0, 0)),
                  row, row, row],
        out_specs=pl.BlockSpec((tm, D), lambda i: (i, 0)),
        out_shape=jax.ShapeDtypeStruct((S, D), BF16),
        scratch_shapes=[pltpu.VMEM((CONV_HALO + tm, D), F32),
                        pltpu.VMEM((SUBLANES - 1, CONV_HALO + tm - SUBLANES, D), F32),
                        pltpu.VMEM((tm, D), F32),
                        pltpu.VMEM((CONV_K, SUBLANES, D), F32)],
        name="conv_ln_silu",
        compiler_params=_cparams("arbitrary"),
    )(hglu, hglu, conv_w, conv_b3, ln_g3, ln_b3)


def _memkv_kernel(mem_ref, g_ref, w_ref, k_ref, v_ref):
    mn = _rms(mem_ref[...], g_ref[...]).astype(BF16)
    kv = jnp.dot(mn, w_ref[...], preferred_element_type=F32)
    n = X_HEADS * X_HEAD_DIM
    k_ref[...] = kv[:, :n].astype(k_ref.dtype)
    v_ref[...] = kv[:, n:].astype(v_ref.dtype)


def mem_kv(mem, g3, w, layer):
    n = X_HEADS * X_HEAD_DIM
    D = mem.shape[1]
    full = lambda shape: pl.BlockSpec(shape, lambda i: (0, 0))
    return pl.pallas_call(
        _memkv_kernel,
        grid=(1,),
        in_specs=[full(mem.shape), _layer_row(D, layer),
                  pl.BlockSpec((None, D, 2 * n), lambda i: (layer, 0, 0))],
        out_specs=[full((N_MEM, n)), full((N_MEM, n))],
        out_shape=[jax.ShapeDtypeStruct((N_MEM, n), BF16)] * 2,
        name="mem_kv",
        compiler_params=_cparams("arbitrary"),
    )(mem, g3, w)


def _mix_cross_kernel(m_ref, x_ref, wo_ref, gmem_ref, wq_ref, k_ref, v_ref, wxo_ref, gffn_ref, xo_ref, ho_ref):
    x1 = x_ref[...] + jnp.dot(m_ref[...], wo_ref[...], preferred_element_type=F32)
    hm = _rms(x1, gmem_ref[...]).astype(BF16)
    q = jnp.dot(hm, wq_ref[...], preferred_element_type=F32) * (X_HEAD_DIM ** -0.5)
    q = q.astype(BF16)
    outs = []
    for h in range(X_HEADS):
        sl = slice(h * X_HEAD_DIM, (h + 1) * X_HEAD_DIM)
        s = lax.dot_general(q[:, sl], k_ref[:, sl], (((1,), (1,)), ((), ())), preferred_element_type=F32)
        p = jnp.exp(s - jnp.max(s, axis=-1, keepdims=True))
        l = jnp.sum(p, axis=-1, keepdims=True)
        o = jnp.dot(p.astype(BF16), v_ref[:, sl], preferred_element_type=F32) / l
        outs.append(o.astype(BF16))
    o = jnp.concatenate(outs, axis=-1)
    x2 = x1 + jnp.dot(o, wxo_ref[...], preferred_element_type=F32)
    xo_ref[...] = x2
    ho_ref[...] = _rms(x2, gffn_ref[...]).astype(ho_ref.dtype)


def mix_cross(merged, x, w_out, g_mem3, w_xq, mk, mv, w_xo, g_ffn3, layer, tm=512):
    M, D = merged.shape
    n = X_HEADS * X_HEAD_DIM
    const = lambda shape: pl.BlockSpec(shape, lambda i: (0, 0))
    rows = pl.BlockSpec((tm, D), lambda i: (i, 0))
    once = pl.Buffered(1)
    weight = lambda k, c: pl.BlockSpec((None, k, c), lambda i: (layer, 0, 0), pipeline_mode=once)
    return pl.pallas_call(
        _mix_cross_kernel,
        grid=(M // tm,),
        in_specs=[rows, rows, weight(D, D), _layer_row(D, layer), weight(D, n),
                  const((N_MEM, n)), const((N_MEM, n)), weight(n, D), _layer_row(D, layer)],
        out_specs=[rows, rows],
        out_shape=[jax.ShapeDtypeStruct((M, D), F32), jax.ShapeDtypeStruct((M, D), BF16)],
        name="mix_cross",
        compiler_params=_cparams("arbitrary"),
    )(merged, x, w_out, g_mem3, w_xq, mk, mv, w_xo, g_ffn3)


def _mlp_kernel(h_ref, w1_ref, w2_ref, x_ref, g_ref, xo_ref, ho_ref):
    f = pl.program_id(1)

    @pl.when(f == 0)
    def _():
        xo_ref[...] = x_ref[...]

    a = jnp.dot(h_ref[...], w1_ref[...], preferred_element_type=F32)
    a = jnp.square(jnp.maximum(a, 0.0)).astype(BF16)
    xo_ref[...] += jnp.dot(a, w2_ref[...], preferred_element_type=F32)

    @pl.when(f == pl.num_programs(1) - 1)
    def _():
        ho_ref[...] = _rms(xo_ref[...], g_ref[...]).astype(ho_ref.dtype)


def mlp(h, w1, w2, layer, x, g3, g_layer, out_dtype, tm=512, tf=1024):
    M, D = h.shape
    F = w1.shape[2]
    rows = pl.BlockSpec((tm, D), lambda i, f: (i, 0))
    return pl.pallas_call(
        _mlp_kernel,
        grid=(M // tm, F // tf),
        in_specs=[rows,
                  pl.BlockSpec((None, D, tf), lambda i, f: (layer, 0, f)),
                  pl.BlockSpec((None, tf, D), lambda i, f: (layer, f, 0)),
                  rows,
                  _layer_row(D, g_layer)],
        out_specs=[rows, rows],
        out_shape=[jax.ShapeDtypeStruct((M, D), F32), jax.ShapeDtypeStruct((M, D), out_dtype)],
        name="mlp",
        compiler_params=_cparams("arbitrary", "arbitrary"),
    )(h, w1, w2, x, g3)


def _permute_q_weight(w_uq):
    L, K, _ = w_uq.shape
    half = QK_ROPE // 2
    w = w_uq.reshape(L, K, MLA_HEADS, QK_NOPE + QK_ROPE)
    pe = w[..., QK_NOPE:]
    w = jnp.concatenate([w, pe[..., half:], pe[..., :half]], axis=-1)
    return w.reshape(L, K, MLA_HEADS * QK_PAD)


def kernel(x, mem, positions, norm_mix_g, w_in, b_gate, conv_w, conv_b, conv_ln_g, conv_ln_b, w_conv_out, q_norm_g, w_uq, kv_norm_g, w_ukv, w_mla_out, w_out, norm_mem_g, mem_norm_g, w_xq, w_xkv, w_xo, norm_ffn_g, w_ff1, w_ff2, final_norm_g):
    B, S, D = x.shape
    assert (B, S, D) == (1, SEQ, D_MODEL)
    x2d = x.reshape(S, D)
    mem2d = mem.reshape(N_MEM, D)
    vec3 = lambda v: v.reshape(v.shape[0], 1, v.shape[1])

    inv_freq = 1.0 / (ROPE_THETA ** (jnp.arange(0, QK_ROPE, 2, dtype=F32) / QK_ROPE))
    cos, sin = rope_tables(positions.reshape(S, 1), jnp.tile(inv_freq, 4).reshape(1, LANES))

    w_in_lo = jnp.swapaxes(w_in, 1, 2).astype(BF16)
    w_q = _permute_q_weight(w_uq).astype(BF16)
    w_kv, w_co, w_mo, w_o = (w.astype(BF16) for w in (w_ukv, w_conv_out, w_mla_out, w_out))
    w_q_x, w_kv_x, w_o_x = (w.astype(BF16) for w in (w_xq, w_xkv, w_xo))
    w_1, w_2 = w_ff1.astype(BF16), w_ff2.astype(BF16)

    g_mix, g_mem, g_memn, g_ffn = vec3(norm_mix_g), vec3(norm_mem_g), vec3(mem_norm_g), vec3(norm_ffn_g)
    g_q, g_kv, b_g = vec3(q_norm_g), vec3(kv_norm_g), vec3(b_gate)
    c_b, ln_g, ln_b = vec3(conv_b), vec3(conv_ln_g), vec3(conv_ln_b)
    g_final = final_norm_g.reshape(1, 1, D)

    h = rms_norm_rows(x2d, g_mix, 0, BF16)
    xcur = x2d
    for l in range(DEPTH):
        (hglu,) = matmul("glu_proj", [h], [(0, w_in_lo, l, COL_GLU_A, True), (0, w_in_lo, l, COL_GLU_G, True)],
                         [], [F32], _epi_glu, D_CONV, 1024, 1024)

        hc = conv_ln_silu(hglu, conv_w, c_b, ln_g, ln_b, l)
        (gy,) = matmul("conv_out_gated", [hc, h], [(0, w_co, l, 0, False), (1, w_in_lo, l, COL_GATE, True)],
                       [("row", b_g, l, 0)], [BF16], _epi_gated, D, 1024, 512)

        q, k, v = mla_proj(h, w_in_lo, w_q, w_kv, l, g_q, g_kv, cos, sin)
        o = mla_attention(q, k, v)
        (merged,) = matmul("mla_out_gated_merge", [o, h],
                           [(0, w_mo, l, 0, False), (1, w_in_lo, l, COL_GATE + D, True)],
                           [("row", b_g, l, D), ("tile", gy, 0)], [BF16], _epi_gated_merge, D, 1024, 512)

        mk, mv = mem_kv(mem2d, g_memn, w_kv_x, l)
        xcur, hf = mix_cross(merged, xcur, w_o, g_mem, w_q_x, mk, mv, w_o_x, g_ffn, l)

        last = l == DEPTH - 1
        xcur, h = mlp(hf, w_1, w_2, l, xcur, g_final if last else g_mix, 0 if last else l + 1,
                      F32 if last else BF16)
    return h.reshape(B, S, D)
```

```python
import functools
import math

import jax
import jax.numpy as jnp
from jax import lax
from jax.experimental import pallas as pl
from jax.experimental.pallas import tpu as pltpu

D_MODEL = 2048
SEQ = 8192
DEPTH = 2
CHUNK = 64
N_MEM = 256
EPS = 1e-6
D_CONV = D_MODEL
CONV_K = 31
MLA_HEADS = 16
Q_LORA = 512
KV_LORA = 512
QK_NOPE = 128
QK_ROPE = 64
V_DIM = 128
ROPE_THETA = 10000.0
X_HEADS = 4
X_HEAD_DIM = 128
D_FF = 4 * D_MODEL

LANES = 128
SUBLANES = 8
QK_PAD = 256
CONV_HALO = 32
NEG_BIG = -1e30
VMEM_LIMIT = 56 * 1024 * 1024

COL_GLU_A = 0
COL_GLU_G = D_CONV
COL_LATENT = 2 * D_CONV
COL_KPE = COL_LATENT + Q_LORA + KV_LORA
COL_GATE = COL_KPE + QK_ROPE

F32 = jnp.float32
BF16 = jnp.bfloat16


def _cparams(*sem):
    return pltpu.CompilerParams(dimension_semantics=sem, vmem_limit_bytes=VMEM_LIMIT)


def _rms(x, g):
    return x * lax.rsqrt(jnp.mean(x * x, axis=-1, keepdims=True) + EPS) * g


def _layer_row(n, layer, col0=0):
    return pl.BlockSpec((None, 1, n), lambda *_: (layer, 0, col0 // n))


def _rope_table_kernel(pos_ref, freq_ref, cos_ref, sin_ref):
    ang = pos_ref[...].astype(F32) * freq_ref[...]
    lane = lax.broadcasted_iota(jnp.int32, ang.shape, 1)
    s = jnp.sin(ang)
    cos_ref[...] = jnp.where(lane < QK_ROPE, jnp.cos(ang), 0.0)
    sin_ref[...] = jnp.where(lane < QK_ROPE // 2, -s, jnp.where(lane < QK_ROPE, s, 0.0))


def rope_tables(pos_col, freq_row):
    S = pos_col.shape[0]
    tm = 1024
    tab = pl.BlockSpec((tm, LANES), lambda i: (i, 0))
    return pl.pallas_call(
        _rope_table_kernel,
        grid=(S // tm,),
        in_specs=[pl.BlockSpec((tm, 1), lambda i: (i, 0)),
                  pl.BlockSpec((1, LANES), lambda i: (0, 0))],
        out_specs=[tab, tab],
        out_shape=[jax.ShapeDtypeStruct((S, LANES), F32)] * 2,
        name="rope_table",
        compiler_params=_cparams("arbitrary"),
    )(pos_col, freq_row)


def _norm_kernel(x_ref, g_ref, o_ref):
    o_ref[...] = _rms(x_ref[...], g_ref[...]).astype(o_ref.dtype)


def rms_norm_rows(x, g3, layer, out_dtype, tm=512):
    M, D = x.shape
    return pl.pallas_call(
        _norm_kernel,
        grid=(M // tm,),
        in_specs=[pl.BlockSpec((tm, D), lambda i: (i, 0)), _layer_row(D, layer)],
        out_specs=pl.BlockSpec((tm, D), lambda i: (i, 0)),
        out_shape=jax.ShapeDtypeStruct((M, D), out_dtype),
        name="rms_norm",
        compiler_params=_cparams("arbitrary"),
    )(x, g3)


def _dot_nt(a, w_t):
    return lax.dot_general(a, w_t, (((1,), (1,)), ((), ())), preferred_element_type=F32)


def _mm_kernel(*refs, a_index, w_transposed, n_a, n_extra, epilogue):
    n_b = len(a_index)
    acts = [r[...] for r in refs[:n_a]]
    accs = [_dot_nt(acts[ai], b[0]) if t else jnp.dot(acts[ai], b[...], preferred_element_type=F32)
            for ai, t, b in zip(a_index, w_transposed, refs[n_a:n_a + n_b])]
    extras = [e[...] for e in refs[n_a + n_b:n_a + n_b + n_extra]]
    outs = epilogue(accs, extras)
    for o_ref, o in zip(refs[n_a + n_b + n_extra:], outs):
        o_ref[...] = o.astype(o_ref.dtype)


def matmul(name, acts, weights, extras, out_dtypes, epilogue, n_out, tm, tn):
    M = acts[0].shape[0]
    in_specs = [pl.BlockSpec((tm, a.shape[1]), lambda i, j: (i, 0)) for a in acts]
    operands = list(acts)
    for _, w, layer, col0, transposed in weights:
        if transposed:
            in_specs.append(pl.BlockSpec((pl.Element(1), pl.Element(tn), pl.Element(w.shape[2])),
                                         lambda i, j, layer=layer, col0=col0:
                                         (layer, pl.multiple_of(col0 + j * tn, LANES // 2), 0)))
        else:
            in_specs.append(pl.BlockSpec((None, w.shape[1], tn),
                                         lambda i, j, layer=layer, off=col0 // tn: (layer, 0, j + off)))
        operands.append(w)
    for e in extras:
        if e[0] == "row":
            _, arr, layer, col0 = e
            in_specs.append(pl.BlockSpec((None, 1, tn),
                                         lambda i, j, layer=layer, off=col0 // tn: (layer, 0, j + off)))
        else:
            _, arr, col0 = e
            in_specs.append(pl.BlockSpec((tm, tn), lambda i, j, off=col0 // tn: (i, j + off)))
        operands.append(arr)
    return pl.pallas_call(
        functools.partial(_mm_kernel, a_index=tuple(w[0] for w in weights),
                          w_transposed=tuple(w[4] for w in weights), n_a=len(acts),
                          n_extra=len(extras), epilogue=epilogue),
        grid=(M // tm, n_out // tn),
        in_specs=in_specs,
        out_specs=[pl.BlockSpec((tm, tn), lambda i, j: (i, j)) for _ in out_dtypes],
        out_shape=[jax.ShapeDtypeStruct((M, n_out), dt) for dt in out_dtypes],
        name=name,
        compiler_params=_cparams("arbitrary", "arbitrary"),
    )(*operands)


def _epi_glu(accs, extras):
    return [accs[0] * jax.nn.sigmoid(accs[1])]


def _epi_gated(accs, extras):
    return [accs[0] * jax.nn.sigmoid(accs[1] + extras[0])]


def _epi_gated_merge(accs, extras):
    return [extras[1] + accs[0] * jax.nn.sigmoid(accs[1] + extras[0])]


def _mla_proj_kernel(h_ref, wlat_ref, wpe_ref, wq_ref, wkv_ref, qg_ref, kvg_ref, cos_ref, sin_ref,
                     q_ref, k_ref, v_ref):
    h = h_ref[...]
    cos, sin = cos_ref[...], sin_ref[...]
    z = _dot_nt(h, wlat_ref[...])
    cq = _rms(z[:, :Q_LORA], qg_ref[...]).astype(BF16)
    ckv = _rms(z[:, Q_LORA:], kvg_ref[...]).astype(BF16)
    pe = _dot_nt(h, wpe_ref[...])
    half = QK_ROPE // 2
    lane = lax.broadcasted_iota(jnp.int32, pe.shape, 1)
    swapped = jnp.where(lane < half, pltpu.roll(pe, LANES - half, 1), pltpu.roll(pe, half, 1))
    kpe = jnp.where(lane < QK_ROPE, pe * cos + swapped * sin, 0.0).astype(k_ref.dtype)

    scale = (QK_NOPE + QK_ROPE) ** -0.5 * math.log2(math.e)
    kv_width = QK_NOPE + V_DIM
    for hd in range(MLA_HEADS):
        blk = jnp.dot(cq, wq_ref[:, hd * QK_PAD:(hd + 1) * QK_PAD], preferred_element_type=F32)
        q_ref[hd, :, 0:QK_NOPE] = (blk[:, :QK_NOPE] * scale).astype(q_ref.dtype)
        qpe = blk[:, QK_NOPE:]
        roped = qpe * cos + pltpu.roll(qpe, QK_ROPE, 1) * sin
        q_ref[hd, :, QK_NOPE:QK_PAD] = (roped * scale).astype(q_ref.dtype)

        blk = jnp.dot(ckv, wkv_ref[:, hd * kv_width:(hd + 1) * kv_width], preferred_element_type=F32)
        k_ref[hd, :, 0:QK_NOPE] = blk[:, :QK_NOPE].astype(k_ref.dtype)
        k_ref[hd, :, QK_NOPE:QK_PAD] = kpe
        v_ref[:, hd * V_DIM:(hd + 1) * V_DIM] = blk[:, QK_NOPE:].astype(v_ref.dtype)


def mla_proj(h, w_in_lo, w_q, w_kv, layer, qg3, kvg3, cos, sin, tm=512):
    M, K = h.shape
    n_lat = Q_LORA + KV_LORA
    rows = lambda n: pl.BlockSpec((tm, n), lambda i: (i, 0))
    once = pl.Buffered(1)
    heads = pl.BlockSpec((MLA_HEADS, tm, QK_PAD), lambda i: (0, i, 0))
    return pl.pallas_call(
        _mla_proj_kernel,
        grid=(M // tm,),
        in_specs=[rows(K),
                  pl.BlockSpec((None, n_lat, K), lambda i: (layer, COL_LATENT // n_lat, 0), pipeline_mode=once),
                  pl.BlockSpec((None, LANES, K), lambda i: (layer, COL_KPE // LANES, 0), pipeline_mode=once),
                  pl.BlockSpec((None, Q_LORA, MLA_HEADS * QK_PAD), lambda i: (layer, 0, 0), pipeline_mode=once),
                  pl.BlockSpec((None, KV_LORA, MLA_HEADS * (QK_NOPE + V_DIM)), lambda i: (layer, 0, 0),
                               pipeline_mode=once),
                  _layer_row(Q_LORA, layer), _layer_row(KV_LORA, layer),
                  rows(LANES), rows(LANES)],
        out_specs=[heads, heads, rows(MLA_HEADS * V_DIM)],
        out_shape=[jax.ShapeDtypeStruct((MLA_HEADS, M, QK_PAD), BF16),
                   jax.ShapeDtypeStruct((MLA_HEADS, M, QK_PAD), BF16),
                   jax.ShapeDtypeStruct((M, MLA_HEADS * V_DIM), BF16)],
        name="mla_proj",
        compiler_params=_cparams("arbitrary"),
    )(h, w_in_lo, w_in_lo, w_q, w_kv, qg3, kvg3, cos, sin)


ATTN_RB = CHUNK
ATTN_UNROLLS = (8, 4, 2)


def _attn_kernel(q_ref, k_ref, v_ref, o_ref, vone_ref, *set_refs, tq):
    vone_ref[:, 0:V_DIM] = v_ref[...]
    vone_ref[:, V_DIM:] = jnp.ones((vone_ref.shape[0], V_DIM), vone_ref.dtype)
    n_set = len(set_refs) // 2
    sets = [_AttnBlock(q_ref, k_ref, vone_ref, o_ref, set_refs[i * n_set:(i + 1) * n_set], tq, i)
            for i in range(2)]
    n_q = q_ref.shape[1] // tq
    assert n_q % 2 == 0

    sets[0].first_block()
    sets[1].fill(jnp.int32(1))

    def query_block_pair(qq, carry):
        qa, qb = 2 * qq, 2 * qq + 1
        sets[1].drain(qa - 1)
        sets[0].fill(qa)
        sets[0].steady(qa)
        sets[0].drain(qa)
        sets[1].fill(qb)
        sets[1].steady(qb)
        return carry

    lax.fori_loop(1, n_q // 2, query_block_pair, 0)
    sets[1].drain(jnp.int32(n_q - 1))


class _AttnBlock:
    def __init__(self, q_ref, k_ref, v_ref, o_ref, refs, tq, par):
        s0, s1, p0, p1, a0, a1, self.m_ref, self.acc_ref = refs
        self.s_refs, self.p_refs, self.a_refs = (s0, s1), (p0, p1), (a0, a1)
        self.q_ref, self.k_ref, self.v_ref, self.o_ref = q_ref, k_ref, v_ref, o_ref
        self.tq, self.par = tq, par

    def _q_rows(self, qi):
        return pl.ds(pl.multiple_of(qi * self.tq, self.tq), self.tq)

    def _init(self):
        self.m_ref[...] = jnp.full(self.m_ref.shape, NEG_BIG, F32)
        self.acc_ref[...] = jnp.zeros(self.acc_ref.shape, F32)

    def scores(self, qi, j, par):
        start = pl.multiple_of(j * self.tq, self.tq)
        kb = self.k_ref[0, pl.ds(start, self.tq), :]
        self.s_refs[par][...] = lax.dot_general(self.q_ref[0, self._q_rows(qi), :], kb,
                                                (((1,), (1,)), ((), ())), preferred_element_type=F32)

    def softmax(self, par, masked):
        s_ref, p_ref, a_ref, m_ref = self.s_refs[par], self.p_refs[par], self.a_refs[par], self.m_ref
        for r in range(self.tq // ATTN_RB):
            rows = slice(r * ATTN_RB, (r + 1) * ATTN_RB)
            cols = []
            for c in range(self.tq // LANES):
                first_chunk = (c * LANES) // CHUNK
                if masked and first_chunk > r:
                    cols.append(None)
                    continue
                sc = s_ref[rows, c * LANES:(c + 1) * LANES]
                if masked and first_chunk == r:
                    lane = lax.broadcasted_iota(jnp.int32, sc.shape, 1)
                    sc = jnp.where(lane < CHUNK, sc, NEG_BIG)
                cols.append(sc)
            mx = functools.reduce(jnp.maximum, [sc for sc in cols if sc is not None])
            m_prev = m_ref[rows, :]
            m_new = jnp.maximum(m_prev, jnp.max(mx, axis=-1, keepdims=True))
            m_ref[rows, :] = m_new
            a_ref[rows, :] = jnp.exp2(m_prev - m_new)
            for c, sc in enumerate(cols):
                pc = jnp.zeros((ATTN_RB, LANES), BF16) if sc is None else jnp.exp2(sc - m_new).astype(BF16)
                p_ref[rows, c * LANES:(c + 1) * LANES] = pc

    def values(self, j, par):
        start = pl.multiple_of(j * self.tq, self.tq)
        vb = self.v_ref[pl.ds(start, self.tq), :]
        alpha = self.a_refs[par][...]
        self.acc_ref[...] = (jnp.concatenate([alpha, alpha], axis=1) * self.acc_ref[...]
                             + jnp.dot(self.p_refs[par][...], vb, preferred_element_type=F32))

    def _write(self, qi):
        acc = self.acc_ref
        self.o_ref[self._q_rows(qi), :] = (acc[:, 0:V_DIM] / acc[:, V_DIM:]).astype(self.o_ref.dtype)

    def first_block(self):
        qi = jnp.int32(0)
        self._init()
        self.scores(qi, qi, 0)
        self.softmax(0, True)
        self.values(qi, 0)
        self._write(qi)

    def fill(self, qi):
        self._init()
        self.scores(qi, 0, 0)
        self.scores(qi, 1, 1)
        self.softmax(0, False)

    def _tick(self, qi, t, par):
        self.scores(qi, t, par)
        self.softmax(1 - par, False)
        self.values(t - 2, par)

    def steady(self, qi):
        n_left = qi - 1 - (1 - self.par)
        t0 = 2
        for unroll in ATTN_UNROLLS:

            def unrolled(i, carry, t0=t0, unroll=unroll):
                for u in range(unroll):
                    self._tick(qi, t0 + unroll * i + u, u % 2)
                return carry

            lax.fori_loop(0, n_left // unroll, unrolled, 0)
            t0 = t0 + (n_left // unroll) * unroll
            n_left = n_left % unroll

    def drain(self, qi):
        if self.par == 0:
            self._tick(qi, qi, 0)
        self.softmax(self.par, True)
        self.values(qi - 1, 1 - self.par)
        self.values(qi, self.par)
        self._write(qi)


def mla_attention(q, k, v, tq=512):
    H, S, _ = q.shape
    buffer_set = ([pltpu.VMEM((tq, tq), F32)] * 2
                  + [pltpu.VMEM((tq, tq), BF16)] * 2
                  + [pltpu.VMEM((tq, LANES), F32)] * 2
                  + [pltpu.VMEM((tq, LANES), F32),
                     pltpu.VMEM((tq, 2 * V_DIM), F32)])
    return pl.pallas_call(
        functools.partial(_attn_kernel, tq=tq),
        grid=(H,),
        in_specs=[pl.BlockSpec((1, S, QK_PAD), lambda h: (h, 0, 0)),
                  pl.BlockSpec((1, S, QK_PAD), lambda h: (h, 0, 0)),
                  pl.BlockSpec((S, V_DIM), lambda h: (0, h))],
        out_specs=pl.BlockSpec((S, V_DIM), lambda h: (0, h)),
        out_shape=jax.ShapeDtypeStruct((S, H * V_DIM), BF16),
        scratch_shapes=[pltpu.VMEM((S, 2 * V_DIM), BF16)] + buffer_set + buffer_set,
        name="mla_attention",
        compiler_params=_cparams("arbitrary"),
    )(q, k, v)


CONV_RB = 64
CONV_CB = LANES


def _conv_kernel(cur_ref, halo_ref, w_ref, b_ref, g_ref, beta_ref, o_ref, buf_ref, sh_ref, acc_ref, wb_ref, *, tm):
    i = pl.program_id(0)

    @pl.when(i == 0)
    def _():
        for k in range(CONV_K):
            wb_ref[k] = jnp.broadcast_to(w_ref[k:k + 1, :], (SUBLANES, D_CONV))

    buf_ref[0:CONV_HALO, :] = jnp.where(i > 0, halo_ref[...], 0.0)
    buf_ref[CONV_HALO:CONV_HALO + tm, :] = cur_ref[...]
    n_buf = tm + CONV_HALO
    n_sh = n_buf - SUBLANES
    for c0 in range(0, D_CONV, LANES):
        x = buf_ref[:, c0:c0 + LANES]
        for b in range(1, SUBLANES):
            sh_ref[b - 1, :, c0:c0 + LANES] = pltpu.roll(x, n_buf - b, 0)[0:n_sh, :]
    off = CONV_HALO - (CONV_K - 1)
    for c0 in range(0, D_CONV, CONV_CB):
        cols = slice(c0, c0 + CONV_CB)
        w_all = wb_ref[:, :, cols]
        bias = jnp.broadcast_to(b_ref[:, cols], (CONV_RB, CONV_CB))

        def row_block(rb, carry, cols=cols, w_all=w_all, bias=bias):
            r0 = pl.multiple_of(rb * CONV_RB, CONV_RB)
            acc = bias
            for b in range(SUBLANES):
                taps = [k for k in range(CONV_K) if (off + k) % SUBLANES == b]
                a_lo, a_hi = (off + taps[0]) // SUBLANES, (off + taps[-1]) // SUBLANES
                rows = pl.ds(r0 + a_lo * SUBLANES, (a_hi - a_lo) * SUBLANES + CONV_RB)
                slab = buf_ref[rows, cols] if b == 0 else sh_ref[b - 1, rows, cols]
                for k in taps:
                    s0 = ((off + k) // SUBLANES - a_lo) * SUBLANES
                    wk = jnp.concatenate([w_all[k]] * (CONV_RB // SUBLANES), axis=0)
                    acc = acc + slab[s0:s0 + CONV_RB, :] * wk
            acc_ref[pl.ds(r0, CONV_RB), cols] = acc
            return carry

        lax.fori_loop(0, tm // CONV_RB, row_block, 0)
    y = acc_ref[...]
    mu = jnp.mean(y, axis=-1, keepdims=True)
    yc = y - mu
    yn = yc * lax.rsqrt(jnp.mean(yc * yc, axis=-1, keepdims=True) + EPS) * g_ref[...] + beta_ref[...]
    o_ref[...] = (yn * jax.nn.sigmoid(yn)).astype(o_ref.dtype)


def conv_ln_silu(hglu, conv_w, conv_b3, ln_g3, ln_b3, layer, tm=256):
    S, D = hglu.shape
    row = _layer_row(D, layer)
    return pl.pallas_call(
        functools.partial(_conv_kernel, tm=tm),
        grid=(S // tm,),
        in_specs=[pl.BlockSpec((tm, D), lambda i: (i, 0)),
                  pl.BlockSpec((CONV_HALO, D), lambda i: (jnp.maximum(i * (tm // CONV_HALO) - 1, 0), 0)),
                  pl.BlockSpec((None, CONV_K, D), lambda i: (layer, 0, 0)),
                  row, row, row],
        out_specs=pl.BlockSpec((tm, D), lambda i: (i, 0)),
        out_shape=jax.ShapeDtypeStruct((S, D), BF16),
        scratch_shapes=[pltpu.VMEM((CONV_HALO + tm, D), F32),
                        pltpu.VMEM((SUBLANES - 1, CONV_HALO + tm - SUBLANES, D), F32),
                        pltpu.VMEM((tm, D), F32),
                        pltpu.VMEM((CONV_K, SUBLANES, D), F32)],
        name="conv_ln_silu",
        compiler_params=_cparams("arbitrary"),
    )(hglu, hglu, conv_w, conv_b3, ln_g3, ln_b3)


def _memkv_kernel(mem_ref, g_ref, w_ref, k_ref, v_ref):
    mn = _rms(mem_ref[...], g_ref[...]).astype(BF16)
    kv = jnp.dot(mn, w_ref[...], preferred_element_type=F32)
    n = X_HEADS * X_HEAD_DIM
    k_ref[...] = kv[:, :n].astype(k_ref.dtype)
    v_ref[...] = kv[:, n:].astype(v_ref.dtype)


def mem_kv(mem, g3, w, layer):
    n = X_HEADS * X_HEAD_DIM
    D = mem.shape[1]
    full = lambda shape: pl.BlockSpec(shape, lambda i: (0, 0))
    return pl.pallas_call(
        _memkv_kernel,
        grid=(1,),
        in_specs=[full(mem.shape), _layer_row(D, layer),
                  pl.BlockSpec((None, D, 2 * n), lambda i: (layer, 0, 0))],
        out_specs=[full((N_MEM, n)), full((N_MEM, n))],
        out_shape=[jax.ShapeDtypeStruct((N_MEM, n), BF16)] * 2,
        name="mem_kv",
        compiler_params=_cparams("arbitrary"),
    )(mem, g3, w)


def _mix_cross_kernel(m_ref, x_ref, wo_ref, gmem_ref, wq_ref, k_ref, v_ref, wxo_ref, gffn_ref, xo_ref, ho_ref):
    x1 = x_ref[...] + jnp.dot(m_ref[...], wo_ref[...], preferred_element_type=F32)
    hm = _rms(x1, gmem_ref[...]).astype(BF16)
    q = jnp.dot(hm, wq_ref[...], preferred_element_type=F32) * (X_HEAD_DIM ** -0.5)
    q = q.astype(BF16)
    outs = []
    for h in range(X_HEADS):
        sl = slice(h * X_HEAD_DIM, (h + 1) * X_HEAD_DIM)
        s = lax.dot_general(q[:, sl], k_ref[:, sl], (((1,), (1,)), ((), ())), preferred_element_type=F32)
        p = jnp.exp(s - jnp.max(s, axis=-1, keepdims=True))
        l = jnp.sum(p, axis=-1, keepdims=True)
        o = jnp.dot(p.astype(BF16), v_ref[:, sl], preferred_element_type=F32) / l
        outs.append(o.astype(BF16))
    o = jnp.concatenate(outs, axis=-1)
    x2 = x1 + jnp.dot(o, wxo_ref[...], preferred_element_type=F32)
    xo_ref[...] = x2
    ho_ref[...] = _rms(x2, gffn_ref[...]).astype(ho_ref.dtype)


def mix_cross(merged, x, w_out, g_mem3, w_xq, mk, mv, w_xo, g_ffn3, layer, tm=512):
    M, D = merged.shape
    n = X_HEADS * X_HEAD_DIM
    const = lambda shape: pl.BlockSpec(shape, lambda i: (0, 0))
    rows = pl.BlockSpec((tm, D), lambda i: (i, 0))
    once = pl.Buffered(1)
    weight = lambda k, c: pl.BlockSpec((None, k, c), lambda i: (layer, 0, 0), pipeline_mode=once)
    return pl.pallas_call(
        _mix_cross_kernel,
        grid=(M // tm,),
        in_specs=[rows, rows, weight(D, D), _layer_row(D, layer), weight(D, n),
                  const((N_MEM, n)), const((N_MEM, n)), weight(n, D), _layer_row(D, layer)],
        out_specs=[rows, rows],
        out_shape=[jax.ShapeDtypeStruct((M, D), F32), jax.ShapeDtypeStruct((M, D), BF16)],
        name="mix_cross",
        compiler_params=_cparams("arbitrary"),
    )(merged, x, w_out, g_mem3, w_xq, mk, mv, w_xo, g_ffn3)


def _mlp_kernel(h_ref, w1_ref, w2_ref, x_ref, g_ref, xo_ref, ho_ref):
    f = pl.program_id(1)

    @pl.when(f == 0)
    def _():
        xo_ref[...] = x_ref[...]

    a = jnp.dot(h_ref[...], w1_ref[...], preferred_element_type=F32)
    a = jnp.square(jnp.maximum(a, 0.0)).astype(BF16)
    xo_ref[...] += jnp.dot(a, w2_ref[...], preferred_element_type=F32)

    @pl.when(f == pl.num_programs(1) - 1)
    def _():
        ho_ref[...] = _rms(xo_ref[...], g_ref[...]).astype(ho_ref.dtype)


def mlp(h, w1, w2, layer, x, g3, g_layer, out_dtype, tm=512, tf=1024):
    M, D = h.shape
    F = w1.shape[2]
    rows = pl.BlockSpec((tm, D), lambda i, f: (i, 0))
    return pl.pallas_call(
        _mlp_kernel,
        grid=(M // tm, F // tf),
        in_specs=[rows,
                  pl.BlockSpec((None, D, tf), lambda i, f: (layer, 0, f)),
                  pl.BlockSpec((None, tf, D), lambda i, f: (layer, f, 0)),
                  rows,
                  _layer_row(D, g_layer)],
        out_specs=[rows, rows],
        out_shape=[jax.ShapeDtypeStruct((M, D), F32), jax.ShapeDtypeStruct((M, D), out_dtype)],
        name="mlp",
        compiler_params=_cparams("arbitrary", "arbitrary"),
    )(h, w1, w2, x, g3)


def _permute_q_weight(w_uq):
    L, K, _ = w_uq.shape
    half = QK_ROPE // 2
    w = w_uq.reshape(L, K, MLA_HEADS, QK_NOPE + QK_ROPE)
    pe = w[..., QK_NOPE:]
    w = jnp.concatenate([w, pe[..., half:], pe[..., :half]], axis=-1)
    return w.reshape(L, K, MLA_HEADS * QK_PAD)


def kernel(x, mem, positions, norm_mix_g, w_in, b_gate, conv_w, conv_b, conv_ln_g, conv_ln_b, w_conv_out, q_norm_g, w_uq, kv_norm_g, w_ukv, w_mla_out, w_out, norm_mem_g, mem_norm_g, w_xq, w_xkv, w_xo, norm_ffn_g, w_ff1, w_ff2, final_norm_g):
    B, S, D = x.shape
    assert (B, S, D) == (1, SEQ, D_MODEL)
    x2d = x.reshape(S, D)
    mem2d = mem.reshape(N_MEM, D)
    vec3 = lambda v: v.reshape(v.shape[0], 1, v.shape[1])

    inv_freq = 1.0 / (ROPE_THETA ** (jnp.arange(0, QK_ROPE, 2, dtype=F32) / QK_ROPE))
    cos, sin = rope_tables(positions.reshape(S, 1), jnp.tile(inv_freq, 4).reshape(1, LANES))

    w_in_lo = jnp.swapaxes(w_in, 1, 2).astype(BF16)
    w_q = _permute_q_weight(w_uq).astype(BF16)
    w_kv, w_co, w_mo, w_o = (w.astype(BF16) for w in (w_ukv, w_conv_out, w_mla_out, w_out))
    w_q_x, w_kv_x, w_o_x = (w.astype(BF16) for w in (w_xq, w_xkv, w_xo))
    w_1, w_2 = w_ff1.astype(BF16), w_ff2.astype(BF16)

    g_mix, g_mem, g_memn, g_ffn = vec3(norm_mix_g), vec3(norm_mem_g), vec3(mem_norm_g), vec3(norm_ffn_g)
    g_q, g_kv, b_g = vec3(q_norm_g), vec3(kv_norm_g), vec3(b_gate)
    c_b, ln_g, ln_b = vec3(conv_b), vec3(conv_ln_g), vec3(conv_ln_b)
    g_final = final_norm_g.reshape(1, 1, D)

    h = rms_norm_rows(x2d, g_mix, 0, BF16)
    xcur = x2d
    for l in range(DEPTH):
        (hglu,) = matmul("glu_proj", [h], [(0, w_in_lo, l, COL_GLU_A, True), (0, w_in_lo, l, COL_GLU_G, True)],
                         [], [F32], _epi_glu, D_CONV, 1024, 1024)

        hc = conv_ln_silu(hglu, conv_w, c_b, ln_g, ln_b, l)
        (gy,) = matmul("conv_out_gated", [hc, h], [(0, w_co, l, 0, False), (1, w_in_lo, l, COL_GATE, True)],
                       [("row", b_g, l, 0)], [BF16], _epi_gated, D, 1024, 512)

        q, k, v = mla_proj(h, w_in_lo, w_q, w_kv, l, g_q, g_kv, cos, sin)
        o = mla_attention(q, k, v)
        (merged,) = matmul("mla_out_gated_merge", [o, h],
                           [(0, w_mo, l, 0, False), (1, w_in_lo, l, COL_GATE + D, True)],
                           [("row", b_g, l, D), ("tile", gy, 0)], [BF16], _epi_gated_merge, D, 1024, 512)

        mk, mv = mem_kv(mem2d, g_memn, w_kv_x, l)
        xcur, hf = mix_cross(merged, xcur, w_o, g_mem, w_q_x, mk, mv, w_o_x, g_ffn, l)

        last = l == DEPTH - 1
        xcur, h = mlp(hf, w_1, w_2, l, xcur, g_final if last else g_mix, 0 if last else l + 1,
                      F32 if last else BF16)
    return h.reshape(B, S, D)
```

```python
import functools
import math

import jax
import jax.numpy as jnp
from jax import lax
from jax.experimental import pallas as pl
from jax.experimental.pallas import tpu as pltpu

D_MODEL = 2048
SEQ = 8192
DEPTH = 2
CHUNK = 64
N_MEM = 256
EPS = 1e-6
D_CONV = D_MODEL
CONV_K = 31
MLA_HEADS = 16
Q_LORA = 512
KV_LORA = 512
QK_NOPE = 128
QK_ROPE = 64
V_DIM = 128
ROPE_THETA = 10000.0
X_HEADS = 4
X_HEAD_DIM = 128
D_FF = 4 * D_MODEL

LANES = 128
SUBLANES = 8
QK_PAD = 256
CONV_HALO = 32
NEG_BIG = -1e30
VMEM_LIMIT = 56 * 1024 * 1024

COL_GLU_A = 0
COL_GLU_G = D_CONV
COL_LATENT = 2 * D_CONV
COL_KPE = COL_LATENT + Q_LORA + KV_LORA
COL_GATE = COL_KPE + QK_ROPE

F32 = jnp.float32
BF16 = jnp.bfloat16


def _cparams(*sem):
    return pltpu.CompilerParams(dimension_semantics=sem, vmem_limit_bytes=VMEM_LIMIT)


def _rms(x, g):
    return x * lax.rsqrt(jnp.mean(x * x, axis=-1, keepdims=True) + EPS) * g


def _layer_row(n, layer, col0=0):
    return pl.BlockSpec((None, 1, n), lambda *_: (layer, 0, col0 // n))


def _rope_table_kernel(pos_ref, freq_ref, cos_ref, sin_ref):
    ang = pos_ref[...].astype(F32) * freq_ref[...]
    lane = lax.broadcasted_iota(jnp.int32, ang.shape, 1)
    s = jnp.sin(ang)
    cos_ref[...] = jnp.where(lane < QK_ROPE, jnp.cos(ang), 0.0)
    sin_ref[...] = jnp.where(lane < QK_ROPE // 2, -s, jnp.where(lane < QK_ROPE, s, 0.0))


def rope_tables(pos_col, freq_row):
    S = pos_col.shape[0]
    tm = 1024
    tab = pl.BlockSpec((tm, LANES), lambda i: (i, 0))
    return pl.pallas_call(
        _rope_table_kernel,
        grid=(S // tm,),
        in_specs=[pl.BlockSpec((tm, 1), lambda i: (i, 0)),
                  pl.BlockSpec((1, LANES), lambda i: (0, 0))],
        out_specs=[tab, tab],
        out_shape=[jax.ShapeDtypeStruct((S, LANES), F32)] * 2,
        name="rope_table",
        compiler_params=_cparams("arbitrary"),
    )(pos_col, freq_row)


def _norm_kernel(x_ref, g_ref, o_ref):
    o_ref[...] = _rms(x_ref[...], g_ref[...]).astype(o_ref.dtype)


def rms_norm_rows(x, g3, layer, out_dtype, tm=512):
    M, D = x.shape
    return pl.pallas_call(
        _norm_kernel,
        grid=(M // tm,),
        in_specs=[pl.BlockSpec((tm, D), lambda i: (i, 0)), _layer_row(D, layer)],
        out_specs=pl.BlockSpec((tm, D), lambda i: (i, 0)),
        out_shape=jax.ShapeDtypeStruct((M, D), out_dtype),
        name="rms_norm",
        compiler_params=_cparams("arbitrary"),
    )(x, g3)


def _dot_nt(a, w_t):
    return lax.dot_general(a, w_t, (((1,), (1,)), ((), ())), preferred_element_type=F32)


def _mm_kernel(*refs, a_index, w_transposed, n_a, n_extra, epilogue):
    n_b = len(a_index)
    acts = [r[...] for r in refs[:n_a]]
    accs = [_dot_nt(acts[ai], b[0]) if t else jnp.dot(acts[ai], b[...], preferred_element_type=F32)
            for ai, t, b in zip(a_index, w_transposed, refs[n_a:n_a + n_b])]
    extras = [e[...] for e in refs[n_a + n_b:n_a + n_b + n_extra]]
    outs = epilogue(accs, extras)
    for o_ref, o in zip(refs[n_a + n_b + n_extra:], outs):
        o_ref[...] = o.astype(o_ref.dtype)


def matmul(name, acts, weights, extras, out_dtypes, epilogue, n_out, tm, tn):
    M = acts[0].shape[0]
    in_specs = [pl.BlockSpec((tm, a.shape[1]), lambda i, j: (i, 0)) for a in acts]
    operands = list(acts)
    for _, w, layer, col0, transposed in weights:
        if transposed:
            in_specs.append(pl.BlockSpec((pl.Element(1), pl.Element(tn), pl.Element(w.shape[2])),
                                         lambda i, j, layer=layer, col0=col0:
                                         (layer, pl.multiple_of(col0 + j * tn, LANES // 2), 0)))
        else:
            in_specs.append(pl.BlockSpec((None, w.shape[1], tn),
                                         lambda i, j, layer=layer, off=col0 // tn: (layer, 0, j + off)))
        operands.append(w)
    for e in extras:
        if e[0] == "row":
            _, arr, layer, col0 = e
            in_specs.append(pl.BlockSpec((None, 1, tn),
                                         lambda i, j, layer=layer, off=col0 // tn: (layer, 0, j + off)))
        else:
            _, arr, col0 = e
            in_specs.append(pl.BlockSpec((tm, tn), lambda i, j, off=col0 // tn: (i, j + off)))
        operands.append(arr)
    return pl.pallas_call(
        functools.partial(_mm_kernel, a_index=tuple(w[0] for w in weights),
                          w_transposed=tuple(w[4] for w in weights), n_a=len(acts),
                          n_extra=len(extras), epilogue=epilogue),
        grid=(M // tm, n_out // tn),
        in_specs=in_specs,
        out_specs=[pl.BlockSpec((tm, tn), lambda i, j: (i, j)) for _ in out_dtypes],
        out_shape=[jax.ShapeDtypeStruct((M, n_out), dt) for dt in out_dtypes],
        name=name,
        compiler_params=_cparams("arbitrary", "arbitrary"),
    )(*operands)


def _epi_glu(accs, extras):
    return [accs[0] * jax.nn.sigmoid(accs[1])]


def _epi_gated(accs, extras):
    return [accs[0] * jax.nn.sigmoid(accs[1] + extras[0])]


def _epi_gated_merge(accs, extras):
    return [extras[1] + accs[0] * jax.nn.sigmoid(accs[1] + extras[0])]


def _mla_proj_kernel(h_ref, wlat_ref, wpe_ref, wq_ref, wkv_ref, qg_ref, kvg_ref, cos_ref, sin_ref,
                     q_ref, k_ref, v_ref):
    h = h_ref[...]
    cos, sin = cos_ref[...], sin_ref[...]
    z = _dot_nt(h, wlat_ref[...])
    cq = _rms(z[:, :Q_LORA], qg_ref[...]).astype(BF16)
    ckv = _rms(z[:, Q_LORA:], kvg_ref[...]).astype(BF16)
    pe = _dot_nt(h, wpe_ref[...])
    half = QK_ROPE // 2
    lane = lax.broadcasted_iota(jnp.int32, pe.shape, 1)
    swapped = jnp.where(lane < half, pltpu.roll(pe, LANES - half, 1), pltpu.roll(pe, half, 1))
    kpe = jnp.where(lane < QK_ROPE, pe * cos + swapped * sin, 0.0).astype(k_ref.dtype)

    scale = (QK_NOPE + QK_ROPE) ** -0.5 * math.log2(math.e)
    kv_width = QK_NOPE + V_DIM
    for hd in range(MLA_HEADS):
        blk = jnp.dot(cq, wq_ref[:, hd * QK_PAD:(hd + 1) * QK_PAD], preferred_element_type=F32)
        q_ref[hd, :, 0:QK_NOPE] = (blk[:, :QK_NOPE] * scale).astype(q_ref.dtype)
        qpe = blk[:, QK_NOPE:]
        roped = qpe * cos + pltpu.roll(qpe, QK_ROPE, 1) * sin
        q_ref[hd, :, QK_NOPE:QK_PAD] = (roped * scale).astype(q_ref.dtype)

        blk = jnp.dot(ckv, wkv_ref[:, hd * kv_width:(hd + 1) * kv_width], preferred_element_type=F32)
        k_ref[hd, :, 0:QK_NOPE] = blk[:, :QK_NOPE].astype(k_ref.dtype)
        k_ref[hd, :, QK_NOPE:QK_PAD] = kpe
        v_ref[:, hd * V_DIM:(hd + 1) * V_DIM] = blk[:, QK_NOPE:].astype(v_ref.dtype)


def mla_proj(h, w_in_lo, w_q, w_kv, layer, qg3, kvg3, cos, sin, tm=512):
    M, K = h.shape
    n_lat = Q_LORA + KV_LORA
    rows = lambda n: pl.BlockSpec((tm, n), lambda i: (i, 0))
    once = pl.Buffered(1)
    heads = pl.BlockSpec((MLA_HEADS, tm, QK_PAD), lambda i: (0, i, 0))
    return pl.pallas_call(
        _mla_proj_kernel,
        grid=(M // tm,),
        in_specs=[rows(K),
                  pl.BlockSpec((None, n_lat, K), lambda i: (layer, COL_LATENT // n_lat, 0), pipeline_mode=once),
                  pl.BlockSpec((None, LANES, K), lambda i: (layer, COL_KPE // LANES, 0), pipeline_mode=once),
                  pl.BlockSpec((None, Q_LORA, MLA_HEADS * QK_PAD), lambda i: (layer, 0, 0), pipeline_mode=once),
                  pl.BlockSpec((None, KV_LORA, MLA_HEADS * (QK_NOPE + V_DIM)), lambda i: (layer, 0, 0),
                               pipeline_mode=once),
                  _layer_row(Q_LORA, layer), _layer_row(KV_LORA, layer),
                  rows(LANES), rows(LANES)],
        out_specs=[heads, heads, rows(MLA_HEADS * V_DIM)],
        out_shape=[jax.ShapeDtypeStruct((MLA_HEADS, M, QK_PAD), BF16),
                   jax.ShapeDtypeStruct((MLA_HEADS, M, QK_PAD), BF16),
                   jax.ShapeDtypeStruct((M, MLA_HEADS * V_DIM), BF16)],
        name="mla_proj",
        compiler_params=_cparams("arbitrary"),
    )(h, w_in_lo, w_in_lo, w_q, w_kv, qg3, kvg3, cos, sin)


ATTN_RB = CHUNK
ATTN_UNROLLS = (12, 8, 4, 2)


def _attn_kernel(q_ref, k_ref, v_ref, o_ref, vone_ref, *set_refs, tq):
    vone_ref[:, 0:V_DIM] = v_ref[...]
    vone_ref[:, V_DIM:] = jnp.ones((vone_ref.shape[0], V_DIM), vone_ref.dtype)
    n_set = len(set_refs) // 2
    sets = [_AttnBlock(q_ref, k_ref, vone_ref, o_ref, set_refs[i * n_set:(i + 1) * n_set], tq, i)
            for i in range(2)]
    n_q = q_ref.shape[1] // tq
    assert n_q % 2 == 0

    sets[0].first_block()
    sets[1].fill(jnp.int32(1))

    def query_block_pair(qq, carry):
        qa, qb = 2 * qq, 2 * qq + 1
        sets[1].drain(qa - 1)
        sets[0].fill(qa)
        sets[0].steady(qa)
        sets[0].drain(qa)
        sets[1].fill(qb)
        sets[1].steady(qb)
        return carry

    lax.fori_loop(1, n_q // 2, query_block_pair, 0)
    sets[1].drain(jnp.int32(n_q - 1))


class _AttnBlock:
    def __init__(self, q_ref, k_ref, v_ref, o_ref, refs, tq, par):
        s0, s1, p0, p1, a0, a1, self.m_ref, self.acc_ref = refs
        self.s_refs, self.p_refs, self.a_refs = (s0, s1), (p0, p1), (a0, a1)
        self.q_ref, self.k_ref, self.v_ref, self.o_ref = q_ref, k_ref, v_ref, o_ref
        self.tq, self.par = tq, par

    def _q_rows(self, qi):
        return pl.ds(pl.multiple_of(qi * self.tq, self.tq), self.tq)

    def _init(self):
        self.m_ref[...] = jnp.full(self.m_ref.shape, NEG_BIG, F32)
        self.acc_ref[...] = jnp.zeros(self.acc_ref.shape, F32)

    def scores(self, qi, j, par):
        start = pl.multiple_of(j * self.tq, self.tq)
        kb = self.k_ref[0, pl.ds(start, self.tq), :]
        self.s_refs[par][...] = lax.dot_general(self.q_ref[0, self._q_rows(qi), :], kb,
                                                (((1,), (1,)), ((), ())), preferred_element_type=F32)

    def softmax(self, par, masked):
        s_ref, p_ref, a_ref, m_ref = self.s_refs[par], self.p_refs[par], self.a_refs[par], self.m_ref
        for r in range(self.tq // ATTN_RB):
            rows = slice(r * ATTN_RB, (r + 1) * ATTN_RB)
            cols = []
            for c in range(self.tq // LANES):
                first_chunk = (c * LANES) // CHUNK
                if masked and first_chunk > r:
                    cols.append(None)
                    continue
                sc = s_ref[rows, c * LANES:(c + 1) * LANES]
                if masked and first_chunk == r:
                    lane = lax.broadcasted_iota(jnp.int32, sc.shape, 1)
                    sc = jnp.where(lane < CHUNK, sc, NEG_BIG)
                cols.append(sc)
            mx = functools.reduce(jnp.maximum, [sc for sc in cols if sc is not None])
            m_prev = m_ref[rows, :]
            m_new = jnp.maximum(m_prev, jnp.max(mx, axis=-1, keepdims=True))
            m_ref[rows, :] = m_new
            a_ref[rows, :] = jnp.exp2(m_prev - m_new)
            for c, sc in enumerate(cols):
                pc = jnp.zeros((ATTN_RB, LANES), BF16) if sc is None else jnp.exp2(sc - m_new).astype(BF16)
                p_ref[rows, c * LANES:(c + 1) * LANES] = pc

    def values(self, j, par):
        start = pl.multiple_of(j * self.tq, self.tq)
        vb = self.v_ref[pl.ds(start, self.tq), :]
        alpha = self.a_refs[par][...]
        self.acc_ref[...] = (jnp.concatenate([alpha, alpha], axis=1) * self.acc_ref[...]
                             + jnp.dot(self.p_refs[par][...], vb, preferred_element_type=F32))

    def _write(self, qi):
        acc = self.acc_ref
        self.o_ref[self._q_rows(qi), :] = (acc[:, 0:V_DIM] / acc[:, V_DIM:]).astype(self.o_ref.dtype)

    def first_block(self):
        qi = jnp.int32(0)
        self._init()
        self.scores(qi, qi, 0)
        self.softmax(0, True)
        self.values(qi, 0)
        self._write(qi)

    def fill(self, qi):
        self._init()
        self.scores(qi, 0, 0)
        self.scores(qi, 1, 1)
        self.softmax(0, False)

    def _tick(self, qi, t, par):
        self.scores(qi, t, par)
        self.softmax(1 - par, False)
        self.values(t - 2, par)

    def steady(self, qi):
        n_left = jnp.maximum(qi - 1, 0)
        t0 = 2
        for unroll in ATTN_UNROLLS:

            def unrolled(i, carry, t0=t0, unroll=unroll):
                for u in range(unroll):
                    self._tick(qi, t0 + unroll * i + u, u % 2)
                return carry

            lax.fori_loop(0, n_left // unroll, unrolled, 0)
            t0 = t0 + (n_left // unroll) * unroll
            n_left = n_left % unroll

        @pl.when(n_left == 1)
        def _():
            self._tick(qi, qi, 0)

    def drain(self, qi):
        self.softmax(self.par, True)
        self.values(qi - 1, 1 - self.par)
        self.values(qi, self.par)
        self._write(qi)


def mla_attention(q, k, v, tq=512):
    H, S, _ = q.shape
    buffer_set = ([pltpu.VMEM((tq, tq), F32)] * 2
                  + [pltpu.VMEM((tq, tq), BF16)] * 2
                  + [pltpu.VMEM((tq, LANES), F32)] * 2
                  + [pltpu.VMEM((tq, LANES), F32),
                     pltpu.VMEM((tq, 2 * V_DIM), F32)])
    return pl.pallas_call(
        functools.partial(_attn_kernel, tq=tq),
        grid=(H,),
        in_specs=[pl.BlockSpec((1, S, QK_PAD), lambda h: (h, 0, 0)),
                  pl.BlockSpec((1, S, QK_PAD), lambda h: (h, 0, 0)),
                  pl.BlockSpec((S, V_DIM), lambda h: (0, h))],
        out_specs=pl.BlockSpec((S, V_DIM), lambda h: (0, h)),
        out_shape=jax.ShapeDtypeStruct((S, H * V_DIM), BF16),
        scratch_shapes=[pltpu.VMEM((S, 2 * V_DIM), BF16)] + buffer_set + buffer_set,
        name="mla_attention",
        compiler_params=_cparams("arbitrary"),
    )(q, k, v)


CONV_RB = 64
CONV_CB = LANES


def _conv_kernel(cur_ref, halo_ref, w_ref, b_ref, g_ref, beta_ref, o_ref, buf_ref, sh_ref, acc_ref, wb_ref, *, tm):
    i = pl.program_id(0)

    @pl.when(i == 0)
    def _():
        for k in range(CONV_K):
            wb_ref[k] = jnp.broadcast_to(w_ref[k:k + 1, :], (SUBLANES, D_CONV))

    buf_ref[0:CONV_HALO, :] = jnp.where(i > 0, halo_ref[...], 0.0)
    buf_ref[CONV_HALO:CONV_HALO + tm, :] = cur_ref[...]
    n_buf = tm + CONV_HALO
    n_sh = n_buf - SUBLANES
    for c0 in range(0, D_CONV, LANES):
        x = buf_ref[:, c0:c0 + LANES]
        for b in range(1, SUBLANES):
            sh_ref[b - 1, :, c0:c0 + LANES] = pltpu.roll(x, n_buf - b, 0)[0:n_sh, :]
    off = CONV_HALO - (CONV_K - 1)
    for c0 in range(0, D_CONV, CONV_CB):
        cols = slice(c0, c0 + CONV_CB)
        w_all = wb_ref[:, :, cols]
        bias = jnp.broadcast_to(b_ref[:, cols], (CONV_RB, CONV_CB))

        def row_block(rb, carry, cols=cols, w_all=w_all, bias=bias):
            r0 = pl.multiple_of(rb * CONV_RB, CONV_RB)
            acc = bias
            for b in range(SUBLANES):
                taps = [k for k in range(CONV_K) if (off + k) % SUBLANES == b]
                a_lo, a_hi = (off + taps[0]) // SUBLANES, (off + taps[-1]) // SUBLANES
                rows = pl.ds(r0 + a_lo * SUBLANES, (a_hi - a_lo) * SUBLANES + CONV_RB)
                slab = buf_ref[rows, cols] if b == 0 else sh_ref[b - 1, rows, cols]
                for k in taps:
                    s0 = ((off + k) // SUBLANES - a_lo) * SUBLANES
                    wk = jnp.concatenate([w_all[k]] * (CONV_RB // SUBLANES), axis=0)
                    acc = acc + slab[s0:s0 + CONV_RB, :] * wk
            acc_ref[pl.ds(r0, CONV_RB), cols] = acc
            return carry

        lax.fori_loop(0, tm // CONV_RB, row_block, 0)
    y = acc_ref[...]
    mu = jnp.mean(y, axis=-1, keepdims=True)
    yc = y - mu
    yn = yc * lax.rsqrt(jnp.mean(yc * yc, axis=-1, keepdims=True) + EPS) * g_ref[...] + beta_ref[...]
    o_ref[...] = (yn * jax.nn.sigmoid(yn)).astype(o_ref.dtype)


def conv_ln_silu(hglu, conv_w, conv_b3, ln_g3, ln_b3, layer, tm=256):
    S, D = hglu.shape
    row = _layer_row(D, layer)
    return pl.pallas_call(
        functools.partial(_conv_kernel, tm=tm),
        grid=(S // tm,),
        in_specs=[pl.BlockSpec((tm, D), lambda i: (i, 0)),
                  pl.BlockSpec((CONV_HALO, D), lambda i: (jnp.maximum(i * (tm // CONV_HALO) - 1, 0), 0)),
                  pl.BlockSpec((None, CONV_K, D), lambda i: (layer, 0, 0)),
                  row, row, row],
        out_specs=pl.BlockSpec((tm, D), lambda i: (i, 0)),
        out_shape=jax.ShapeDtypeStruct((S, D), BF16),
        scratch_shapes=[pltpu.VMEM((CONV_HALO + tm, D), F32),
                        pltpu.VMEM((SUBLANES - 1, CONV_HALO + tm - SUBLANES, D), F32),
                        pltpu.VMEM((tm, D), F32),
                        pltpu.VMEM((CONV_K, SUBLANES, D), F32)],
        name="conv_ln_silu",
        compiler_params=_cparams("arbitrary"),
    )(hglu, hglu, conv_w, conv_b3, ln_g3, ln_b3)


def _memkv_kernel(mem_ref, g_ref, w_ref, k_ref, v_ref):
    mn = _rms(mem_ref[...], g_ref[...]).astype(BF16)
    kv = jnp.dot(mn, w_ref[...], preferred_element_type=F32)
    n = X_HEADS * X_HEAD_DIM
    k_ref[...] = kv[:, :n].astype(k_ref.dtype)
    v_ref[...] = kv[:, n:].astype(v_ref.dtype)


def mem_kv(mem, g3, w, layer):
    n = X_HEADS * X_HEAD_DIM
    D = mem.shape[1]
    full = lambda shape: pl.BlockSpec(shape, lambda i: (0, 0))
    return pl.pallas_call(
        _memkv_kernel,
        grid=(1,),
        in_specs=[full(mem.shape), _layer_row(D, layer),
                  pl.BlockSpec((None, D, 2 * n), lambda i: (layer, 0, 0))],
        out_specs=[full((N_MEM, n)), full((N_MEM, n))],
        out_shape=[jax.ShapeDtypeStruct((N_MEM, n), BF16)] * 2,
        name="mem_kv",
        compiler_params=_cparams("arbitrary"),
    )(mem, g3, w)


def _mix_cross_kernel(m_ref, x_ref, wo_ref, gmem_ref, wq_ref, k_ref, v_ref, wxo_ref, gffn_ref, xo_ref, ho_ref):
    x1 = x_ref[...] + jnp.dot(m_ref[...], wo_ref[...], preferred_element_type=F32)
    hm = _rms(x1, gmem_ref[...]).astype(BF16)
    q = jnp.dot(hm, wq_ref[...], preferred_element_type=F32) * (X_HEAD_DIM ** -0.5)
    q = q.astype(BF16)
    outs = []
    for h in range(X_HEADS):
        sl = slice(h * X_HEAD_DIM, (h + 1) * X_HEAD_DIM)
        s = lax.dot_general(q[:, sl], k_ref[:, sl], (((1,), (1,)), ((), ())), preferred_element_type=F32)
        p = jnp.exp(s - jnp.max(s, axis=-1, keepdims=True))
        l = jnp.sum(p, axis=-1, keepdims=True)
        o = jnp.dot(p.astype(BF16), v_ref[:, sl], preferred_element_type=F32) / l
        outs.append(o.astype(BF16))
    o = jnp.concatenate(outs, axis=-1)
    x2 = x1 + jnp.dot(o, wxo_ref[...], preferred_element_type=F32)
    xo_ref[...] = x2
    ho_ref[...] = _rms(x2, gffn_ref[...]).astype(ho_ref.dtype)


def mix_cross(merged, x, w_out, g_mem3, w_xq, mk, mv, w_xo, g_ffn3, layer, tm=512):
    M, D = merged.shape
    n = X_HEADS * X_HEAD_DIM
    const = lambda shape: pl.BlockSpec(shape, lambda i: (0, 0))
    rows = pl.BlockSpec((tm, D), lambda i: (i, 0))
    once = pl.Buffered(1)
    weight = lambda k, c: pl.BlockSpec((None, k, c), lambda i: (layer, 0, 0), pipeline_mode=once)
    return pl.pallas_call(
        _mix_cross_kernel,
        grid=(M // tm,),
        in_specs=[rows, rows, weight(D, D), _layer_row(D, layer), weight(D, n),
                  const((N_MEM, n)), const((N_MEM, n)), weight(n, D), _layer_row(D, layer)],
        out_specs=[rows, rows],
        out_shape=[jax.ShapeDtypeStruct((M, D), F32), jax.ShapeDtypeStruct((M, D), BF16)],
        name="mix_cross",
        compiler_params=_cparams("arbitrary"),
    )(merged, x, w_out, g_mem3, w_xq, mk, mv, w_xo, g_ffn3)


def _mlp_kernel(h_ref, w1_ref, w2_ref, x_ref, g_ref, xo_ref, ho_ref):
    f = pl.program_id(1)

    @pl.when(f == 0)
    def _():
        xo_ref[...] = x_ref[...]

    a = jnp.dot(h_ref[...], w1_ref[...], preferred_element_type=F32)
    a = jnp.square(jnp.maximum(a, 0.0)).astype(BF16)
    xo_ref[...] += jnp.dot(a, w2_ref[...], preferred_element_type=F32)

    @pl.when(f == pl.num_programs(1) - 1)
    def _():
        ho_ref[...] = _rms(xo_ref[...], g_ref[...]).astype(ho_ref.dtype)


def mlp(h, w1, w2, layer, x, g3, g_layer, out_dtype, tm=512, tf=1024):
    M, D = h.shape
    F = w1.shape[2]
    rows = pl.BlockSpec((tm, D), lambda i, f: (i, 0))
    return pl.pallas_call(
        _mlp_kernel,
        grid=(M // tm, F // tf),
        in_specs=[rows,
                  pl.BlockSpec((None, D, tf), lambda i, f: (layer, 0, f)),
                  pl.BlockSpec((None, tf, D), lambda i, f: (layer, f, 0)),
                  rows,
                  _layer_row(D, g_layer)],
        out_specs=[rows, rows],
        out_shape=[jax.ShapeDtypeStruct((M, D), F32), jax.ShapeDtypeStruct((M, D), out_dtype)],
        name="mlp",
        compiler_params=_cparams("arbitrary", "arbitrary"),
    )(h, w1, w2, x, g3)


def _permute_q_weight(w_uq):
    L, K, _ = w_uq.shape
    half = QK_ROPE // 2
    w = w_uq.reshape(L, K, MLA_HEADS, QK_NOPE + QK_ROPE)
    pe = w[..., QK_NOPE:]
    w = jnp.concatenate([w, pe[..., half:], pe[..., :half]], axis=-1)
    return w.reshape(L, K, MLA_HEADS * QK_PAD)


def kernel(x, mem, positions, norm_mix_g, w_in, b_gate, conv_w, conv_b, conv_ln_g, conv_ln_b, w_conv_out, q_norm_g, w_uq, kv_norm_g, w_ukv, w_mla_out, w_out, norm_mem_g, mem_norm_g, w_xq, w_xkv, w_xo, norm_ffn_g, w_ff1, w_ff2, final_norm_g):
    B, S, D = x.shape
    assert (B, S, D) == (1, SEQ, D_MODEL)
    x2d = x.reshape(S, D)
    mem2d = mem.reshape(N_MEM, D)
    vec3 = lambda v: v.reshape(v.shape[0], 1, v.shape[1])

    inv_freq = 1.0 / (ROPE_THETA ** (jnp.arange(0, QK_ROPE, 2, dtype=F32) / QK_ROPE))
    cos, sin = rope_tables(positions.reshape(S, 1), jnp.tile(inv_freq, 4).reshape(1, LANES))

    w_in_lo = jnp.swapaxes(w_in, 1, 2).astype(BF16)
    w_q = _permute_q_weight(w_uq).astype(BF16)
    w_kv, w_co, w_mo, w_o = (w.astype(BF16) for w in (w_ukv, w_conv_out, w_mla_out, w_out))
    w_q_x, w_kv_x, w_o_x = (w.astype(BF16) for w in (w_xq, w_xkv, w_xo))
    w_1, w_2 = w_ff1.astype(BF16), w_ff2.astype(BF16)

    g_mix, g_mem, g_memn, g_ffn = vec3(norm_mix_g), vec3(norm_mem_g), vec3(mem_norm_g), vec3(norm_ffn_g)
    g_q, g_kv, b_g = vec3(q_norm_g), vec3(kv_norm_g), vec3(b_gate)
    c_b, ln_g, ln_b = vec3(conv_b), vec3(conv_ln_g), vec3(conv_ln_b)
    g_final = final_norm_g.reshape(1, 1, D)

    h = rms_norm_rows(x2d, g_mix, 0, BF16)
    xcur = x2d
    for l in range(DEPTH):
        (hglu,) = matmul("glu_proj", [h], [(0, w_in_lo, l, COL_GLU_A, True), (0, w_in_lo, l, COL_GLU_G, True)],
                         [], [F32], _epi_glu, D_CONV, 1024, 1024)

        hc = conv_ln_silu(hglu, conv_w, c_b, ln_g, ln_b, l)
        (gy,) = matmul("conv_out_gated", [hc, h], [(0, w_co, l, 0, False), (1, w_in_lo, l, COL_GATE, True)],
                       [("row", b_g, l, 0)], [BF16], _epi_gated, D, 1024, 512)

        q, k, v = mla_proj(h, w_in_lo, w_q, w_kv, l, g_q, g_kv, cos, sin)
        o = mla_attention(q, k, v)
        (merged,) = matmul("mla_out_gated_merge", [o, h],
                           [(0, w_mo, l, 0, False), (1, w_in_lo, l, COL_GATE + D, True)],
                           [("row", b_g, l, D), ("tile", gy, 0)], [BF16], _epi_gated_merge, D, 1024, 512)

        mk, mv = mem_kv(mem2d, g_memn, w_kv_x, l)
        xcur, hf = mix_cross(merged, xcur, w_o, g_mem, w_q_x, mk, mv, w_o_x, g_ffn, l)

        last = l == DEPTH - 1
        xcur, h = mlp(hf, w_1, w_2, l, xcur, g_final if last else g_mix, 0 if last else l + 1,
                      F32 if last else BF16)
    return h.reshape(B, S, D)
```

```python
import functools
import math

import jax
import jax.numpy as jnp
from jax import lax
from jax.experimental import pallas as pl
from jax.experimental.pallas import tpu as pltpu

D_MODEL = 2048
SEQ = 8192
DEPTH = 2
CHUNK = 64
N_MEM = 256
EPS = 1e-6
D_CONV = D_MODEL
CONV_K = 31
MLA_HEADS = 16
Q_LORA = 512
KV_LORA = 512
QK_NOPE = 128
QK_ROPE = 64
V_DIM = 128
ROPE_THETA = 10000.0
X_HEADS = 4
X_HEAD_DIM = 128
D_FF = 4 * D_MODEL

LANES = 128
SUBLANES = 8
QK_PAD = 256
CONV_HALO = 32
NEG_BIG = -1e30
VMEM_LIMIT = 56 * 1024 * 1024

COL_GLU_A = 0
COL_GLU_G = D_CONV
COL_LATENT = 2 * D_CONV
COL_KPE = COL_LATENT + Q_LORA + KV_LORA
COL_GATE = COL_KPE + QK_ROPE

F32 = jnp.float32
BF16 = jnp.bfloat16


def _cparams(*sem):
    return pltpu.CompilerParams(dimension_semantics=sem, vmem_limit_bytes=VMEM_LIMIT)


def _rms(x, g):
    return x * lax.rsqrt(jnp.mean(x * x, axis=-1, keepdims=True) + EPS) * g


def _layer_row(n, layer, col0=0):
    return pl.BlockSpec((None, 1, n), lambda *_: (layer, 0, col0 // n))


def _rope_table_kernel(pos_ref, freq_ref, cos_ref, sin_ref):
    ang = pos_ref[...].astype(F32) * freq_ref[...]
    lane = lax.broadcasted_iota(jnp.int32, ang.shape, 1)
    s = jnp.sin(ang)
    cos_ref[...] = jnp.where(lane < QK_ROPE, jnp.cos(ang), 0.0)
    sin_ref[...] = jnp.where(lane < QK_ROPE // 2, -s, jnp.where(lane < QK_ROPE, s, 0.0))


def rope_tables(pos_col, freq_row):
    S = pos_col.shape[0]
    tm = 1024
    tab = pl.BlockSpec((tm, LANES), lambda i: (i, 0))
    return pl.pallas_call(
        _rope_table_kernel,
        grid=(S // tm,),
        in_specs=[pl.BlockSpec((tm, 1), lambda i: (i, 0)),
                  pl.BlockSpec((1, LANES), lambda i: (0, 0))],
        out_specs=[tab, tab],
        out_shape=[jax.ShapeDtypeStruct((S, LANES), F32)] * 2,
        name="rope_table",
        compiler_params=_cparams("arbitrary"),
    )(pos_col, freq_row)


def _norm_kernel(x_ref, g_ref, o_ref):
    o_ref[...] = _rms(x_ref[...], g_ref[...]).astype(o_ref.dtype)


def rms_norm_rows(x, g3, layer, out_dtype, tm=512):
    M, D = x.shape
    return pl.pallas_call(
        _norm_kernel,
        grid=(M // tm,),
        in_specs=[pl.BlockSpec((tm, D), lambda i: (i, 0)), _layer_row(D, layer)],
        out_specs=pl.BlockSpec((tm, D), lambda i: (i, 0)),
        out_shape=jax.ShapeDtypeStruct((M, D), out_dtype),
        name="rms_norm",
        compiler_params=_cparams("arbitrary"),
    )(x, g3)


def _dot_nt(a, w_t):
    return lax.dot_general(a, w_t, (((1,), (1,)), ((), ())), preferred_element_type=F32)


def _mm_kernel(*refs, a_index, w_transposed, n_a, n_extra, epilogue):
    n_b = len(a_index)
    acts = [r[...] for r in refs[:n_a]]
    accs = [_dot_nt(acts[ai], b[0]) if t else jnp.dot(acts[ai], b[...], preferred_element_type=F32)
            for ai, t, b in zip(a_index, w_transposed, refs[n_a:n_a + n_b])]
    extras = [e[...] for e in refs[n_a + n_b:n_a + n_b + n_extra]]
    outs = epilogue(accs, extras)
    for o_ref, o in zip(refs[n_a + n_b + n_extra:], outs):
        o_ref[...] = o.astype(o_ref.dtype)


def matmul(name, acts, weights, extras, out_dtypes, epilogue, n_out, tm, tn):
    M = acts[0].shape[0]
    in_specs = [pl.BlockSpec((tm, a.shape[1]), lambda i, j: (i, 0)) for a in acts]
    operands = list(acts)
    for _, w, layer, col0, transposed in weights:
        if transposed:
            in_specs.append(pl.BlockSpec((pl.Element(1), pl.Element(tn), pl.Element(w.shape[2])),
                                         lambda i, j, layer=layer, col0=col0:
                                         (layer, pl.multiple_of(col0 + j * tn, LANES // 2), 0)))
        else:
            in_specs.append(pl.BlockSpec((None, w.shape[1], tn),
                                         lambda i, j, layer=layer, off=col0 // tn: (layer, 0, j + off)))
        operands.append(w)
    for e in extras:
        if e[0] == "row":
            _, arr, layer, col0 = e
            in_specs.append(pl.BlockSpec((None, 1, tn),
                                         lambda i, j, layer=layer, off=col0 // tn: (layer, 0, j + off)))
        else:
            _, arr, col0 = e
            in_specs.append(pl.BlockSpec((tm, tn), lambda i, j, off=col0 // tn: (i, j + off)))
        operands.append(arr)
    return pl.pallas_call(
        functools.partial(_mm_kernel, a_index=tuple(w[0] for w in weights),
                          w_transposed=tuple(w[4] for w in weights), n_a=len(acts),
                          n_extra=len(extras), epilogue=epilogue),
        grid=(M // tm, n_out // tn),
        in_specs=in_specs,
        out_specs=[pl.BlockSpec((tm, tn), lambda i, j: (i, j)) for _ in out_dtypes],
        out_shape=[jax.ShapeDtypeStruct((M, n_out), dt) for dt in out_dtypes],
        name=name,
        compiler_params=_cparams("arbitrary", "arbitrary"),
    )(*operands)


def _epi_glu(accs, extras):
    return [accs[0] * jax.nn.sigmoid(accs[1])]


def _epi_gated(accs, extras):
    return [accs[0] * jax.nn.sigmoid(accs[1] + extras[0])]


def _epi_gated_merge(accs, extras):
    return [extras[1] + accs[0] * jax.nn.sigmoid(accs[1] + extras[0])]


def _mla_proj_kernel(h_ref, wlat_ref, wpe_ref, wq_ref, wkv_ref, qg_ref, kvg_ref, cos_ref, sin_ref,
                     q_ref, k_ref, v_ref):
    h = h_ref[...]
    cos, sin = cos_ref[...], sin_ref[...]
    z = _dot_nt(h, wlat_ref[...])
    cq = _rms(z[:, :Q_LORA], qg_ref[...]).astype(BF16)
    ckv = _rms(z[:, Q_LORA:], kvg_ref[...]).astype(BF16)
    pe = _dot_nt(h, wpe_ref[...])
    half = QK_ROPE // 2
    lane = lax.broadcasted_iota(jnp.int32, pe.shape, 1)
    swapped = jnp.where(lane < half, pltpu.roll(pe, LANES - half, 1), pltpu.roll(pe, half, 1))
    kpe = jnp.where(lane < QK_ROPE, pe * cos + swapped * sin, 0.0).astype(k_ref.dtype)

    scale = (QK_NOPE + QK_ROPE) ** -0.5 * math.log2(math.e)
    kv_width = QK_NOPE + V_DIM
    for hd in range(MLA_HEADS):
        blk = jnp.dot(cq, wq_ref[:, hd * QK_PAD:(hd + 1) * QK_PAD], preferred_element_type=F32)
        q_ref[hd, :, 0:QK_NOPE] = (blk[:, :QK_NOPE] * scale).astype(q_ref.dtype)
        qpe = blk[:, QK_NOPE:]
        roped = qpe * cos + pltpu.roll(qpe, QK_ROPE, 1) * sin
        q_ref[hd, :, QK_NOPE:QK_PAD] = (roped * scale).astype(q_ref.dtype)

        blk = jnp.dot(ckv, wkv_ref[:, hd * kv_width:(hd + 1) * kv_width], preferred_element_type=F32)
        k_ref[hd, :, 0:QK_NOPE] = blk[:, :QK_NOPE].astype(k_ref.dtype)
        k_ref[hd, :, QK_NOPE:QK_PAD] = kpe
        v_ref[:, hd * V_DIM:(hd + 1) * V_DIM] = blk[:, QK_NOPE:].astype(v_ref.dtype)


def mla_proj(h, w_in_lo, w_q, w_kv, layer, qg3, kvg3, cos, sin, tm=512):
    M, K = h.shape
    n_lat = Q_LORA + KV_LORA
    rows = lambda n: pl.BlockSpec((tm, n), lambda i: (i, 0))
    once = pl.Buffered(1)
    heads = pl.BlockSpec((MLA_HEADS, tm, QK_PAD), lambda i: (0, i, 0))
    return pl.pallas_call(
        _mla_proj_kernel,
        grid=(M // tm,),
        in_specs=[rows(K),
                  pl.BlockSpec((None, n_lat, K), lambda i: (layer, COL_LATENT // n_lat, 0), pipeline_mode=once),
                  pl.BlockSpec((None, LANES, K), lambda i: (layer, COL_KPE // LANES, 0), pipeline_mode=once),
                  pl.BlockSpec((None, Q_LORA, MLA_HEADS * QK_PAD), lambda i: (layer, 0, 0), pipeline_mode=once),
                  pl.BlockSpec((None, KV_LORA, MLA_HEADS * (QK_NOPE + V_DIM)), lambda i: (layer, 0, 0),
                               pipeline_mode=once),
                  _layer_row(Q_LORA, layer), _layer_row(KV_LORA, layer),
                  rows(LANES), rows(LANES)],
        out_specs=[heads, heads, rows(MLA_HEADS * V_DIM)],
        out_shape=[jax.ShapeDtypeStruct((MLA_HEADS, M, QK_PAD), BF16),
                   jax.ShapeDtypeStruct((MLA_HEADS, M, QK_PAD), BF16),
                   jax.ShapeDtypeStruct((M, MLA_HEADS * V_DIM), BF16)],
        name="mla_proj",
        compiler_params=_cparams("arbitrary"),
    )(h, w_in_lo, w_in_lo, w_q, w_kv, qg3, kvg3, cos, sin)


ATTN_RB = CHUNK
ATTN_UNROLLS = (12, 8, 4, 2)


def _attn_kernel(q_ref, k_ref, v_ref, o_ref, vone_ref, *set_refs, tq):
    vone_ref[:, 0:V_DIM] = v_ref[...]
    vone_ref[:, V_DIM:] = jnp.ones((vone_ref.shape[0], V_DIM), vone_ref.dtype)
    n_set = len(set_refs) // 2
    sets = [_AttnBlock(q_ref, k_ref, vone_ref, o_ref, set_refs[i * n_set:(i + 1) * n_set], tq, i)
            for i in range(2)]
    n_q = q_ref.shape[1] // tq
    assert n_q % 2 == 0

    sets[0].first_block()
    sets[1].fill(jnp.int32(1))

    def query_block_pair(qq, carry):
        qa, qb = 2 * qq, 2 * qq + 1
        sets[1].drain(qa - 1)
        sets[0].fill(qa)
        sets[0].steady(qa)
        sets[0].drain(qa)
        sets[1].fill(qb)
        sets[1].steady(qb)
        return carry

    lax.fori_loop(1, n_q // 2, query_block_pair, 0)
    sets[1].drain(jnp.int32(n_q - 1))


class _AttnBlock:
    def __init__(self, q_ref, k_ref, v_ref, o_ref, refs, tq, par):
        s0, s1, p0, p1, a0, a1, self.m_ref, self.acc_ref = refs
        self.s_refs, self.p_refs, self.a_refs = (s0, s1), (p0, p1), (a0, a1)
        self.q_ref, self.k_ref, self.v_ref, self.o_ref = q_ref, k_ref, v_ref, o_ref
        self.tq, self.par = tq, par

    def _q_rows(self, qi):
        return pl.ds(pl.multiple_of(qi * self.tq, self.tq), self.tq)

    def _init(self):
        self.m_ref[...] = jnp.full(self.m_ref.shape, NEG_BIG, F32)
        self.acc_ref[...] = jnp.zeros(self.acc_ref.shape, F32)

    def scores(self, qi, j, par):
        start = pl.multiple_of(j * self.tq, self.tq)
        kb = self.k_ref[0, pl.ds(start, self.tq), :]
        self.s_refs[par][...] = lax.dot_general(self.q_ref[0, self._q_rows(qi), :], kb,
                                                (((1,), (1,)), ((), ())), preferred_element_type=F32)

    def softmax(self, par, masked):
        s_ref, p_ref, a_ref, m_ref = self.s_refs[par], self.p_refs[par], self.a_refs[par], self.m_ref
        for r in range(self.tq // ATTN_RB):
            rows = slice(r * ATTN_RB, (r + 1) * ATTN_RB)
            cols = []
            for c in range(self.tq // LANES):
                first_chunk = (c * LANES) // CHUNK
                if masked and first_chunk > r:
                    cols.append(None)
                    continue
                sc = s_ref[rows, c * LANES:(c + 1) * LANES]
                if masked and first_chunk == r:
                    lane = lax.broadcasted_iota(jnp.int32, sc.shape, 1)
                    sc = jnp.where(lane < CHUNK, sc, NEG_BIG)
                cols.append(sc)
            mx = functools.reduce(jnp.maximum, [sc for sc in cols if sc is not None])
            m_prev = m_ref[rows, :]
            m_new = jnp.maximum(m_prev, jnp.max(mx, axis=-1, keepdims=True))
            m_ref[rows, :] = m_new
            a_ref[rows, :] = jnp.exp2(m_prev - m_new)
            for c, sc in enumerate(cols):
                pc = jnp.zeros((ATTN_RB, LANES), BF16) if sc is None else jnp.exp2(sc - m_new).astype(BF16)
                p_ref[rows, c * LANES:(c + 1) * LANES] = pc

    def values(self, j, par):
        start = pl.multiple_of(j * self.tq, self.tq)
        vb = self.v_ref[pl.ds(start, self.tq), :]
        alpha = self.a_refs[par][...]
        self.acc_ref[...] = (jnp.concatenate([alpha, alpha], axis=1) * self.acc_ref[...]
                             + jnp.dot(self.p_refs[par][...], vb, preferred_element_type=F32))

    def _write(self, qi):
        acc = self.acc_ref
        self.o_ref[self._q_rows(qi), :] = (acc[:, 0:V_DIM] / acc[:, V_DIM:]).astype(self.o_ref.dtype)

    def first_block(self):
        qi = jnp.int32(0)
        self._init()
        self.scores(qi, qi, 0)
        self.softmax(0, True)
        self.values(qi, 0)
        self._write(qi)

    def fill(self, qi):
        self._init()
        self.scores(qi, 0, 0)
        self.scores(qi, 1, 1)
        self.softmax(0, False)
        if self.par == 0:
            self._tick(qi, 2, 0)

    def _tick(self, qi, t, par):
        self.scores(qi, t, par)
        self.softmax(1 - par, False)
        self.values(t - 2, par)

    def steady(self, qi):
        first = 3 if self.par == 0 else 2
        n_left = qi + 1 - first
        t0 = first
        for unroll in ATTN_UNROLLS:

            def unrolled(i, carry, t0=t0, unroll=unroll):
                for u in range(unroll):
                    self._tick(qi, t0 + unroll * i + u, (first + u) % 2)
                return carry

            lax.fori_loop(0, n_left // unroll, unrolled, 0)
            t0 = t0 + (n_left // unroll) * unroll
            n_left = n_left % unroll

    def drain(self, qi):
        self.softmax(self.par, True)
        self.values(qi - 1, 1 - self.par)
        self.values(qi, self.par)
        self._write(qi)


def mla_attention(q, k, v, tq=512):
    H, S, _ = q.shape
    buffer_set = ([pltpu.VMEM((tq, tq), F32)] * 2
                  + [pltpu.VMEM((tq, tq), BF16)] * 2
                  + [pltpu.VMEM((tq, LANES), F32)] * 2
                  + [pltpu.VMEM((tq, LANES), F32),
                     pltpu.VMEM((tq, 2 * V_DIM), F32)])
    return pl.pallas_call(
        functools.partial(_attn_kernel, tq=tq),
        grid=(H,),
        in_specs=[pl.BlockSpec((1, S, QK_PAD), lambda h: (h, 0, 0)),
                  pl.BlockSpec((1, S, QK_PAD), lambda h: (h, 0, 0)),
                  pl.BlockSpec((S, V_DIM), lambda h: (0, h))],
        out_specs=pl.BlockSpec((S, V_DIM), lambda h: (0, h)),
        out_shape=jax.ShapeDtypeStruct((S, H * V_DIM), BF16),
        scratch_shapes=[pltpu.VMEM((S, 2 * V_DIM), BF16)] + buffer_set + buffer_set,
        name="mla_attention",
        compiler_params=_cparams("arbitrary"),
    )(q, k, v)


CONV_RB = 64
CONV_CB = LANES


def _conv_kernel(cur_ref, halo_ref, w_ref, b_ref, g_ref, beta_ref, o_ref, buf_ref, sh_ref, acc_ref, wb_ref, *, tm):
    i = pl.program_id(0)

    @pl.when(i == 0)
    def _():
        for k in range(CONV_K):
            wb_ref[k] = jnp.broadcast_to(w_ref[k:k + 1, :], (SUBLANES, D_CONV))

    buf_ref[0:CONV_HALO, :] = jnp.where(i > 0, halo_ref[...], 0.0)
    buf_ref[CONV_HALO:CONV_HALO + tm, :] = cur_ref[...]
    n_buf = tm + CONV_HALO
    n_sh = n_buf - SUBLANES
    for c0 in range(0, D_CONV, LANES):
        x = buf_ref[:, c0:c0 + LANES]
        for b in range(1, SUBLANES):
            sh_ref[b - 1, :, c0:c0 + LANES] = pltpu.roll(x, n_buf - b, 0)[0:n_sh, :]
    off = CONV_HALO - (CONV_K - 1)
    for c0 in range(0, D_CONV, CONV_CB):
        cols = slice(c0, c0 + CONV_CB)
        w_all = wb_ref[:, :, cols]
        bias = jnp.broadcast_to(b_ref[:, cols], (CONV_RB, CONV_CB))

        def row_block(rb, carry, cols=cols, w_all=w_all, bias=bias):
            r0 = pl.multiple_of(rb * CONV_RB, CONV_RB)
            acc = bias
            for b in range(SUBLANES):
                taps = [k for k in range(CONV_K) if (off + k) % SUBLANES == b]
                a_lo, a_hi = (off + taps[0]) // SUBLANES, (off + taps[-1]) // SUBLANES
                rows = pl.ds(r0 + a_lo * SUBLANES, (a_hi - a_lo) * SUBLANES + CONV_RB)
                slab = buf_ref[rows, cols] if b == 0 else sh_ref[b - 1, rows, cols]
                for k in taps:
                    s0 = ((off + k) // SUBLANES - a_lo) * SUBLANES
                    wk = jnp.concatenate([w_all[k]] * (CONV_RB // SUBLANES), axis=0)
                    acc = acc + slab[s0:s0 + CONV_RB, :] * wk
            acc_ref[pl.ds(r0, CONV_RB), cols] = acc
            return carry

        lax.fori_loop(0, tm // CONV_RB, row_block, 0)
    y = acc_ref[...]
    mu = jnp.mean(y, axis=-1, keepdims=True)
    yc = y - mu
    yn = yc * lax.rsqrt(jnp.mean(yc * yc, axis=-1, keepdims=True) + EPS) * g_ref[...] + beta_ref[...]
    o_ref[...] = (yn * jax.nn.sigmoid(yn)).astype(o_ref.dtype)


def conv_ln_silu(hglu, conv_w, conv_b3, ln_g3, ln_b3, layer, tm=256):
    S, D = hglu.shape
    row = _layer_row(D, layer)
    return pl.pallas_call(
        functools.partial(_conv_kernel, tm=tm),
        grid=(S // tm,),
        in_specs=[pl.BlockSpec((tm, D), lambda i: (i, 0)),
                  pl.BlockSpec((CONV_HALO, D), lambda i: (jnp.maximum(i * (tm // CONV_HALO) - 1, 0), 0)),
                  pl.BlockSpec((None, CONV_K, D), lambda i: (layer, 0, 0)),
                  row, row, row],
        out_specs=pl.BlockSpec((tm, D), lambda i: (i, 0)),
        out_shape=jax.ShapeDtypeStruct((S, D), BF16),
        scratch_shapes=[pltpu.VMEM((CONV_HALO + tm, D), F32),
                        pltpu.VMEM((SUBLANES - 1, CONV_HALO + tm - SUBLANES, D), F32),
                        pltpu.VMEM((tm, D), F32),
                        pltpu.VMEM((CONV_K, SUBLANES, D), F32)],
        name="conv_ln_silu",
        compiler_params=_cparams("arbitrary"),
    )(hglu, hglu, conv_w, conv_b3, ln_g3, ln_b3)


def _memkv_kernel(mem_ref, g_ref, w_ref, k_ref, v_ref):
    mn = _rms(mem_ref[...], g_ref[...]).astype(BF16)
    kv = jnp.dot(mn, w_ref[...], preferred_element_type=F32)
    n = X_HEADS * X_HEAD_DIM
    k_ref[...] = kv[:, :n].astype(k_ref.dtype)
    v_ref[...] = kv[:, n:].astype(v_ref.dtype)


def mem_kv(mem, g3, w, layer):
    n = X_HEADS * X_HEAD_DIM
    D = mem.shape[1]
    full = lambda shape: pl.BlockSpec(shape, lambda i: (0, 0))
    return pl.pallas_call(
        _memkv_kernel,
        grid=(1,),
        in_specs=[full(mem.shape), _layer_row(D, layer),
                  pl.BlockSpec((None, D, 2 * n), lambda i: (layer, 0, 0))],
        out_specs=[full((N_MEM, n)), full((N_MEM, n))],
        out_shape=[jax.ShapeDtypeStruct((N_MEM, n), BF16)] * 2,
        name="mem_kv",
        compiler_params=_cparams("arbitrary"),
    )(mem, g3, w)


def _mix_cross_kernel(m_ref, x_ref, wo_ref, gmem_ref, wq_ref, k_ref, v_ref, wxo_ref, gffn_ref, xo_ref, ho_ref):
    x1 = x_ref[...] + jnp.dot(m_ref[...], wo_ref[...], preferred_element_type=F32)
    hm = _rms(x1, gmem_ref[...]).astype(BF16)
    q = jnp.dot(hm, wq_ref[...], preferred_element_type=F32) * (X_HEAD_DIM ** -0.5)
    q = q.astype(BF16)
    outs = []
    for h in range(X_HEADS):
        sl = slice(h * X_HEAD_DIM, (h + 1) * X_HEAD_DIM)
        s = lax.dot_general(q[:, sl], k_ref[:, sl], (((1,), (1,)), ((), ())), preferred_element_type=F32)
        p = jnp.exp(s - jnp.max(s, axis=-1, keepdims=True))
        l = jnp.sum(p, axis=-1, keepdims=True)
        o = jnp.dot(p.astype(BF16), v_ref[:, sl], preferred_element_type=F32) / l
        outs.append(o.astype(BF16))
    o = jnp.concatenate(outs, axis=-1)
    x2 = x1 + jnp.dot(o, wxo_ref[...], preferred_element_type=F32)
    xo_ref[...] = x2
    ho_ref[...] = _rms(x2, gffn_ref[...]).astype(ho_ref.dtype)


def mix_cross(merged, x, w_out, g_mem3, w_xq, mk, mv, w_xo, g_ffn3, layer, tm=512):
    M, D = merged.shape
    n = X_HEADS * X_HEAD_DIM
    const = lambda shape: pl.BlockSpec(shape, lambda i: (0, 0))
    rows = pl.BlockSpec((tm, D), lambda i: (i, 0))
    once = pl.Buffered(1)
    weight = lambda k, c: pl.BlockSpec((None, k, c), lambda i: (layer, 0, 0), pipeline_mode=once)
    return pl.pallas_call(
        _mix_cross_kernel,
        grid=(M // tm,),
        in_specs=[rows, rows, weight(D, D), _layer_row(D, layer), weight(D, n),
                  const((N_MEM, n)), const((N_MEM, n)), weight(n, D), _layer_row(D, layer)],
        out_specs=[rows, rows],
        out_shape=[jax.ShapeDtypeStruct((M, D), F32), jax.ShapeDtypeStruct((M, D), BF16)],
        name="mix_cross",
        compiler_params=_cparams("arbitrary"),
    )(merged, x, w_out, g_mem3, w_xq, mk, mv, w_xo, g_ffn3)


def _mlp_kernel(h_ref, w1_ref, w2_ref, x_ref, g_ref, xo_ref, ho_ref):
    f = pl.program_id(1)

    @pl.when(f == 0)
    def _():
        xo_ref[...] = x_ref[...]

    a = jnp.dot(h_ref[...], w1_ref[...], preferred_element_type=F32)
    a = jnp.square(jnp.maximum(a, 0.0)).astype(BF16)
    xo_ref[...] += jnp.dot(a, w2_ref[...], preferred_element_type=F32)

    @pl.when(f == pl.num_programs(1) - 1)
    def _():
        ho_ref[...] = _rms(xo_ref[...], g_ref[...]).astype(ho_ref.dtype)


def mlp(h, w1, w2, layer, x, g3, g_layer, out_dtype, tm=512, tf=1024):
    M, D = h.shape
    F = w1.shape[2]
    rows = pl.BlockSpec((tm, D), lambda i, f: (i, 0))
    return pl.pallas_call(
        _mlp_kernel,
        grid=(M // tm, F // tf),
        in_specs=[rows,
                  pl.BlockSpec((None, D, tf), lambda i, f: (layer, 0, f)),
                  pl.BlockSpec((None, tf, D), lambda i, f: (layer, f, 0)),
                  rows,
                  _layer_row(D, g_layer)],
        out_specs=[rows, rows],
        out_shape=[jax.ShapeDtypeStruct((M, D), F32), jax.ShapeDtypeStruct((M, D), out_dtype)],
        name="mlp",
        compiler_params=_cparams("arbitrary", "arbitrary"),
    )(h, w1, w2, x, g3)


def _permute_q_weight(w_uq):
    L, K, _ = w_uq.shape
    half = QK_ROPE // 2
    w = w_uq.reshape(L, K, MLA_HEADS, QK_NOPE + QK_ROPE)
    pe = w[..., QK_NOPE:]
    w = jnp.concatenate([w, pe[..., half:], pe[..., :half]], axis=-1)
    return w.reshape(L, K, MLA_HEADS * QK_PAD)


def kernel(x, mem, positions, norm_mix_g, w_in, b_gate, conv_w, conv_b, conv_ln_g, conv_ln_b, w_conv_out, q_norm_g, w_uq, kv_norm_g, w_ukv, w_mla_out, w_out, norm_mem_g, mem_norm_g, w_xq, w_xkv, w_xo, norm_ffn_g, w_ff1, w_ff2, final_norm_g):
    B, S, D = x.shape
    assert (B, S, D) == (1, SEQ, D_MODEL)
    x2d = x.reshape(S, D)
    mem2d = mem.reshape(N_MEM, D)
    vec3 = lambda v: v.reshape(v.shape[0], 1, v.shape[1])

    inv_freq = 1.0 / (ROPE_THETA ** (jnp.arange(0, QK_ROPE, 2, dtype=F32) / QK_ROPE))
    cos, sin = rope_tables(positions.reshape(S, 1), jnp.tile(inv_freq, 4).reshape(1, LANES))

    w_in_lo = jnp.swapaxes(w_in, 1, 2).astype(BF16)
    w_q = _permute_q_weight(w_uq).astype(BF16)
    w_kv, w_co, w_mo, w_o = (w.astype(BF16) for w in (w_ukv, w_conv_out, w_mla_out, w_out))
    w_q_x, w_kv_x, w_o_x = (w.astype(BF16) for w in (w_xq, w_xkv, w_xo))
    w_1, w_2 = w_ff1.astype(BF16), w_ff2.astype(BF16)

    g_mix, g_mem, g_memn, g_ffn = vec3(norm_mix_g), vec3(norm_mem_g), vec3(mem_norm_g), vec3(norm_ffn_g)
    g_q, g_kv, b_g = vec3(q_norm_g), vec3(kv_norm_g), vec3(b_gate)
    c_b, ln_g, ln_b = vec3(conv_b), vec3(conv_ln_g), vec3(conv_ln_b)
    g_final = final_norm_g.reshape(1, 1, D)

    h = rms_norm_rows(x2d, g_mix, 0, BF16)
    xcur = x2d
    for l in range(DEPTH):
        (hglu,) = matmul("glu_proj", [h], [(0, w_in_lo, l, COL_GLU_A, True), (0, w_in_lo, l, COL_GLU_G, True)],
                         [], [F32], _epi_glu, D_CONV, 1024, 1024)

        hc = conv_ln_silu(hglu, conv_w, c_b, ln_g, ln_b, l)
        (gy,) = matmul("conv_out_gated", [hc, h], [(0, w_co, l, 0, False), (1, w_in_lo, l, COL_GATE, True)],
                       [("row", b_g, l, 0)], [BF16], _epi_gated, D, 1024, 512)

        q, k, v = mla_proj(h, w_in_lo, w_q, w_kv, l, g_q, g_kv, cos, sin)
        o = mla_attention(q, k, v)
        (merged,) = matmul("mla_out_gated_merge", [o, h],
                           [(0, w_mo, l, 0, False), (1, w_in_lo, l, COL_GATE + D, True)],
                           [("row", b_g, l, D), ("tile", gy, 0)], [BF16], _epi_gated_merge, D, 1024, 512)

        mk, mv = mem_kv(mem2d, g_memn, w_kv_x, l)
        xcur, hf = mix_cross(merged, xcur, w_o, g_mem, w_q_x, mk, mv, w_o_x, g_ffn, l)

        last = l == DEPTH - 1
        xcur, h = mlp(hf, w_1, w_2, l, xcur, g_final if last else g_mix, 0 if last else l + 1,
                      F32 if last else BF16)
    return h.reshape(B, S, D)
```

```python
import functools
import math

import jax
import jax.numpy as jnp
from jax import lax
from jax.experimental import pallas as pl
from jax.experimental.pallas import tpu as pltpu

D_MODEL = 2048
SEQ = 8192
DEPTH = 2
CHUNK = 64
N_MEM = 256
EPS = 1e-6
D_CONV = D_MODEL
CONV_K = 31
MLA_HEADS = 16
Q_LORA = 512
KV_LORA = 512
QK_NOPE = 128
QK_ROPE = 64
V_DIM = 128
ROPE_THETA = 10000.0
X_HEADS = 4
X_HEAD_DIM = 128
D_FF = 4 * D_MODEL

LANES = 128
SUBLANES = 8
QK_PAD = 256
CONV_HALO = 32
NEG_BIG = -1e30
VMEM_LIMIT = 56 * 1024 * 1024

COL_GLU_A = 0
COL_GLU_G = D_CONV
COL_LATENT = 2 * D_CONV
COL_KPE = COL_LATENT + Q_LORA + KV_LORA
COL_GATE = COL_KPE + QK_ROPE

F32 = jnp.float32
BF16 = jnp.bfloat16


def _cparams(*sem):
    return pltpu.CompilerParams(dimension_semantics=sem, vmem_limit_bytes=VMEM_LIMIT)


def _rms(x, g):
    return x * lax.rsqrt(jnp.mean(x * x, axis=-1, keepdims=True) + EPS) * g


def _layer_row(n, layer, col0=0):
    return pl.BlockSpec((None, 1, n), lambda *_: (layer, 0, col0 // n))


def _rope_table_kernel(pos_ref, freq_ref, cos_ref, sin_ref):
    ang = pos_ref[...].astype(F32) * freq_ref[...]
    lane = lax.broadcasted_iota(jnp.int32, ang.shape, 1)
    s = jnp.sin(ang)
    cos_ref[...] = jnp.where(lane < QK_ROPE, jnp.cos(ang), 0.0)
    sin_ref[...] = jnp.where(lane < QK_ROPE // 2, -s, jnp.where(lane < QK_ROPE, s, 0.0))


def rope_tables(pos_col, freq_row):
    S = pos_col.shape[0]
    tm = 1024
    tab = pl.BlockSpec((tm, LANES), lambda i: (i, 0))
    return pl.pallas_call(
        _rope_table_kernel,
        grid=(S // tm,),
        in_specs=[pl.BlockSpec((tm, 1), lambda i: (i, 0)),
                  pl.BlockSpec((1, LANES), lambda i: (0, 0))],
        out_specs=[tab, tab],
        out_shape=[jax.ShapeDtypeStruct((S, LANES), F32)] * 2,
        name="rope_table",
        compiler_params=_cparams("arbitrary"),
    )(pos_col, freq_row)


def _norm_kernel(x_ref, g_ref, o_ref):
    o_ref[...] = _rms(x_ref[...], g_ref[...]).astype(o_ref.dtype)


def rms_norm_rows(x, g3, layer, out_dtype, tm=512):
    M, D = x.shape
    return pl.pallas_call(
        _norm_kernel,
        grid=(M // tm,),
        in_specs=[pl.BlockSpec((tm, D), lambda i: (i, 0)), _layer_row(D, layer)],
        out_specs=pl.BlockSpec((tm, D), lambda i: (i, 0)),
        out_shape=jax.ShapeDtypeStruct((M, D), out_dtype),
        name="rms_norm",
        compiler_params=_cparams("arbitrary"),
    )(x, g3)


def _dot_nt(a, w_t):
    return lax.dot_general(a, w_t, (((1,), (1,)), ((), ())), preferred_element_type=F32)


def _mm_kernel(*refs, a_index, w_transposed, n_a, n_extra, epilogue):
    n_b = len(a_index)
    acts = [r[...] for r in refs[:n_a]]
    accs = [_dot_nt(acts[ai], b[0]) if t else jnp.dot(acts[ai], b[...], preferred_element_type=F32)
            for ai, t, b in zip(a_index, w_transposed, refs[n_a:n_a + n_b])]
    extras = [e[...] for e in refs[n_a + n_b:n_a + n_b + n_extra]]
    outs = epilogue(accs, extras)
    for o_ref, o in zip(refs[n_a + n_b + n_extra:], outs):
        o_ref[...] = o.astype(o_ref.dtype)


def matmul(name, acts, weights, extras, out_dtypes, epilogue, n_out, tm, tn):
    M = acts[0].shape[0]
    in_specs = [pl.BlockSpec((tm, a.shape[1]), lambda i, j: (i, 0)) for a in acts]
    operands = list(acts)
    for _, w, layer, col0, transposed in weights:
        if transposed:
            in_specs.append(pl.BlockSpec((pl.Element(1), pl.Element(tn), pl.Element(w.shape[2])),
                                         lambda i, j, layer=layer, col0=col0:
                                         (layer, pl.multiple_of(col0 + j * tn, LANES // 2), 0)))
        else:
            in_specs.append(pl.BlockSpec((None, w.shape[1], tn),
                                         lambda i, j, layer=layer, off=col0 // tn: (layer, 0, j + off)))
        operands.append(w)
    for e in extras:
        if e[0] == "row":
            _, arr, layer, col0 = e
            in_specs.append(pl.BlockSpec((None, 1, tn),
                                         lambda i, j, layer=layer, off=col0 // tn: (layer, 0, j + off)))
        else:
            _, arr, col0 = e
            in_specs.append(pl.BlockSpec((tm, tn), lambda i, j, off=col0 // tn: (i, j + off)))
        operands.append(arr)
    return pl.pallas_call(
        functools.partial(_mm_kernel, a_index=tuple(w[0] for w in weights),
                          w_transposed=tuple(w[4] for w in weights), n_a=len(acts),
                          n_extra=len(extras), epilogue=epilogue),
        grid=(M // tm, n_out // tn),
        in_specs=in_specs,
        out_specs=[pl.BlockSpec((tm, tn), lambda i, j: (i, j)) for _ in out_dtypes],
        out_shape=[jax.ShapeDtypeStruct((M, n_out), dt) for dt in out_dtypes],
        name=name,
        compiler_params=_cparams("arbitrary", "arbitrary"),
    )(*operands)


def _epi_glu(accs, extras):
    return [accs[0] * jax.nn.sigmoid(accs[1])]


def _epi_gated(accs, extras):
    return [accs[0] * jax.nn.sigmoid(accs[1] + extras[0])]


def _epi_gated_merge(accs, extras):
    return [extras[1] + accs[0] * jax.nn.sigmoid(accs[1] + extras[0])]


def _mla_proj_kernel(h_ref, wlat_ref, wpe_ref, wq_ref, wkv_ref, qg_ref, kvg_ref, cos_ref, sin_ref,
                     q_ref, k_ref, v_ref):
    h = h_ref[...]
    cos, sin = cos_ref[...], sin_ref[...]
    z = _dot_nt(h, wlat_ref[...])
    cq = _rms(z[:, :Q_LORA], qg_ref[...]).astype(BF16)
    ckv = _rms(z[:, Q_LORA:], kvg_ref[...]).astype(BF16)
    pe = _dot_nt(h, wpe_ref[...])
    half = QK_ROPE // 2
    lane = lax.broadcasted_iota(jnp.int32, pe.shape, 1)
    swapped = jnp.where(lane < half, pltpu.roll(pe, LANES - half, 1), pltpu.roll(pe, half, 1))
    kpe = jnp.where(lane < QK_ROPE, pe * cos + swapped * sin, 0.0).astype(k_ref.dtype)

    scale = (QK_NOPE + QK_ROPE) ** -0.5 * math.log2(math.e)
    kv_width = QK_NOPE + V_DIM
    for hd in range(MLA_HEADS):
        blk = jnp.dot(cq, wq_ref[:, hd * QK_PAD:(hd + 1) * QK_PAD], preferred_element_type=F32)
        q_ref[hd, :, 0:QK_NOPE] = (blk[:, :QK_NOPE] * scale).astype(q_ref.dtype)
        qpe = blk[:, QK_NOPE:]
        roped = qpe * cos + pltpu.roll(qpe, QK_ROPE, 1) * sin
        q_ref[hd, :, QK_NOPE:QK_PAD] = (roped * scale).astype(q_ref.dtype)

        blk = jnp.dot(ckv, wkv_ref[:, hd * kv_width:(hd + 1) * kv_width], preferred_element_type=F32)
        k_ref[hd, :, 0:QK_NOPE] = blk[:, :QK_NOPE].astype(k_ref.dtype)
        k_ref[hd, :, QK_NOPE:QK_PAD] = kpe
        v_ref[:, hd * V_DIM:(hd + 1) * V_DIM] = blk[:, QK_NOPE:].astype(v_ref.dtype)


def mla_proj(h, w_in_lo, w_q, w_kv, layer, qg3, kvg3, cos, sin, tm=512):
    M, K = h.shape
    n_lat = Q_LORA + KV_LORA
    rows = lambda n: pl.BlockSpec((tm, n), lambda i: (i, 0))
    once = pl.Buffered(1)
    heads = pl.BlockSpec((MLA_HEADS, tm, QK_PAD), lambda i: (0, i, 0))
    return pl.pallas_call(
        _mla_proj_kernel,
        grid=(M // tm,),
        in_specs=[rows(K),
                  pl.BlockSpec((None, n_lat, K), lambda i: (layer, COL_LATENT // n_lat, 0), pipeline_mode=once),
                  pl.BlockSpec((None, LANES, K), lambda i: (layer, COL_KPE // LANES, 0), pipeline_mode=once),
                  pl.BlockSpec((None, Q_LORA, MLA_HEADS * QK_PAD), lambda i: (layer, 0, 0), pipeline_mode=once),
                  pl.BlockSpec((None, KV_LORA, MLA_HEADS * (QK_NOPE + V_DIM)), lambda i: (layer, 0, 0),
                               pipeline_mode=once),
                  _layer_row(Q_LORA, layer), _layer_row(KV_LORA, layer),
                  rows(LANES), rows(LANES)],
        out_specs=[heads, heads, rows(MLA_HEADS * V_DIM)],
        out_shape=[jax.ShapeDtypeStruct((MLA_HEADS, M, QK_PAD), BF16),
                   jax.ShapeDtypeStruct((MLA_HEADS, M, QK_PAD), BF16),
                   jax.ShapeDtypeStruct((M, MLA_HEADS * V_DIM), BF16)],
        name="mla_proj",
        compiler_params=_cparams("arbitrary"),
    )(h, w_in_lo, w_in_lo, w_q, w_kv, qg3, kvg3, cos, sin)


ATTN_RB = CHUNK
ATTN_UNROLLS = (12, 8, 4, 2)


def _attn_kernel(q_ref, k_ref, v_ref, o_ref, vone_ref, *set_refs, tq):
    vone_ref[:, 0:V_DIM] = v_ref[...]
    vone_ref[:, V_DIM:] = jnp.ones((vone_ref.shape[0], V_DIM), vone_ref.dtype)
    n_set = len(set_refs) // 2
    sets = [_AttnBlock(q_ref, k_ref, vone_ref, o_ref, set_refs[i * n_set:(i + 1) * n_set], tq, i)
            for i in range(2)]
    n_q = q_ref.shape[1] // tq
    assert n_q % 2 == 0

    sets[0].first_block()
    sets[1].fill(jnp.int32(1))

    def query_block_pair(qq, carry):
        qa, qb = 2 * qq, 2 * qq + 1
        sets[1].drain(qa - 1)
        sets[0].fill(qa)
        sets[0].steady(qa)
        sets[0].drain(qa)
        sets[1].fill(qb)
        sets[1].steady(qb)
        return carry

    lax.fori_loop(1, n_q // 2, query_block_pair, 0)
    sets[1].drain(jnp.int32(n_q - 1))


class _AttnBlock:
    def __init__(self, q_ref, k_ref, v_ref, o_ref, refs, tq, par):
        s0, s1, p0, p1, a0, a1, self.m_ref, self.acc_ref = refs
        self.s_refs, self.p_refs, self.a_refs = (s0, s1), (p0, p1), (a0, a1)
        self.q_ref, self.k_ref, self.v_ref, self.o_ref = q_ref, k_ref, v_ref, o_ref
        self.tq, self.par = tq, par

    def _q_rows(self, qi):
        return pl.ds(pl.multiple_of(qi * self.tq, self.tq), self.tq)

    def _init(self):
        self.m_ref[...] = jnp.full(self.m_ref.shape, NEG_BIG, F32)
        self.acc_ref[...] = jnp.zeros(self.acc_ref.shape, F32)

    def scores(self, qi, j, par):
        start = pl.multiple_of(j * self.tq, self.tq)
        kb = self.k_ref[0, pl.ds(start, self.tq), :]
        self.s_refs[par][...] = lax.dot_general(self.q_ref[0, self._q_rows(qi), :], kb,
                                                (((1,), (1,)), ((), ())), preferred_element_type=F32)

    def softmax(self, par, masked):
        s_ref, p_ref, a_ref, m_ref = self.s_refs[par], self.p_refs[par], self.a_refs[par], self.m_ref
        for r in range(self.tq // ATTN_RB):
            rows = slice(r * ATTN_RB, (r + 1) * ATTN_RB)
            cols = []
            for c in range(self.tq // LANES):
                first_chunk = (c * LANES) // CHUNK
                if masked and first_chunk > r:
                    cols.append(None)
                    continue
                sc = s_ref[rows, c * LANES:(c + 1) * LANES]
                if masked and first_chunk == r:
                    lane = lax.broadcasted_iota(jnp.int32, sc.shape, 1)
                    sc = jnp.where(lane < CHUNK, sc, NEG_BIG)
                cols.append(sc)
            mx = functools.reduce(jnp.maximum, [sc for sc in cols if sc is not None])
            m_prev = m_ref[rows, :]
            m_new = jnp.maximum(m_prev, jnp.max(mx, axis=-1, keepdims=True))
            m_ref[rows, :] = m_new
            a_ref[rows, :] = jnp.exp2(m_prev - m_new)
            for c, sc in enumerate(cols):
                pc = jnp.zeros((ATTN_RB, LANES), BF16) if sc is None else jnp.exp2(sc - m_new).astype(BF16)
                p_ref[rows, c * LANES:(c + 1) * LANES] = pc

    def values(self, j, par):
        start = pl.multiple_of(j * self.tq, self.tq)
        vb = self.v_ref[pl.ds(start, self.tq), :]
        alpha = self.a_refs[par][...]
        self.acc_ref[...] = (jnp.concatenate([alpha, alpha], axis=1) * self.acc_ref[...]
                             + jnp.dot(self.p_refs[par][...], vb, preferred_element_type=F32))

    def _write(self, qi):
        acc = self.acc_ref
        self.o_ref[self._q_rows(qi), :] = (acc[:, 0:V_DIM] / acc[:, V_DIM:]).astype(self.o_ref.dtype)

    def first_block(self):
        qi = jnp.int32(0)
        self._init()
        self.scores(qi, qi, 0)
        self.softmax(0, True)
        self.values(qi, 0)
        self._write(qi)

    def fill(self, qi):
        self._init()
        self.scores(qi, 0, 0)
        self.scores(qi, 1, 1)
        self.softmax(0, False)
        if self.par == 0:
            self._tick(qi, 2, 0)

    def _tick(self, qi, t, par):
        self.scores(qi, t, par)
        self.softmax(1 - par, False)
        self.values(t - 2, par)

    def steady(self, qi):
        first = 3 if self.par == 0 else 2
        n_left = qi + 1 - first
        t0 = first
        for unroll in ATTN_UNROLLS:

            def unrolled(i, carry, t0=t0, unroll=unroll):
                for u in range(unroll):
                    self._tick(qi, t0 + unroll * i + u, (first + u) % 2)
                return carry

            lax.fori_loop(0, n_left // unroll, unrolled, 0)
            t0 = t0 + (n_left // unroll) * unroll
            n_left = n_left % unroll

    def drain(self, qi):
        self.softmax(self.par, True)
        self.values(qi - 1, 1 - self.par)
        self.values(qi, self.par)
        self._write(qi)


def mla_attention(q, k, v, tq=512):
    H, S, _ = q.shape
    buffer_set = ([pltpu.VMEM((tq, tq), F32)] * 2
                  + [pltpu.VMEM((tq, tq), BF16)] * 2
                  + [pltpu.VMEM((tq, LANES), F32)] * 2
                  + [pltpu.VMEM((tq, LANES), F32),
                     pltpu.VMEM((tq, 2 * V_DIM), F32)])
    return pl.pallas_call(
        functools.partial(_attn_kernel, tq=tq),
        grid=(H,),
        in_specs=[pl.BlockSpec((1, S, QK_PAD), lambda h: (h, 0, 0)),
                  pl.BlockSpec((1, S, QK_PAD), lambda h: (h, 0, 0)),
                  pl.BlockSpec((S, V_DIM), lambda h: (0, h))],
        out_specs=pl.BlockSpec((S, V_DIM), lambda h: (0, h)),
        out_shape=jax.ShapeDtypeStruct((S, H * V_DIM), BF16),
        scratch_shapes=[pltpu.VMEM((S, 2 * V_DIM), BF16)] + buffer_set + buffer_set,
        name="mla_attention",
        compiler_params=_cparams("arbitrary"),
    )(q, k, v)


CONV_RB = 64
CONV_CB = LANES


def _conv_kernel(cur_ref, halo_ref, w_ref, b_ref, g_ref, beta_ref, o_ref, buf_ref, sh_ref, acc_ref, wb_ref, *, tm):
    i = pl.program_id(0)

    @pl.when(i == 0)
    def _():
        for k in range(CONV_K):
            wb_ref[k] = jnp.broadcast_to(w_ref[k:k + 1, :], (SUBLANES, D_CONV))

    buf_ref[0:CONV_HALO, :] = jnp.where(i > 0, halo_ref[...], 0.0)
    buf_ref[CONV_HALO:CONV_HALO + tm, :] = cur_ref[...]
    n_buf = tm + CONV_HALO
    n_sh = n_buf - SUBLANES
    for c0 in range(0, D_CONV, LANES):
        x = buf_ref[:, c0:c0 + LANES]
        for b in range(1, SUBLANES):
            sh_ref[b - 1, :, c0:c0 + LANES] = pltpu.roll(x, n_buf - b, 0)[0:n_sh, :]
    off = CONV_HALO - (CONV_K - 1)
    for c0 in range(0, D_CONV, CONV_CB):
        cols = slice(c0, c0 + CONV_CB)
        w_all = wb_ref[:, :, cols]
        bias = jnp.broadcast_to(b_ref[:, cols], (CONV_RB, CONV_CB))

        def row_block(rb, carry, cols=cols, w_all=w_all, bias=bias):
            r0 = pl.multiple_of(rb * CONV_RB, CONV_RB)
            acc = bias
            for b in range(SUBLANES):
                taps = [k for k in range(CONV_K) if (off + k) % SUBLANES == b]
                a_lo, a_hi = (off + taps[0]) // SUBLANES, (off + taps[-1]) // SUBLANES
                rows = pl.ds(r0 + a_lo * SUBLANES, (a_hi - a_lo) * SUBLANES + CONV_RB)
                slab = buf_ref[rows, cols] if b == 0 else sh_ref[b - 1, rows, cols]
                for k in taps:
                    s0 = ((off + k) // SUBLANES - a_lo) * SUBLANES
                    wk = jnp.concatenate([w_all[k]] * (CONV_RB // SUBLANES), axis=0)
                    acc = acc + slab[s0:s0 + CONV_RB, :] * wk
            acc_ref[pl.ds(r0, CONV_RB), cols] = acc
            return carry

        lax.fori_loop(0, tm // CONV_RB, row_block, 0)
    y = acc_ref[...]
    mu = jnp.mean(y, axis=-1, keepdims=True)
    yc = y - mu
    yn = yc * lax.rsqrt(jnp.mean(yc * yc, axis=-1, keepdims=True) + EPS) * g_ref[...] + beta_ref[...]
    o_ref[...] = (yn * jax.nn.sigmoid(yn)).astype(o_ref.dtype)


def conv_ln_silu(hglu, conv_w, conv_b3, ln_g3, ln_b3, layer, tm=256):
    S, D = hglu.shape
    row = _layer_row(D, layer)
    return pl.pallas_call(
        functools.partial(_conv_kernel, tm=tm),
        grid=(S // tm,),
        in_specs=[pl.BlockSpec((tm, D), lambda i: (i, 0)),
                  pl.BlockSpec((CONV_HALO, D), lambda i: (jnp.maximum(i * (tm // CONV_HALO) - 1, 0), 0)),
                  pl.BlockSpec((None, CONV_K, D), lambda i: (layer, 0, 0)),
                  row, row, row],
        out_specs=pl.BlockSpec((tm, D), lambda i: (i, 0)),
        out_shape=jax.ShapeDtypeStruct((S, D), BF16),
        scratch_shapes=[pltpu.VMEM((CONV_HALO + tm, D), F32),
                        pltpu.VMEM((SUBLANES - 1, CONV_HALO + tm - SUBLANES, D), F32),
                        pltpu.VMEM((tm, D), F32),
                        pltpu.VMEM((CONV_K, SUBLANES, D), F32)],
        name="conv_ln_silu",
        compiler_params=_cparams("arbitrary"),
    )(hglu, hglu, conv_w, conv_b3, ln_g3, ln_b3)


def _memkv_kernel(mem_ref, g_ref, w_ref, k_ref, v_ref):
    mn = _rms(mem_ref[...], g_ref[...]).astype(BF16)
    kv = jnp.dot(mn, w_ref[...], preferred_element_type=F32)
    n = X_HEADS * X_HEAD_DIM
    k_ref[...] = kv[:, :n].astype(k_ref.dtype)
    v_ref[...] = kv[:, n:].astype(v_ref.dtype)


def mem_kv(mem, g3, w, layer):
    n = X_HEADS * X_HEAD_DIM
    D = mem.shape[1]
    full = lambda shape: pl.BlockSpec(shape, lambda i: (0, 0))
    return pl.pallas_call(
        _memkv_kernel,
        grid=(1,),
        in_specs=[full(mem.shape), _layer_row(D, layer),
                  pl.BlockSpec((None, D, 2 * n), lambda i: (layer, 0, 0))],
        out_specs=[full((N_MEM, n)), full((N_MEM, n))],
        out_shape=[jax.ShapeDtypeStruct((N_MEM, n), BF16)] * 2,
        name="mem_kv",
        compiler_params=_cparams("arbitrary"),
    )(mem, g3, w)


def _mix_cross_kernel(m_ref, x_ref, wo_ref, gmem_ref, wq_ref, k_ref, v_ref, wxo_ref, gffn_ref, xo_ref, ho_ref):
    x1 = x_ref[...] + jnp.dot(m_ref[...], wo_ref[...], preferred_element_type=F32)
    hm = _rms(x1, gmem_ref[...]).astype(BF16)
    q = jnp.dot(hm, wq_ref[...], preferred_element_type=F32) * (X_HEAD_DIM ** -0.5)
    q = q.astype(BF16)
    outs = []
    for h in range(X_HEADS):
        sl = slice(h * X_HEAD_DIM, (h + 1) * X_HEAD_DIM)
        s = lax.dot_general(q[:, sl], k_ref[:, sl], (((1,), (1,)), ((), ())), preferred_element_type=F32)
        p = jnp.exp(s - jnp.max(s, axis=-1, keepdims=True))
        l = jnp.sum(p, axis=-1, keepdims=True)
        o = jnp.dot(p.astype(BF16), v_ref[:, sl], preferred_element_type=F32) / l
        outs.append(o.astype(BF16))
    o = jnp.concatenate(outs, axis=-1)
    x2 = x1 + jnp.dot(o, wxo_ref[...], preferred_element_type=F32)
    xo_ref[...] = x2
    ho_ref[...] = _rms(x2, gffn_ref[...]).astype(ho_ref.dtype)


def mix_cross(merged, x, w_out, g_mem3, w_xq, mk, mv, w_xo, g_ffn3, layer, tm=512):
    M, D = merged.shape
    n = X_HEADS * X_HEAD_DIM
    const = lambda shape: pl.BlockSpec(shape, lambda i: (0, 0))
    rows = pl.BlockSpec((tm, D), lambda i: (i, 0))
    once = pl.Buffered(1)
    weight = lambda k, c: pl.BlockSpec((None, k, c), lambda i: (layer, 0, 0), pipeline_mode=once)
    return pl.pallas_call(
        _mix_cross_kernel,
        grid=(M // tm,),
        in_specs=[rows, rows, weight(D, D), _layer_row(D, layer), weight(D, n),
                  const((N_MEM, n)), const((N_MEM, n)), weight(n, D), _layer_row(D, layer)],
        out_specs=[rows, rows],
        out_shape=[jax.ShapeDtypeStruct((M, D), F32), jax.ShapeDtypeStruct((M, D), BF16)],
        name="mix_cross",
        compiler_params=_cparams("arbitrary"),
    )(merged, x, w_out, g_mem3, w_xq, mk, mv, w_xo, g_ffn3)


def _mlp_kernel(h_ref, w1_ref, w2_ref, x_ref, g_ref, xo_ref, ho_ref):
    f = pl.program_id(1)

    @pl.when(f == 0)
    def _():
        xo_ref[...] = x_ref[...]

    a = jnp.dot(h_ref[...], w1_ref[...], preferred_element_type=F32)
    a = jnp.square(jnp.maximum(a, 0.0)).astype(BF16)
    xo_ref[...] += jnp.dot(a, w2_ref[...], preferred_element_type=F32)

    @pl.when(f == pl.num_programs(1) - 1)
    def _():
        ho_ref[...] = _rms(xo_ref[...], g_ref[...]).astype(ho_ref.dtype)


def mlp(h, w1, w2, layer, x, g3, g_layer, out_dtype, tm=512, tf=1024):
    M, D = h.shape
    F = w1.shape[2]
    rows = pl.BlockSpec((tm, D), lambda i, f: (i, 0))
    return pl.pallas_call(
        _mlp_kernel,
        grid=(M // tm, F // tf),
        in_specs=[rows,
                  pl.BlockSpec((None, D, tf), lambda i, f: (layer, 0, f)),
                  pl.BlockSpec((None, tf, D), lambda i, f: (layer, f, 0)),
                  rows,
                  _layer_row(D, g_layer)],
        out_specs=[rows, rows],
        out_shape=[jax.ShapeDtypeStruct((M, D), F32), jax.ShapeDtypeStruct((M, D), out_dtype)],
        name="mlp",
        compiler_params=_cparams("arbitrary", "arbitrary"),
    )(h, w1, w2, x, g3)


def _permute_q_weight(w_uq):
    L, K, _ = w_uq.shape
    half = QK_ROPE // 2
    w = w_uq.reshape(L, K, MLA_HEADS, QK_NOPE + QK_ROPE)
    pe = w[..., QK_NOPE:]
    w = jnp.concatenate([w, pe[..., half:], pe[..., :half]], axis=-1)
    return w.reshape(L, K, MLA_HEADS * QK_PAD)


def kernel(x, mem, positions, norm_mix_g, w_in, b_gate, conv_w, conv_b, conv_ln_g, conv_ln_b, w_conv_out, q_norm_g, w_uq, kv_norm_g, w_ukv, w_mla_out, w_out, norm_mem_g, mem_norm_g, w_xq, w_xkv, w_xo, norm_ffn_g, w_ff1, w_ff2, final_norm_g):
    B, S, D = x.shape
    assert (B, S, D) == (1, SEQ, D_MODEL)
    x2d = x.reshape(S, D)
    mem2d = mem.reshape(N_MEM, D)
    vec3 = lambda v: v.reshape(v.shape[0], 1, v.shape[1])

    inv_freq = 1.0 / (ROPE_THETA ** (jnp.arange(0, QK_ROPE, 2, dtype=F32) / QK_ROPE))
    cos, sin = rope_tables(positions.reshape(S, 1), jnp.tile(inv_freq, 4).reshape(1, LANES))

    w_in_lo = jnp.swapaxes(w_in, 1, 2).astype(BF16)
    w_q = _permute_q_weight(w_uq.astype(BF16))
    w_kv, w_co, w_mo, w_o = (w.astype(BF16) for w in (w_ukv, w_conv_out, w_mla_out, w_out))
    w_q_x, w_kv_x, w_o_x = (w.astype(BF16) for w in (w_xq, w_xkv, w_xo))
    w_1, w_2 = w_ff1.astype(BF16), w_ff2.astype(BF16)

    g_mix, g_mem, g_memn, g_ffn = vec3(norm_mix_g), vec3(norm_mem_g), vec3(mem_norm_g), vec3(norm_ffn_g)
    g_q, g_kv, b_g = vec3(q_norm_g), vec3(kv_norm_g), vec3(b_gate)
    c_b, ln_g, ln_b = vec3(conv_b), vec3(conv_ln_g), vec3(conv_ln_b)
    g_final = final_norm_g.reshape(1, 1, D)

    h = rms_norm_rows(x2d, g_mix, 0, BF16)
    xcur = x2d
    for l in range(DEPTH):
        (hglu,) = matmul("glu_proj", [h], [(0, w_in_lo, l, COL_GLU_A, True), (0, w_in_lo, l, COL_GLU_G, True)],
                         [], [F32], _epi_glu, D_CONV, 1024, 1024)

        hc = conv_ln_silu(hglu, conv_w, c_b, ln_g, ln_b, l)
        (gy,) = matmul("conv_out_gated", [hc, h], [(0, w_co, l, 0, False), (1, w_in_lo, l, COL_GATE, True)],
                       [("row", b_g, l, 0)], [BF16], _epi_gated, D, 1024, 512)

        q, k, v = mla_proj(h, w_in_lo, w_q, w_kv, l, g_q, g_kv, cos, sin)
        o = mla_attention(q, k, v)
        (merged,) = matmul("mla_out_gated_merge", [o, h],
                           [(0, w_mo, l, 0, False), (1, w_in_lo, l, COL_GATE + D, True)],
                           [("row", b_g, l, D), ("tile", gy, 0)], [BF16], _epi_gated_merge, D, 1024, 512)

        mk, mv = mem_kv(mem2d, g_memn, w_kv_x, l)
        xcur, hf = mix_cross(merged, xcur, w_o, g_mem, w_q_x, mk, mv, w_o_x, g_ffn, l)

        last = l == DEPTH - 1
        xcur, h = mlp(hf, w_1, w_2, l, xcur, g_final if last else g_mix, 0 if last else l + 1,
                      F32 if last else BF16)
    return h.reshape(B, S, D)
```

```python
import functools
import math

import jax
import jax.numpy as jnp
from jax import lax
from jax.experimental import pallas as pl
from jax.experimental.pallas import tpu as pltpu

D_MODEL = 2048
SEQ = 8192
DEPTH = 2
CHUNK = 64
N_MEM = 256
EPS = 1e-6
D_CONV = D_MODEL
CONV_K = 31
MLA_HEADS = 16
Q_LORA = 512
KV_LORA = 512
QK_NOPE = 128
QK_ROPE = 64
V_DIM = 128
ROPE_THETA = 10000.0
X_HEADS = 4
X_HEAD_DIM = 128
D_FF = 4 * D_MODEL

LANES = 128
SUBLANES = 8
QK_PAD = 256
CONV_HALO = 32
NEG_BIG = -1e30
VMEM_LIMIT = 56 * 1024 * 1024

COL_GLU_A = 0
COL_GLU_G = D_CONV
COL_LATENT = 2 * D_CONV
COL_KPE = COL_LATENT + Q_LORA + KV_LORA
COL_GATE = COL_KPE + QK_ROPE

F32 = jnp.float32
BF16 = jnp.bfloat16


def _cparams(*sem):
    return pltpu.CompilerParams(dimension_semantics=sem, vmem_limit_bytes=VMEM_LIMIT)


def _rms(x, g):
    return x * lax.rsqrt(jnp.mean(x * x, axis=-1, keepdims=True) + EPS) * g


def _layer_row(n, layer, col0=0):
    return pl.BlockSpec((None, 1, n), lambda *_: (layer, 0, col0 // n))


def _rope_table_kernel(pos_ref, freq_ref, cos_ref, sin_ref):
    ang = pos_ref[...].astype(F32) * freq_ref[...]
    lane = lax.broadcasted_iota(jnp.int32, ang.shape, 1)
    s = jnp.sin(ang)
    cos_ref[...] = jnp.where(lane < QK_ROPE, jnp.cos(ang), 0.0)
    sin_ref[...] = jnp.where(lane < QK_ROPE // 2, -s, jnp.where(lane < QK_ROPE, s, 0.0))


def rope_tables(pos_col, freq_row):
    S = pos_col.shape[0]
    tm = 1024
    tab = pl.BlockSpec((tm, LANES), lambda i: (i, 0))
    return pl.pallas_call(
        _rope_table_kernel,
        grid=(S // tm,),
        in_specs=[pl.BlockSpec((tm, 1), lambda i: (i, 0)),
                  pl.BlockSpec((1, LANES), lambda i: (0, 0))],
        out_specs=[tab, tab],
        out_shape=[jax.ShapeDtypeStruct((S, LANES), F32)] * 2,
        name="rope_table",
        compiler_params=_cparams("arbitrary"),
    )(pos_col, freq_row)


def _norm_kernel(x_ref, g_ref, o_ref):
    o_ref[...] = _rms(x_ref[...], g_ref[...]).astype(o_ref.dtype)


def rms_norm_rows(x, g3, layer, out_dtype, tm=512):
    M, D = x.shape
    return pl.pallas_call(
        _norm_kernel,
        grid=(M // tm,),
        in_specs=[pl.BlockSpec((tm, D), lambda i: (i, 0)), _layer_row(D, layer)],
        out_specs=pl.BlockSpec((tm, D), lambda i: (i, 0)),
        out_shape=jax.ShapeDtypeStruct((M, D), out_dtype),
        name="rms_norm",
        compiler_params=_cparams("arbitrary"),
    )(x, g3)


def _dot_nt(a, w_t):
    return lax.dot_general(a, w_t, (((1,), (1,)), ((), ())), preferred_element_type=F32)


def _mm_kernel(*refs, a_index, w_transposed, n_a, n_extra, epilogue):
    n_b = len(a_index)
    acts = [r[...] for r in refs[:n_a]]
    accs = [_dot_nt(acts[ai], b[0]) if t else jnp.dot(acts[ai], b[...], preferred_element_type=F32)
            for ai, t, b in zip(a_index, w_transposed, refs[n_a:n_a + n_b])]
    extras = [e[...] for e in refs[n_a + n_b:n_a + n_b + n_extra]]
    outs = epilogue(accs, extras)
    for o_ref, o in zip(refs[n_a + n_b + n_extra:], outs):
        o_ref[...] = o.astype(o_ref.dtype)


def matmul(name, acts, weights, extras, out_dtypes, epilogue, n_out, tm, tn):
    M = acts[0].shape[0]
    in_specs = [pl.BlockSpec((tm, a.shape[1]), lambda i, j: (i, 0)) for a in acts]
    operands = list(acts)
    for _, w, layer, col0, transposed in weights:
        if transposed:
            in_specs.append(pl.BlockSpec((pl.Element(1), pl.Element(tn), pl.Element(w.shape[2])),
                                         lambda i, j, layer=layer, col0=col0:
                                         (layer, pl.multiple_of(col0 + j * tn, LANES // 2), 0)))
        else:
            in_specs.append(pl.BlockSpec((None, w.shape[1], tn),
                                         lambda i, j, layer=layer, off=col0 // tn: (layer, 0, j + off)))
        operands.append(w)
    for e in extras:
        if e[0] == "row":
            _, arr, layer, col0 = e
            in_specs.append(pl.BlockSpec((None, 1, tn),
                                         lambda i, j, layer=layer, off=col0 // tn: (layer, 0, j + off)))
        else:
            _, arr, col0 = e
            in_specs.append(pl.BlockSpec((tm, tn), lambda i, j, off=col0 // tn: (i, j + off)))
        operands.append(arr)
    return pl.pallas_call(
        functools.partial(_mm_kernel, a_index=tuple(w[0] for w in weights),
                          w_transposed=tuple(w[4] for w in weights), n_a=len(acts),
                          n_extra=len(extras), epilogue=epilogue),
        grid=(M // tm, n_out // tn),
        in_specs=in_specs,
        out_specs=[pl.BlockSpec((tm, tn), lambda i, j: (i, j)) for _ in out_dtypes],
        out_shape=[jax.ShapeDtypeStruct((M, n_out), dt) for dt in out_dtypes],
        name=name,
        compiler_params=_cparams("arbitrary", "arbitrary"),
    )(*operands)


def _epi_glu(accs, extras):
    return [accs[0] * jax.nn.sigmoid(accs[1])]


def _epi_gated(accs, extras):
    return [accs[0] * jax.nn.sigmoid(accs[1] + extras[0])]


def _epi_gated_merge(accs, extras):
    return [extras[1] + accs[0] * jax.nn.sigmoid(accs[1] + extras[0])]


def _mla_proj_kernel(h_ref, wlat_ref, wpe_ref, wq_ref, wkv_ref, qg_ref, kvg_ref, cos_ref, sin_ref,
                     q_ref, k_ref, v_ref):
    h = h_ref[...]
    cos, sin = cos_ref[...], sin_ref[...]
    z = _dot_nt(h, wlat_ref[...])
    cq = _rms(z[:, :Q_LORA], qg_ref[...]).astype(BF16)
    ckv = _rms(z[:, Q_LORA:], kvg_ref[...]).astype(BF16)
    pe = _dot_nt(h, wpe_ref[...])
    half = QK_ROPE // 2
    lane = lax.broadcasted_iota(jnp.int32, pe.shape, 1)
    swapped = jnp.where(lane < half, pltpu.roll(pe, LANES - half, 1), pltpu.roll(pe, half, 1))
    kpe = jnp.where(lane < QK_ROPE, pe * cos + swapped * sin, 0.0).astype(k_ref.dtype)

    scale = (QK_NOPE + QK_ROPE) ** -0.5 * math.log2(math.e)
    kv_width = QK_NOPE + V_DIM
    for hd in range(MLA_HEADS):
        blk = jnp.dot(cq, wq_ref[:, hd * QK_PAD:(hd + 1) * QK_PAD], preferred_element_type=F32)
        q_ref[hd, :, 0:QK_NOPE] = (blk[:, :QK_NOPE] * scale).astype(q_ref.dtype)
        qpe = blk[:, QK_NOPE:]
        roped = qpe * cos + pltpu.roll(qpe, QK_ROPE, 1) * sin
        q_ref[hd, :, QK_NOPE:QK_PAD] = (roped * scale).astype(q_ref.dtype)

        blk = jnp.dot(ckv, wkv_ref[:, hd * kv_width:(hd + 1) * kv_width], preferred_element_type=F32)
        k_ref[hd, :, 0:QK_NOPE] = blk[:, :QK_NOPE].astype(k_ref.dtype)
        k_ref[hd, :, QK_NOPE:QK_PAD] = kpe
        v_ref[:, hd * V_DIM:(hd + 1) * V_DIM] = blk[:, QK_NOPE:].astype(v_ref.dtype)


def mla_proj(h, w_in_lo, w_q, w_kv, layer, qg3, kvg3, cos, sin, tm=512):
    M, K = h.shape
    n_lat = Q_LORA + KV_LORA
    rows = lambda n: pl.BlockSpec((tm, n), lambda i: (i, 0))
    once = pl.Buffered(1)
    heads = pl.BlockSpec((MLA_HEADS, tm, QK_PAD), lambda i: (0, i, 0))
    return pl.pallas_call(
        _mla_proj_kernel,
        grid=(M // tm,),
        in_specs=[rows(K),
                  pl.BlockSpec((None, n_lat, K), lambda i: (layer, COL_LATENT // n_lat, 0), pipeline_mode=once),
                  pl.BlockSpec((None, LANES, K), lambda i: (layer, COL_KPE // LANES, 0), pipeline_mode=once),
                  pl.BlockSpec((None, Q_LORA, MLA_HEADS * QK_PAD), lambda i: (layer, 0, 0), pipeline_mode=once),
                  pl.BlockSpec((None, KV_LORA, MLA_HEADS * (QK_NOPE + V_DIM)), lambda i: (layer, 0, 0),
                               pipeline_mode=once),
                  _layer_row(Q_LORA, layer), _layer_row(KV_LORA, layer),
                  rows(LANES), rows(LANES)],
        out_specs=[heads, heads, rows(MLA_HEADS * V_DIM)],
        out_shape=[jax.ShapeDtypeStruct((MLA_HEADS, M, QK_PAD), BF16),
                   jax.ShapeDtypeStruct((MLA_HEADS, M, QK_PAD), BF16),
                   jax.ShapeDtypeStruct((M, MLA_HEADS * V_DIM), BF16)],
        name="mla_proj",
        compiler_params=_cparams("arbitrary"),
    )(h, w_in_lo, w_in_lo, w_q, w_kv, qg3, kvg3, cos, sin)


ATTN_RB = CHUNK
ATTN_UNROLLS = (12, 8, 6, 4, 2)


def _attn_kernel(q_ref, k_ref, v_ref, o_ref, vone_ref, *set_refs, tq):
    vone_ref[:, 0:V_DIM] = v_ref[...]
    vone_ref[:, V_DIM:] = jnp.ones((vone_ref.shape[0], V_DIM), vone_ref.dtype)
    n_set = len(set_refs) // 2
    sets = [_AttnBlock(q_ref, k_ref, vone_ref, o_ref, set_refs[i * n_set:(i + 1) * n_set], tq, i)
            for i in range(2)]
    n_q = q_ref.shape[1] // tq
    assert n_q % 2 == 0

    sets[0].first_block()
    sets[1].fill(jnp.int32(1))

    def query_block_pair(qq, carry):
        qa, qb = 2 * qq, 2 * qq + 1
        sets[1].drain(qa - 1)
        sets[0].fill(qa)
        sets[0].steady(qa)
        sets[0].drain(qa)
        sets[1].fill(qb)
        sets[1].steady(qb)
        return carry

    lax.fori_loop(1, n_q // 2, query_block_pair, 0)
    sets[1].drain(jnp.int32(n_q - 1))


class _AttnBlock:
    def __init__(self, q_ref, k_ref, v_ref, o_ref, refs, tq, par):
        s0, s1, p0, p1, a0, a1, self.m_ref, self.acc_ref = refs
        self.s_refs, self.p_refs, self.a_refs = (s0, s1), (p0, p1), (a0, a1)
        self.q_ref, self.k_ref, self.v_ref, self.o_ref = q_ref, k_ref, v_ref, o_ref
        self.tq, self.par = tq, par

    def _q_rows(self, qi):
        return pl.ds(pl.multiple_of(qi * self.tq, self.tq), self.tq)

    def _init(self):
        self.m_ref[...] = jnp.full(self.m_ref.shape, NEG_BIG, F32)
        self.acc_ref[...] = jnp.zeros(self.acc_ref.shape, F32)

    def scores(self, qi, j, par):
        start = pl.multiple_of(j * self.tq, self.tq)
        kb = self.k_ref[0, pl.ds(start, self.tq), :]
        self.s_refs[par][...] = lax.dot_general(self.q_ref[0, self._q_rows(qi), :], kb,
                                                (((1,), (1,)), ((), ())), preferred_element_type=F32)

    def softmax(self, par, masked):
        s_ref, p_ref, a_ref, m_ref = self.s_refs[par], self.p_refs[par], self.a_refs[par], self.m_ref
        for r in range(self.tq // ATTN_RB):
            rows = slice(r * ATTN_RB, (r + 1) * ATTN_RB)
            cols = []
            for c in range(self.tq // LANES):
                first_chunk = (c * LANES) // CHUNK
                if masked and first_chunk > r:
                    cols.append(None)
                    continue
                sc = s_ref[rows, c * LANES:(c + 1) * LANES]
                if masked and first_chunk == r:
                    lane = lax.broadcasted_iota(jnp.int32, sc.shape, 1)
                    sc = jnp.where(lane < CHUNK, sc, NEG_BIG)
                cols.append(sc)
            mx = functools.reduce(jnp.maximum, [sc for sc in cols if sc is not None])
            m_prev = m_ref[rows, :]
            m_new = jnp.maximum(m_prev, jnp.max(mx, axis=-1, keepdims=True))
            m_ref[rows, :] = m_new
            a_ref[rows, :] = jnp.exp2(m_prev - m_new)
            for c, sc in enumerate(cols):
                pc = jnp.zeros((ATTN_RB, LANES), BF16) if sc is None else jnp.exp2(sc - m_new).astype(BF16)
                p_ref[rows, c * LANES:(c + 1) * LANES] = pc

    def values(self, j, par):
        start = pl.multiple_of(j * self.tq, self.tq)
        vb = self.v_ref[pl.ds(start, self.tq), :]
        alpha = self.a_refs[par][...]
        self.acc_ref[...] = (jnp.concatenate([alpha, alpha], axis=1) * self.acc_ref[...]
                             + jnp.dot(self.p_refs[par][...], vb, preferred_element_type=F32))

    def _write(self, qi):
        acc = self.acc_ref
        self.o_ref[self._q_rows(qi), :] = (acc[:, 0:V_DIM] / acc[:, V_DIM:]).astype(self.o_ref.dtype)

    def first_block(self):
        qi = jnp.int32(0)
        self._init()
        self.scores(qi, qi, 0)
        self.softmax(0, True)
        self.values(qi, 0)
        self._write(qi)

    def fill(self, qi):
        self._init()
        self.scores(qi, 0, 0)
        self.scores(qi, 1, 1)
        self.softmax(0, False)
        if self.par == 0:
            self._tick(qi, 2, 0)

    def _tick(self, qi, t, par):
        self.scores(qi, t, par)
        self.softmax(1 - par, False)
        self.values(t - 2, par)

    def steady(self, qi):
        first = 3 if self.par == 0 else 2
        n_left = qi + 1 - first
        t0 = first
        for unroll in ATTN_UNROLLS:

            def unrolled(i, carry, t0=t0, unroll=unroll):
                for u in range(unroll):
                    self._tick(qi, t0 + unroll * i + u, (first + u) % 2)
                return carry

            lax.fori_loop(0, n_left // unroll, unrolled, 0)
            t0 = t0 + (n_left // unroll) * unroll
            n_left = n_left % unroll

    def drain(self, qi):
        self.softmax(self.par, True)
        self.values(qi - 1, 1 - self.par)
        self.values(qi, self.par)
        self._write(qi)


def mla_attention(q, k, v, tq=512):
    H, S, _ = q.shape
    buffer_set = ([pltpu.VMEM((tq, tq), F32)] * 2
                  + [pltpu.VMEM((tq, tq), BF16)] * 2
                  + [pltpu.VMEM((tq, LANES), F32)] * 2
                  + [pltpu.VMEM((tq, LANES), F32),
                     pltpu.VMEM((tq, 2 * V_DIM), F32)])
    return pl.pallas_call(
        functools.partial(_attn_kernel, tq=tq),
        grid=(H,),
        in_specs=[pl.BlockSpec((1, S, QK_PAD), lambda h: (h, 0, 0)),
                  pl.BlockSpec((1, S, QK_PAD), lambda h: (h, 0, 0)),
                  pl.BlockSpec((S, V_DIM), lambda h: (0, h))],
        out_specs=pl.BlockSpec((S, V_DIM), lambda h: (0, h)),
        out_shape=jax.ShapeDtypeStruct((S, H * V_DIM), BF16),
        scratch_shapes=[pltpu.VMEM((S, 2 * V_DIM), BF16)] + buffer_set + buffer_set,
        name="mla_attention",
        compiler_params=_cparams("arbitrary"),
    )(q, k, v)


CONV_RB = 64
CONV_CB = LANES


def _conv_kernel(cur_ref, halo_ref, w_ref, b_ref, g_ref, beta_ref, o_ref, buf_ref, sh_ref, acc_ref, wb_ref, *, tm):
    i = pl.program_id(0)

    @pl.when(i == 0)
    def _():
        for k in range(CONV_K):
            wb_ref[k] = jnp.broadcast_to(w_ref[k:k + 1, :], (SUBLANES, D_CONV))

    buf_ref[0:CONV_HALO, :] = jnp.where(i > 0, halo_ref[...], 0.0)
    buf_ref[CONV_HALO:CONV_HALO + tm, :] = cur_ref[...]
    n_buf = tm + CONV_HALO
    n_sh = n_buf - SUBLANES
    for c0 in range(0, D_CONV, LANES):
        x = buf_ref[:, c0:c0 + LANES]
        for b in range(1, SUBLANES):
            sh_ref[b - 1, :, c0:c0 + LANES] = pltpu.roll(x, n_buf - b, 0)[0:n_sh, :]
    off = CONV_HALO - (CONV_K - 1)
    for c0 in range(0, D_CONV, CONV_CB):
        cols = slice(c0, c0 + CONV_CB)
        w_all = wb_ref[:, :, cols]
        bias = jnp.broadcast_to(b_ref[:, cols], (CONV_RB, CONV_CB))

        def row_block(rb, carry, cols=cols, w_all=w_all, bias=bias):
            r0 = pl.multiple_of(rb * CONV_RB, CONV_RB)
            acc = bias
            for b in range(SUBLANES):
                taps = [k for k in range(CONV_K) if (off + k) % SUBLANES == b]
                a_lo, a_hi = (off + taps[0]) // SUBLANES, (off + taps[-1]) // SUBLANES
                rows = pl.ds(r0 + a_lo * SUBLANES, (a_hi - a_lo) * SUBLANES + CONV_RB)
                slab = buf_ref[rows, cols] if b == 0 else sh_ref[b - 1, rows, cols]
                for k in taps:
                    s0 = ((off + k) // SUBLANES - a_lo) * SUBLANES
                    wk = jnp.concatenate([w_all[k]] * (CONV_RB // SUBLANES), axis=0)
                    acc = acc + slab[s0:s0 + CONV_RB, :] * wk
            acc_ref[pl.ds(r0, CONV_RB), cols] = acc
            return carry

        lax.fori_loop(0, tm // CONV_RB, row_block, 0)
    y = acc_ref[...]
    mu = jnp.mean(y, axis=-1, keepdims=True)
    yc = y - mu
    yn = yc * lax.rsqrt(jnp.mean(yc * yc, axis=-1, keepdims=True) + EPS) * g_ref[...] + beta_ref[...]
    o_ref[...] = (yn * jax.nn.sigmoid(yn)).astype(o_ref.dtype)


def conv_ln_silu(hglu, conv_w, conv_b3, ln_g3, ln_b3, layer, tm=256):
    S, D = hglu.shape
    row = _layer_row(D, layer)
    return pl.pallas_call(
        functools.partial(_conv_kernel, tm=tm),
        grid=(S // tm,),
        in_specs=[pl.BlockSpec((tm, D), lambda i: (i, 0)),
                  pl.BlockSpec((CONV_HALO, D), lambda i: (jnp.maximum(i * (tm // CONV_HALO) - 1, 0), 0)),
                  pl.BlockSpec((None, CONV_K, D), lambda i: (layer, 0, 0)),
                  row, row, row],
        out_specs=pl.BlockSpec((tm, D), lambda i: (i, 0)),
        out_shape=jax.ShapeDtypeStruct((S, D), BF16),
        scratch_shapes=[pltpu.VMEM((CONV_HALO + tm, D), F32),
                        pltpu.VMEM((SUBLANES - 1, CONV_HALO + tm - SUBLANES, D), F32),
                        pltpu.VMEM((tm, D), F32),
                        pltpu.VMEM((CONV_K, SUBLANES, D), F32)],
        name="conv_ln_silu",
        compiler_params=_cparams("arbitrary"),
    )(hglu, hglu, conv_w, conv_b3, ln_g3, ln_b3)


def _memkv_kernel(mem_ref, g_ref, w_ref, k_ref, v_ref):
    mn = _rms(mem_ref[...], g_ref[...]).astype(BF16)
    kv = jnp.dot(mn, w_ref[...], preferred_element_type=F32)
    n = X_HEADS * X_HEAD_DIM
    k_ref[...] = kv[:, :n].astype(k_ref.dtype)
    v_ref[...] = kv[:, n:].astype(v_ref.dtype)


def mem_kv(mem, g3, w, layer):
    n = X_HEADS * X_HEAD_DIM
    D = mem.shape[1]
    full = lambda shape: pl.BlockSpec(shape, lambda i: (0, 0))
    return pl.pallas_call(
        _memkv_kernel,
        grid=(1,),
        in_specs=[full(mem.shape), _layer_row(D, layer),
                  pl.BlockSpec((None, D, 2 * n), lambda i: (layer, 0, 0))],
        out_specs=[full((N_MEM, n)), full((N_MEM, n))],
        out_shape=[jax.ShapeDtypeStruct((N_MEM, n), BF16)] * 2,
        name="mem_kv",
        compiler_params=_cparams("arbitrary"),
    )(mem, g3, w)


def _mix_cross_kernel(m_ref, x_ref, wo_ref, gmem_ref, wq_ref, k_ref, v_ref, wxo_ref, gffn_ref, xo_ref, ho_ref):
    x1 = x_ref[...] + jnp.dot(m_ref[...], wo_ref[...], preferred_element_type=F32)
    hm = _rms(x1, gmem_ref[...]).astype(BF16)
    q = jnp.dot(hm, wq_ref[...], preferred_element_type=F32) * (X_HEAD_DIM ** -0.5)
    q = q.astype(BF16)
    outs = []
    for h in range(X_HEADS):
        sl = slice(h * X_HEAD_DIM, (h + 1) * X_HEAD_DIM)
        s = lax.dot_general(q[:, sl], k_ref[:, sl], (((1,), (1,)), ((), ())), preferred_element_type=F32)
        p = jnp.exp(s - jnp.max(s, axis=-1, keepdims=True))
        l = jnp.sum(p, axis=-1, keepdims=True)
        o = jnp.dot(p.astype(BF16), v_ref[:, sl], preferred_element_type=F32) / l
        outs.append(o.astype(BF16))
    o = jnp.concatenate(outs, axis=-1)
    x2 = x1 + jnp.dot(o, wxo_ref[...], preferred_element_type=F32)
    xo_ref[...] = x2
    ho_ref[...] = _rms(x2, gffn_ref[...]).astype(ho_ref.dtype)


def mix_cross(merged, x, w_out, g_mem3, w_xq, mk, mv, w_xo, g_ffn3, layer, tm=512):
    M, D = merged.shape
    n = X_HEADS * X_HEAD_DIM
    const = lambda shape: pl.BlockSpec(shape, lambda i: (0, 0))
    rows = pl.BlockSpec((tm, D), lambda i: (i, 0))
    once = pl.Buffered(1)
    weight = lambda k, c: pl.BlockSpec((None, k, c), lambda i: (layer, 0, 0), pipeline_mode=once)
    return pl.pallas_call(
        _mix_cross_kernel,
        grid=(M // tm,),
        in_specs=[rows, rows, weight(D, D), _layer_row(D, layer), weight(D, n),
                  const((N_MEM, n)), const((N_MEM, n)), weight(n, D), _layer_row(D, layer)],
        out_specs=[rows, rows],
        out_shape=[jax.ShapeDtypeStruct((M, D), F32), jax.ShapeDtypeStruct((M, D), BF16)],
        name="mix_cross",
        compiler_params=_cparams("arbitrary"),
    )(merged, x, w_out, g_mem3, w_xq, mk, mv, w_xo, g_ffn3)


def _mlp_kernel(h_ref, w1_ref, w2_ref, x_ref, g_ref, xo_ref, ho_ref):
    f = pl.program_id(1)

    @pl.when(f == 0)
    def _():
        xo_ref[...] = x_ref[...]

    a = jnp.dot(h_ref[...], w1_ref[...], preferred_element_type=F32)
    a = jnp.square(jnp.maximum(a, 0.0)).astype(BF16)
    xo_ref[...] += jnp.dot(a, w2_ref[...], preferred_element_type=F32)

    @pl.when(f == pl.num_programs(1) - 1)
    def _():
        ho_ref[...] = _rms(xo_ref[...], g_ref[...]).astype(ho_ref.dtype)


def mlp(h, w1, w2, layer, x, g3, g_layer, out_dtype, tm=512, tf=1024):
    M, D = h.shape
    F = w1.shape[2]
    rows = pl.BlockSpec((tm, D), lambda i, f: (i, 0))
    return pl.pallas_call(
        _mlp_kernel,
        grid=(M // tm, F // tf),
        in_specs=[rows,
                  pl.BlockSpec((None, D, tf), lambda i, f: (layer, 0, f)),
                  pl.BlockSpec((None, tf, D), lambda i, f: (layer, f, 0)),
                  rows,
                  _layer_row(D, g_layer)],
        out_specs=[rows, rows],
        out_shape=[jax.ShapeDtypeStruct((M, D), F32), jax.ShapeDtypeStruct((M, D), out_dtype)],
        name="mlp",
        compiler_params=_cparams("arbitrary", "arbitrary"),
    )(h, w1, w2, x, g3)


def _permute_q_weight(w_uq):
    L, K, _ = w_uq.shape
    half = QK_ROPE // 2
    w = w_uq.reshape(L, K, MLA_HEADS, QK_NOPE + QK_ROPE)
    pe = w[..., QK_NOPE:]
    w = jnp.concatenate([w, pe[..., half:], pe[..., :half]], axis=-1)
    return w.reshape(L, K, MLA_HEADS * QK_PAD)


def kernel(x, mem, positions, norm_mix_g, w_in, b_gate, conv_w, conv_b, conv_ln_g, conv_ln_b, w_conv_out, q_norm_g, w_uq, kv_norm_g, w_ukv, w_mla_out, w_out, norm_mem_g, mem_norm_g, w_xq, w_xkv, w_xo, norm_ffn_g, w_ff1, w_ff2, final_norm_g):
    B, S, D = x.shape
    assert (B, S, D) == (1, SEQ, D_MODEL)
    x2d = x.reshape(S, D)
    mem2d = mem.reshape(N_MEM, D)
    vec3 = lambda v: v.reshape(v.shape[0], 1, v.shape[1])

    inv_freq = 1.0 / (ROPE_THETA ** (jnp.arange(0, QK_ROPE, 2, dtype=F32) / QK_ROPE))
    cos, sin = rope_tables(positions.reshape(S, 1), jnp.tile(inv_freq, 4).reshape(1, LANES))

    w_in_lo = jnp.swapaxes(w_in, 1, 2).astype(BF16)
    w_q = _permute_q_weight(w_uq.astype(BF16))
    w_kv, w_co, w_mo, w_o = (w.astype(BF16) for w in (w_ukv, w_conv_out, w_mla_out, w_out))
    w_q_x, w_kv_x, w_o_x = (w.astype(BF16) for w in (w_xq, w_xkv, w_xo))
    w_1, w_2 = w_ff1.astype(BF16), w_ff2.astype(BF16)

    g_mix, g_mem, g_memn, g_ffn = vec3(norm_mix_g), vec3(norm_mem_g), vec3(mem_norm_g), vec3(norm_ffn_g)
    g_q, g_kv, b_g = vec3(q_norm_g), vec3(kv_norm_g), vec3(b_gate)
    c_b, ln_g, ln_b = vec3(conv_b), vec3(conv_ln_g), vec3(conv_ln_b)
    g_final = final_norm_g.reshape(1, 1, D)

    h = rms_norm_rows(x2d, g_mix, 0, BF16)
    xcur = x2d
    for l in range(DEPTH):
        (hglu,) = matmul("glu_proj", [h], [(0, w_in_lo, l, COL_GLU_A, True), (0, w_in_lo, l, COL_GLU_G, True)],
                         [], [F32], _epi_glu, D_CONV, 1024, 1024)

        hc = conv_ln_silu(hglu, conv_w, c_b, ln_g, ln_b, l)
        (gy,) = matmul("conv_out_gated", [hc, h], [(0, w_co, l, 0, False), (1, w_in_lo, l, COL_GATE, True)],
                       [("row", b_g, l, 0)], [BF16], _epi_gated, D, 1024, 512)

        q, k, v = mla_proj(h, w_in_lo, w_q, w_kv, l, g_q, g_kv, cos, sin)
        o = mla_attention(q, k, v)
        (merged,) = matmul("mla_out_gated_merge", [o, h],
                           [(0, w_mo, l, 0, False), (1, w_in_lo, l, COL_GATE + D, True)],
                           [("row", b_g, l, D), ("tile", gy, 0)], [BF16], _epi_gated_merge, D, 1024, 512)

        mk, mv = mem_kv(mem2d, g_memn, w_kv_x, l)
        xcur, hf = mix_cross(merged, xcur, w_o, g_mem, w_q_x, mk, mv, w_o_x, g_ffn, l)

        last = l == DEPTH - 1
        xcur, h = mlp(hf, w_1, w_2, l, xcur, g_final if last else g_mix, 0 if last else l + 1,
                      F32 if last else BF16)
    return h.reshape(B, S, D)
```
